```python
import math
import jax, jax.numpy as jnp
from jax import lax
import numpy as np

D_MODEL = 1024
BATCH = 2
SEQ = 8192
DEPTH = 2
DEC_BATCH = 8
DEC_SEQ = 2048
PAST_LEN = 128

GRID_W = 64
HEAD_DIM = 64
D_RWKV = D_MODEL // 2
RWKV_HEADS = D_RWKV // HEAD_DIM
D_NA = D_MODEL // 4
NA_HEADS = D_NA // HEAD_DIM
D_HY = D_MODEL // 4
N_BRANCH = 3
N_IN = 3 * D_RWKV + 3 * D_NA + 3 * D_HY + N_BRANCH * D_MODEL
DECAY_LORA = 32
AAA_LORA = 32
GATE_LORA = 96
GN_EPS = 64e-5
NA_WIN_ROWS = 8
NA_WIN_COLS = 16
HY_ORDER = 2
HY_EMB = 33
HY_FFN = 64
HY_SHORT = 3
HY_TOL = 1e-2
HY_FAST_PCT = 0.3
HY_SLOW_PCT = 1.5
N_EXPERTS = 16
EXPERT_FF = 2048
CAPACITY_FACTOR = 2
ALPHA = (2 * DEPTH) ** 0.25
BETA = (8 * DEPTH) ** -0.25
LN_EPS = 1e-5

kernel_name = 'hybrid_rwkv7_natten_hyena_ec_encoder'


def _ln(x, g, b, eps=LN_EPS):
    xf = x.astype(jnp.float32)
    xc = xf - jnp.mean(xf, -1, keepdims=True)
    var = jnp.mean(xc * xc, -1, keepdims=True)
    return (xc * lax.rsqrt(var + eps) * g.astype(jnp.float32) + b.astype(jnp.float32)).astype(x.dtype)


def _prev(x):
    return jnp.pad(x, ((0, 0), (1, 0), (0, 0)))[:, :-1]


def _next(x):
    return jnp.pad(x, ((0, 0), (0, 1), (0, 0)))[:, 1:]


def _bishift(x, mu):
    return x + mu[0] * (_prev(x) - x) + mu[1] * (_next(x) - x)


def _rwkv7_step(state, inp):
    r, w, k, v, a, b = inp
    sa = jnp.einsum('zbhvk,zbhk->zbhv', state, a)
    state = state * w[..., None, :] + sa[..., :, None] * b[..., None, :] + v[..., :, None] * k[..., None, :]
    y = jnp.einsum('zbhvk,zbhk->zbhv', state, r)
    return state, y


def rwkv7_mixer(u, r, k, v, p):
    f32 = jnp.float32
    B, T, _ = u.shape
    H, N, C = RWKV_HEADS, HEAD_DIM, D_RWKV
    r = _bishift(r, p['rwkv_mu_rkv'][0])
    k = _bishift(k, p['rwkv_mu_rkv'][1])
    v = _bishift(v, p['rwkv_mu_rkv'][2])
    xw = _bishift(u, p['rwkv_mu_x'][0])
    xa = _bishift(u, p['rwkv_mu_x'][1])
    xg = _bishift(u, p['rwkv_mu_x'][2])
    w_raw = p['rwkv_w0'][:, None, None, :] + jnp.einsum(
        'zbtr,zrc->zbtc', jnp.tanh(jnp.einsum('btd,zdr->zbtr', xw, p['rwkv_w1'])), p['rwkv_w2'])
    decay = jnp.exp(-jnp.exp(-jax.nn.softplus(-w_raw.astype(f32)) - 0.5))
    a = jax.nn.sigmoid((p['rwkv_a0'][:, None, None, :] + jnp.einsum(
        'zbtr,zrc->zbtc', jnp.einsum('btd,zdr->zbtr', xa, p['rwkv_a1']), p['rwkv_a2'])).astype(f32))
    g = jnp.einsum('btr,rc->btc', jax.nn.sigmoid(xg @ p['rwkv_g1']), p['rwkv_g2']).astype(f32)
    rf, kf, vf = r.astype(f32), k.astype(f32), v.astype(f32)
    kk = (kf * p['rwkv_k_k'].astype(f32)).reshape(B, T, H, N)
    kk = (kk / jnp.maximum(jnp.sqrt(jnp.sum(kk * kk, -1, keepdims=True)), 1e-12)).reshape(B, T, C)
    k_dir = kf[None] * (1.0 + (a - 1.0) * p['rwkv_k_a'].astype(f32))

    def stack_dirs(fw, bw):
        s = jnp.stack([fw, jnp.flip(bw, 1)]).reshape(2, B, T, H, N)
        return jnp.moveaxis(s, 2, 0)

    xs = (stack_dirs(rf, rf), stack_dirs(decay[0], decay[1]), stack_dirs(k_dir[0], k_dir[1]),
          stack_dirs(vf, vf), stack_dirs(-kk, -kk), stack_dirs(kk * a[0], kk * a[1]))
    s0 = jnp.zeros((2, B, H, N, N), f32)
    _, ys = lax.scan(_rwkv7_step, s0, xs)
    y = jnp.moveaxis(ys[:, 0] + jnp.flip(ys[:, 1], 0), 0, 1)
    yc = y - jnp.mean(y, -1, keepdims=True)
    y = yc * lax.rsqrt(jnp.mean(yc * yc, -1, keepdims=True) + GN_EPS)
    y = y.reshape(B, T, C) * p['rwkv_lnx_g'].astype(f32) + p['rwkv_lnx_b'].astype(f32)
    bonus = jnp.sum(rf.reshape(1, B, T, H, N) * k_dir.reshape(2, B, T, H, N) * p['rwkv_r_k'].astype(f32),
                    -1, keepdims=True) * vf.reshape(1, B, T, H, N)
    y = y + jnp.sum(bonus, 0).reshape(B, T, C)
    return (y * g).astype(u.dtype)


def neighbourhood_attention(q, k, v, rpb):
    f32 = jnp.float32
    B, T, _ = q.shape
    rows = T // GRID_W
    kr = min(NA_WIN_ROWS, rows)
    kc = NA_WIN_COLS
    H, N = NA_HEADS, HEAD_DIM
    qg = q.reshape(B, rows, GRID_W, H, N)
    kg = k.reshape(B, rows, GRID_W, H, N)
    vg = v.reshape(B, rows, GRID_W, H, N)
    ri = jnp.arange(rows)
    row_idx = jnp.clip(ri - kr // 2, 0, rows - kr)[:, None] + jnp.arange(kr)[None]
    ci = jnp.arange(GRID_W)
    col_start = jnp.clip(ci - kc // 2, 0, GRID_W - kc)
    col_valid = (ci[None] >= col_start[:, None]) & (ci[None] < col_start[:, None] + kc)
    dr_idx = row_idx - ri[:, None] + NA_WIN_ROWS - 1
    dc_idx = jnp.clip(ci[None] - ci[:, None] + NA_WIN_COLS - 1, 0, 2 * NA_WIN_COLS - 2)
    bias = rpb.astype(f32)[:, dr_idx[:, None, :, None], dc_idx[None, :, None, :]]
    k_rows = kg[:, row_idx]
    v_rows = vg[:, row_idx]
    s = jnp.einsum('bichd,birshd->bhicrs', qg, k_rows).astype(f32) * (N ** -0.5) + bias[None]
    s = jnp.where(col_valid[:, None, :], s, -1e30)
    pr = jax.nn.softmax(s.reshape(B, H, rows, GRID_W, kr * GRID_W), -1).reshape(s.shape).astype(v.dtype)
    o = jnp.einsum('bhicrs,birshd->bichd', pr, v_rows)
    return o.reshape(B, T, D_NA)


def hyena_filter_spectra(L, p):
    f32 = jnp.float32
    t = jnp.linspace(0.0, 1.0, L, dtype=f32)[:, None]
    n_bands = (HY_EMB - 1) // 2
    omega = 2.0 * math.pi * jnp.arange(L, dtype=f32)[:, None] / L
    bands = jnp.linspace(1e-4, n_bands - 1, n_bands, dtype=f32)[None]
    z = jnp.concatenate([t, jnp.cos(bands * omega), -jnp.sin(bands * omega)], -1)
    freq = p['hy_freq'].astype(f32)
    h = jnp.sin(freq * (z @ p['hy_w1'].astype(f32) + p['hy_b1'].astype(f32)))
    h = jnp.sin(freq * (h @ p['hy_w2'].astype(f32) + p['hy_b2'].astype(f32)))
    h = jnp.sin(freq * (h @ p['hy_w3'].astype(f32) + p['hy_b3'].astype(f32)))
    h = (h @ p['hy_w4'].astype(f32)).reshape(L, HY_ORDER, 2, D_HY)
    deltas = jnp.abs(jnp.linspace(math.log(HY_TOL) / HY_FAST_PCT, math.log(HY_TOL) / HY_SLOW_PCT, D_HY, dtype=f32))
    h = h * jnp.exp(-t * deltas)[:, None, None, :]
    h_f, h_b = h[:, :, 0], h[:, :, 1]
    kern = jnp.concatenate([h_f, jnp.zeros((1, HY_ORDER, D_HY), f32), h_b[1:][::-1]], 0)
    kern = kern / jnp.sum(jnp.abs(kern), 0, keepdims=True)
    return jnp.fft.rfft(kern, axis=0)


def _long_conv(z, spec, bias):
    L = z.shape[1]
    zf = z.astype(jnp.float32)
    y = jnp.fft.irfft(jnp.fft.rfft(zf, n=2 * L, axis=1) * spec[None], n=2 * L, axis=1)[:, :L]
    return (y + zf * bias.astype(jnp.float32)).astype(z.dtype)


def hyena_mixer(x1, x2, v, p):
    T = v.shape[1]
    u = jnp.concatenate([x1, x2, v], -1)
    sw = p['hy_short_w']
    u = _prev(u) * sw[0] + u * sw[1] + _next(u) * sw[2] + p['hy_short_b']
    x1, x2, v = jnp.split(u, 3, axis=-1)
    spec = hyena_filter_spectra(T, p)
    z = v
    for o, gate in enumerate((x1, x2)):
        z = gate * _long_conv(z, spec[:, o], p['hy_bias'][o])
    return z


def expert_choice_moe(x, w_router, w_gate, w_up, w_down):
    b, t, d = x.shape
    n = b * t
    xt = x.reshape(n, d)
    cap = (CAPACITY_FACTOR * n) // N_EXPERTS
    aff = jax.nn.softmax((xt @ w_router).astype(jnp.float32), axis=-1)
    gate, idx = lax.top_k(aff.T, cap)
    xe = jnp.take(xt, idx, axis=0)
    h = jax.nn.silu(jnp.einsum('ecd,edf->ecf', xe, w_gate)) * jnp.einsum('ecd,edf->ecf', xe, w_up)
    ye = jnp.einsum('ecf,efd->ecd', h, w_down) * gate[..., None].astype(x.dtype)
    out = jnp.zeros_like(xt).at[idx.reshape(-1)].add(ye.reshape(-1, d))
    return out.reshape(b, t, d)


def encoder_layer(x, p):
    B, T, _ = x.shape
    proj = x @ p['w_in']
    sizes = [D_RWKV] * 3 + [D_NA] * 3 + [D_HY] * 3
    cuts = [int(c) for c in np.cumsum(sizes)]
    ra, ka, va, qb, kb, vb, x1c, x2c, vc, gates = jnp.split(proj, cuts, axis=-1)
    y_a = rwkv7_mixer(x, ra, ka, va, p)
    y_b = neighbourhood_attention(qb, kb, vb, p['na_rpb'])
    y_c = hyena_mixer(x1c, x2c, vc, p)
    g = jax.nn.sigmoid(gates).reshape(B, T, N_BRANCH, D_MODEL)
    m = (g[:, :, 0] * (y_a @ p['w_branch_a']) + g[:, :, 1] * (y_b @ p['w_branch_b'])
         + g[:, :, 2] * (y_c @ p['w_branch_c']))
    x = _ln(ALPHA * x + m @ p['w_out'], p['ln1_g'], p['ln1_b'])
    ffn = expert_choice_moe(x, p['w_router'], p['w_exp_gate'], p['w_exp_up'], p['w_exp_down'])
    return _ln(ALPHA * x + ffn, p['ln2_g'], p['ln2_b'])


def trunk(x, ln_in_g, ln_in_b, stacked):
    x = _ln(x, ln_in_g, ln_in_b)
    for l in range(DEPTH):
        x = encoder_layer(x, {name: arr[l] for name, arr in stacked.items()})
    return x


def setup_inputs(seed: int = 0) -> dict:
    key = jax.random.key(seed)
    ks = iter(jax.random.split(key, 64))
    nrm = lambda shape, scale: scale * jax.random.normal(next(ks), shape, jnp.float32)
    uni = lambda shape, lo, hi: jax.random.uniform(next(ks), shape, jnp.float32, lo, hi)
    L, D, F, E = DEPTH, D_MODEL, HY_FFN, N_EXPERTS
    return {
        'x_prompt': nrm((BATCH, SEQ, D), 1.0),
        'x_sample': nrm((DEC_BATCH, DEC_SEQ, D), 1.0),
        'ln_in_g': 1.0 + nrm((D,), 0.02),
        'ln_in_b': nrm((D,), 0.02),
        'w_in': nrm((L, D, N_IN), D ** -0.5),
        'rwkv_mu_rkv': uni((L, 3, 2, D_RWKV), 0.0, 0.5),
        'rwkv_mu_x': uni((L, 3, 2, D), 0.0, 0.5),
        'rwkv_w0': uni((L, 2, D_RWKV), -6.5, -1.5),
        'rwkv_w1': nrm((L, 2, D, DECAY_LORA), D ** -0.5),
        'rwkv_w2': nrm((L, 2, DECAY_LORA, D_RWKV), 0.5 * DECAY_LORA ** -0.5),
        'rwkv_a0': nrm((L, 2, D_RWKV), 0.1),
        'rwkv_a1': nrm((L, 2, D, AAA_LORA), D ** -0.5),
        'rwkv_a2': nrm((L, 2, AAA_LORA, D_RWKV), AAA_LORA ** -0.5),
        'rwkv_g1': nrm((L, D, GATE_LORA), D ** -0.5),
        'rwkv_g2': nrm((L, GATE_LORA, D_RWKV), GATE_LORA ** -0.5),
        'rwkv_k_k': 0.85 + nrm((L, D_RWKV), 0.02),
        'rwkv_k_a': 1.0 + nrm((L, D_RWKV), 0.02),
        'rwkv_r_k': nrm((L, RWKV_HEADS, HEAD_DIM), 0.1),
        'rwkv_lnx_g': 1.0 + nrm((L, D_RWKV), 0.02),
        'rwkv_lnx_b': nrm((L, D_RWKV), 0.02),
        'na_rpb': nrm((L, NA_HEADS, 2 * NA_WIN_ROWS - 1, 2 * NA_WIN_COLS - 1), 0.02),
        'hy_short_w': nrm((L, HY_SHORT, 3 * D_HY), HY_SHORT ** -0.5),
        'hy_short_b': nrm((L, 3 * D_HY), 0.02),
        'hy_w1': nrm((L, HY_EMB, F), HY_EMB ** -0.5),
        'hy_b1': nrm((L, F), 0.02),
        'hy_w2': nrm((L, F, F), F ** -0.5),
        'hy_b2': nrm((L, F), 0.02),
        'hy_w3': nrm((L, F, F), F ** -0.5),
        'hy_b3': nrm((L, F), 0.02),
        'hy_w4': nrm((L, F, HY_ORDER * 2 * D_HY), F ** -0.5),
        'hy_freq': 1.0 + nrm((L, F), 0.1),
        'hy_bias': nrm((L, HY_ORDER, D_HY), 0.5),
        'w_branch_a': nrm((L, D_RWKV, D), BETA * D_RWKV ** -0.5),
        'w_branch_b': nrm((L, D_NA, D), BETA * D_NA ** -0.5),
        'w_branch_c': nrm((L, D_HY, D), BETA * D_HY ** -0.5),
        'w_out': nrm((L, D, D), BETA * D ** -0.5),
        'ln1_g': 1.0 + nrm((L, D), 0.02),
        'ln1_b': nrm((L, D), 0.02),
        'w_router': nrm((L, D, E), D ** -0.5),
        'w_exp_gate': nrm((L, E, D, EXPERT_FF), D ** -0.5),
        'w_exp_up': nrm((L, E, D, EXPERT_FF), D ** -0.5),
        'w_exp_down': nrm((L, E, EXPERT_FF, D), BETA * EXPERT_FF ** -0.5),
        'ln2_g': 1.0 + nrm((L, D), 0.02),
        'ln2_b': nrm((L, D), 0.02),
    }


def reference(x_prompt, x_sample, ln_in_g, ln_in_b, w_in, rwkv_mu_rkv, rwkv_mu_x, rwkv_w0, rwkv_w1, rwkv_w2,
              rwkv_a0, rwkv_a1, rwkv_a2, rwkv_g1, rwkv_g2, rwkv_k_k, rwkv_k_a, rwkv_r_k, rwkv_lnx_g, rwkv_lnx_b,
              na_rpb, hy_short_w, hy_short_b, hy_w1, hy_b1, hy_w2, hy_b2, hy_w3, hy_b3, hy_w4, hy_freq, hy_bias,
              w_branch_a, w_branch_b, w_branch_c, w_out, ln1_g, ln1_b, w_router, w_exp_gate, w_exp_up,
              w_exp_down, ln2_g, ln2_b):
    stacked = {
        'w_in': w_in, 'rwkv_mu_rkv': rwkv_mu_rkv, 'rwkv_mu_x': rwkv_mu_x, 'rwkv_w0': rwkv_w0,
        'rwkv_w1': rwkv_w1, 'rwkv_w2': rwkv_w2, 'rwkv_a0': rwkv_a0, 'rwkv_a1': rwkv_a1, 'rwkv_a2': rwkv_a2,
        'rwkv_g1': rwkv_g1, 'rwkv_g2': rwkv_g2, 'rwkv_k_k': rwkv_k_k, 'rwkv_k_a': rwkv_k_a,
        'rwkv_r_k': rwkv_r_k, 'rwkv_lnx_g': rwkv_lnx_g, 'rwkv_lnx_b': rwkv_lnx_b, 'na_rpb': na_rpb,
        'hy_short_w': hy_short_w, 'hy_short_b': hy_short_b, 'hy_w1': hy_w1, 'hy_b1': hy_b1,
        'hy_w2': hy_w2, 'hy_b2': hy_b2, 'hy_w3': hy_w3, 'hy_b3': hy_b3, 'hy_w4': hy_w4,
        'hy_freq': hy_freq, 'hy_bias': hy_bias, 'w_branch_a': w_branch_a, 'w_branch_b': w_branch_b,
        'w_branch_c': w_branch_c, 'w_out': w_out, 'ln1_g': ln1_g, 'ln1_b': ln1_b, 'w_router': w_router,
        'w_exp_gate': w_exp_gate, 'w_exp_up': w_exp_up, 'w_exp_down': w_exp_down,
        'ln2_g': ln2_g, 'ln2_b': ln2_b,
    }
    y_prompt = trunk(x_prompt, ln_in_g, ln_in_b, stacked)
    y_sample = trunk(x_sample, ln_in_g, ln_in_b, stacked)
    return (y_prompt, y_sample)
```

```python
import functools
import math

import jax
import jax.numpy as jnp
import numpy as np
from jax import lax
from jax.experimental import pallas as pl
from jax.experimental.pallas import tpu as pltpu

D_MODEL = 1024
DEPTH = 2
GRID_W = 64
HEAD_DIM = 64
D_RWKV = D_MODEL // 2
RWKV_HEADS = D_RWKV // HEAD_DIM
D_NA = D_MODEL // 4
NA_HEADS = D_NA // HEAD_DIM
D_HY = D_MODEL // 4
N_BRANCH = 3
GN_EPS = 64e-5
NA_WIN_ROWS = 8
NA_WIN_COLS = 16
HY_ORDER = 2
HY_EMB = 33
HY_TOL = 1e-2
HY_FAST_PCT = 0.3
HY_SLOW_PCT = 1.5
N_EXPERTS = 16
CAPACITY_FACTOR = 2
ALPHA = (2 * DEPTH) ** 0.25
LN_EPS = 1e-5

F32 = jnp.float32
BF16 = jnp.bfloat16


def _mm_body(x_ref, w_ref, o_ref):
    o_ref[...] = jnp.dot(x_ref[...], w_ref[...], preferred_element_type=F32).astype(o_ref.dtype)


def _matmul(x, w, *, tm=1024, tn=512, out_dtype=F32):
    m, k = x.shape
    _, n = w.shape
    tm = min(tm, m)
    tn = min(tn, n)
    assert m % tm == 0 and n % tn == 0
    return pl.pallas_call(
        _mm_body,
        out_shape=jax.ShapeDtypeStruct((m, n), out_dtype),
        grid=(m // tm, n // tn),
        in_specs=[pl.BlockSpec((tm, k), lambda i, j: (i, 0)),
                  pl.BlockSpec((k, tn), lambda i, j: (0, j))],
        out_specs=pl.BlockSpec((tm, tn), lambda i, j: (i, j)),
        name="dense_matmul",
    )(x.astype(BF16), w.astype(BF16))


def _bmm_body(x_ref, w_ref, o_ref):
    o_ref[0] = jnp.dot(x_ref[0], w_ref[0], preferred_element_type=F32).astype(o_ref.dtype)


def _expert_matmul(x, w, *, tm=1024, tn=512, out_dtype=F32):
    e, m, k = x.shape
    n = w.shape[-1]
    tm = min(tm, m)
    tn = min(tn, n)
    assert m % tm == 0 and n % tn == 0
    return pl.pallas_call(
        _bmm_body,
        out_shape=jax.ShapeDtypeStruct((e, m, n), out_dtype),
        grid=(e, m // tm, n // tn),
        in_specs=[pl.BlockSpec((1, tm, k), lambda g, i, j: (g, i, 0)),
                  pl.BlockSpec((1, k, tn), lambda g, i, j: (g, 0, j))],
        out_specs=pl.BlockSpec((1, tm, tn), lambda g, i, j: (g, i, j)),
        name="expert_matmul",
    )(x.astype(BF16), w.astype(BF16))


def _ln(x, g, b, eps=LN_EPS):
    xc = x - jnp.mean(x, -1, keepdims=True)
    var = jnp.mean(xc * xc, -1, keepdims=True)
    return xc * lax.rsqrt(var + eps) * g + b


def _prev(x):
    return jnp.pad(x, ((0, 0), (1, 0), (0, 0)))[:, :-1]


def _next(x):
    return jnp.pad(x, ((0, 0), (0, 1), (0, 0)))[:, 1:]


def _bishift(x, mu):
    return x + mu[0] * (_prev(x) - x) + mu[1] * (_next(x) - x)


def _rwkv7_step(state, inp):
    r, w, k, v, a, b = inp
    sa = jnp.einsum('zbhvk,zbhk->zbhv', state, a)
    state = state * w[..., None, :] + sa[..., :, None] * b[..., None, :] + v[..., :, None] * k[..., None, :]
    y = jnp.einsum('zbhvk,zbhk->zbhv', state, r)
    return state, y


def _rwkv7_mixer(u, r, k, v, p):
    B, T, _ = u.shape
    H, N, C = RWKV_HEADS, HEAD_DIM, D_RWKV
    r = _bishift(r, p['rwkv_mu_rkv'][0])
    k = _bishift(k, p['rwkv_mu_rkv'][1])
    v = _bishift(v, p['rwkv_mu_rkv'][2])
    xw = _bishift(u, p['rwkv_mu_x'][0])
    xa = _bishift(u, p['rwkv_mu_x'][1])
    xg = _bishift(u, p['rwkv_mu_x'][2])
    w_raw = p['rwkv_w0'][:, None, None, :] + jnp.einsum(
        'zbtr,zrc->zbtc', jnp.tanh(jnp.einsum('btd,zdr->zbtr', xw, p['rwkv_w1'])), p['rwkv_w2'])
    decay = jnp.exp(-jnp.exp(-jax.nn.softplus(-w_raw) - 0.5))
    a = jax.nn.sigmoid(p['rwkv_a0'][:, None, None, :] + jnp.einsum(
        'zbtr,zrc->zbtc', jnp.einsum('btd,zdr->zbtr', xa, p['rwkv_a1']), p['rwkv_a2']))
    g = jnp.einsum('btr,rc->btc', jax.nn.sigmoid(xg @ p['rwkv_g1']), p['rwkv_g2'])
    kk = (k * p['rwkv_k_k']).reshape(B, T, H, N)
    kk = (kk / jnp.maximum(jnp.sqrt(jnp.sum(kk * kk, -1, keepdims=True)), 1e-12)).reshape(B, T, C)
    k_dir = k[None] * (1.0 + (a - 1.0) * p['rwkv_k_a'])

    def stack_dirs(fw, bw):
        s = jnp.stack([fw, jnp.flip(bw, 1)]).reshape(2, B, T, H, N)
        return jnp.moveaxis(s, 2, 0)

    xs = (stack_dirs(r, r), stack_dirs(decay[0], decay[1]), stack_dirs(k_dir[0], k_dir[1]),
          stack_dirs(v, v), stack_dirs(-kk, -kk), stack_dirs(kk * a[0], kk * a[1]))
    s0 = jnp.zeros((2, B, H, N, N), F32)
    _, ys = lax.scan(_rwkv7_step, s0, xs)
    y = jnp.moveaxis(ys[:, 0] + jnp.flip(ys[:, 1], 0), 0, 1)
    yc = y - jnp.mean(y, -1, keepdims=True)
    y = yc * lax.rsqrt(jnp.mean(yc * yc, -1, keepdims=True) + GN_EPS)
    y = y.reshape(B, T, C) * p['rwkv_lnx_g'] + p['rwkv_lnx_b']
    bonus = jnp.sum(r.reshape(1, B, T, H, N) * k_dir.reshape(2, B, T, H, N) * p['rwkv_r_k'],
                    -1, keepdims=True) * v.reshape(1, B, T, H, N)
    y = y + jnp.sum(bonus, 0).reshape(B, T, C)
    return y * g


def _neighbourhood_attention(q, k, v, rpb):
    B, T, _ = q.shape
    rows = T // GRID_W
    kr = min(NA_WIN_ROWS, rows)
    kc = NA_WIN_COLS
    H, N = NA_HEADS, HEAD_DIM
    qg = q.reshape(B, rows, GRID_W, H, N)
    kg = k.reshape(B, rows, GRID_W, H, N)
    vg = v.reshape(B, rows, GRID_W, H, N)
    ri = jnp.arange(rows)
    row_idx = jnp.clip(ri - kr // 2, 0, rows - kr)[:, None] + jnp.arange(kr)[None]
    ci = jnp.arange(GRID_W)
    col_start = jnp.clip(ci - kc // 2, 0, GRID_W - kc)
    col_valid = (ci[None] >= col_start[:, None]) & (ci[None] < col_start[:, None] + kc)
    dr_idx = row_idx - ri[:, None] + NA_WIN_ROWS - 1
    dc_idx = jnp.clip(ci[None] - ci[:, None] + NA_WIN_COLS - 1, 0, 2 * NA_WIN_COLS - 2)
    bias = rpb[:, dr_idx[:, None, :, None], dc_idx[None, :, None, :]]
    k_rows = kg[:, row_idx]
    v_rows = vg[:, row_idx]
    s = jnp.einsum('bichd,birshd->bhicrs', qg, k_rows) * (N ** -0.5) + bias[None]
    s = jnp.where(col_valid[:, None, :], s, -1e30)
    pr = jax.nn.softmax(s.reshape(B, H, rows, GRID_W, kr * GRID_W), -1).reshape(s.shape)
    o = jnp.einsum('bhicrs,birshd->bichd', pr, v_rows)
    return o.reshape(B, T, D_NA)


def _hyena_filter_spectra(L, p):
    t = jnp.linspace(0.0, 1.0, L, dtype=F32)[:, None]
    n_bands = (HY_EMB - 1) // 2
    omega = 2.0 * math.pi * jnp.arange(L, dtype=F32)[:, None] / L
    bands = jnp.linspace(1e-4, n_bands - 1, n_bands, dtype=F32)[None]
    z = jnp.concatenate([t, jnp.cos(bands * omega), -jnp.sin(bands * omega)], -1)
    freq = p['hy_freq']
    h = jnp.sin(freq * (z @ p['hy_w1'] + p['hy_b1']))
    h = jnp.sin(freq * (h @ p['hy_w2'] + p['hy_b2']))
    h = jnp.sin(freq * (h @ p['hy_w3'] + p['hy_b3']))
    h = (h @ p['hy_w4']).reshape(L, HY_ORDER, 2, D_HY)
    deltas = jnp.abs(jnp.linspace(math.log(HY_TOL) / HY_FAST_PCT, math.log(HY_TOL) / HY_SLOW_PCT, D_HY, dtype=F32))
    h = h * jnp.exp(-t * deltas)[:, None, None, :]
    h_f, h_b = h[:, :, 0], h[:, :, 1]
    kern = jnp.concatenate([h_f, jnp.zeros((1, HY_ORDER, D_HY), F32), h_b[1:][::-1]], 0)
    kern = kern / jnp.sum(jnp.abs(kern), 0, keepdims=True)
    return jnp.fft.rfft(kern, axis=0)


def _long_conv(z, spec, bias):
    L = z.shape[1]
    y = jnp.fft.irfft(jnp.fft.rfft(z, n=2 * L, axis=1) * spec[None], n=2 * L, axis=1)[:, :L]
    return y + z * bias


def _hyena_mixer(x1, x2, v, p):
    T = v.shape[1]
    u = jnp.concatenate([x1, x2, v], -1)
    sw = p['hy_short_w']
    u = _prev(u) * sw[0] + u * sw[1] + _next(u) * sw[2] + p['hy_short_b']
    x1, x2, v = jnp.split(u, 3, axis=-1)
    spec = _hyena_filter_spectra(T, p)
    z = v
    for o, gate in enumerate((x1, x2)):
        z = gate * _long_conv(z, spec[:, o], p['hy_bias'][o])
    return z


def _expert_choice_moe(x, w_router, w_gate, w_up, w_down):
    b, t, d = x.shape
    n = b * t
    xt = x.reshape(n, d)
    cap = (CAPACITY_FACTOR * n) // N_EXPERTS
    aff = jax.nn.softmax(jnp.dot(xt, w_router, precision=lax.Precision.HIGHEST), axis=-1)
    gate, idx = lax.top_k(aff.T, cap)
    xe = jnp.take(xt.astype(BF16), idx, axis=0)
    hg = _expert_matmul(xe, w_gate)
    hu = _expert_matmul(xe, w_up)
    h = jax.nn.silu(hg) * hu
    ye = _expert_matmul(h, w_down) * gate[..., None]
    out = jnp.zeros_like(xt).at[idx.reshape(-1)].add(ye.reshape(-1, d))
    return out.reshape(b, t, d)


def _encoder_layer(x, p):
    B, T, D = x.shape
    n = B * T
    proj = _matmul(x.reshape(n, D), p['w_in']).reshape(B, T, -1)
    sizes = [D_RWKV] * 3 + [D_NA] * 3 + [D_HY] * 3
    cuts = [int(c) for c in np.cumsum(sizes)]
    ra, ka, va, qb, kb, vb, x1c, x2c, vc, gates = jnp.split(proj, cuts, axis=-1)
    y_a = _rwkv7_mixer(x, ra, ka, va, p)
    y_b = _neighbourhood_attention(qb, kb, vb, p['na_rpb'])
    y_c = _hyena_mixer(x1c, x2c, vc, p)
    g = jax.nn.sigmoid(gates).reshape(B, T, N_BRANCH, D_MODEL)
    m = (g[:, :, 0] * _matmul(y_a.reshape(n, -1), p['w_branch_a']).reshape(B, T, D)
         + g[:, :, 1] * _matmul(y_b.reshape(n, -1), p['w_branch_b']).reshape(B, T, D)
         + g[:, :, 2] * _matmul(y_c.reshape(n, -1), p['w_branch_c']).reshape(B, T, D))
    x = _ln(ALPHA * x + _matmul(m.reshape(n, D), p['w_out']).reshape(B, T, D), p['ln1_g'], p['ln1_b'])
    ffn = _expert_choice_moe(x, p['w_router'], p['w_exp_gate'], p['w_exp_up'], p['w_exp_down'])
    return _ln(ALPHA * x + ffn, p['ln2_g'], p['ln2_b'])


def _trunk(x, ln_in_g, ln_in_b, stacked):
    x = _ln(x, ln_in_g, ln_in_b)
    for l in range(DEPTH):
        x = _encoder_layer(x, {name: arr[l] for name, arr in stacked.items()})
    return x


def kernel(x_prompt, x_sample, ln_in_g, ln_in_b, w_in, rwkv_mu_rkv, rwkv_mu_x, rwkv_w0, rwkv_w1, rwkv_w2,
           rwkv_a0, rwkv_a1, rwkv_a2, rwkv_g1, rwkv_g2, rwkv_k_k, rwkv_k_a, rwkv_r_k, rwkv_lnx_g, rwkv_lnx_b,
           na_rpb, hy_short_w, hy_short_b, hy_w1, hy_b1, hy_w2, hy_b2, hy_w3, hy_b3, hy_w4, hy_freq, hy_bias,
           w_branch_a, w_branch_b, w_branch_c, w_out, ln1_g, ln1_b, w_router, w_exp_gate, w_exp_up,
           w_exp_down, ln2_g, ln2_b):
    stacked = {
        'w_in': w_in, 'rwkv_mu_rkv': rwkv_mu_rkv, 'rwkv_mu_x': rwkv_mu_x, 'rwkv_w0': rwkv_w0,
        'rwkv_w1': rwkv_w1, 'rwkv_w2': rwkv_w2, 'rwkv_a0': rwkv_a0, 'rwkv_a1': rwkv_a1, 'rwkv_a2': rwkv_a2,
        'rwkv_g1': rwkv_g1, 'rwkv_g2': rwkv_g2, 'rwkv_k_k': rwkv_k_k, 'rwkv_k_a': rwkv_k_a,
        'rwkv_r_k': rwkv_r_k, 'rwkv_lnx_g': rwkv_lnx_g, 'rwkv_lnx_b': rwkv_lnx_b, 'na_rpb': na_rpb,
        'hy_short_w': hy_short_w, 'hy_short_b': hy_short_b, 'hy_w1': hy_w1, 'hy_b1': hy_b1,
        'hy_w2': hy_w2, 'hy_b2': hy_b2, 'hy_w3': hy_w3, 'hy_b3': hy_b3, 'hy_w4': hy_w4,
        'hy_freq': hy_freq, 'hy_bias': hy_bias, 'w_branch_a': w_branch_a, 'w_branch_b': w_branch_b,
        'w_branch_c': w_branch_c, 'w_out': w_out, 'ln1_g': ln1_g, 'ln1_b': ln1_b, 'w_router': w_router,
        'w_exp_gate': w_exp_gate, 'w_exp_up': w_exp_up, 'w_exp_down': w_exp_down,
        'ln2_g': ln2_g, 'ln2_b': ln2_b,
    }
    y_prompt = _trunk(x_prompt, ln_in_g, ln_in_b, stacked)
    y_sample = _trunk(x_sample, ln_in_g, ln_in_b, stacked)
    return (y_prompt, y_sample)
```

```python
import functools
import math

import jax
import jax.numpy as jnp
import numpy as np
from jax import lax
from jax.experimental import pallas as pl
from jax.experimental.pallas import tpu as pltpu

D_MODEL = 1024
DEPTH = 2
GRID_W = 64
HEAD_DIM = 64
D_RWKV = D_MODEL // 2
RWKV_HEADS = D_RWKV // HEAD_DIM
D_NA = D_MODEL // 4
NA_HEADS = D_NA // HEAD_DIM
D_HY = D_MODEL // 4
N_BRANCH = 3
GN_EPS = 64e-5
NA_WIN_ROWS = 8
NA_WIN_COLS = 16
HY_ORDER = 2
HY_EMB = 33
HY_TOL = 1e-2
HY_FAST_PCT = 0.3
HY_SLOW_PCT = 1.5
N_EXPERTS = 16
CAPACITY_FACTOR = 2
ALPHA = (2 * DEPTH) ** 0.25
LN_EPS = 1e-5

F32 = jnp.float32
BF16 = jnp.bfloat16


def _mm_body(x_ref, w_ref, o_ref):
    o_ref[...] = jnp.dot(x_ref[...], w_ref[...], preferred_element_type=F32).astype(o_ref.dtype)


def _matmul(x, w, *, tm=1024, tn=512, out_dtype=F32):
    m, k = x.shape
    _, n = w.shape
    tm = min(tm, m)
    tn = min(tn, n)
    assert m % tm == 0 and n % tn == 0
    return pl.pallas_call(
        _mm_body,
        out_shape=jax.ShapeDtypeStruct((m, n), out_dtype),
        grid=(m // tm, n // tn),
        in_specs=[pl.BlockSpec((tm, k), lambda i, j: (i, 0)),
                  pl.BlockSpec((k, tn), lambda i, j: (0, j))],
        out_specs=pl.BlockSpec((tm, tn), lambda i, j: (i, j)),
        name="dense_matmul",
    )(x.astype(BF16), w.astype(BF16))


def _bmm_body(x_ref, w_ref, o_ref):
    o_ref[0] = jnp.dot(x_ref[0], w_ref[0], preferred_element_type=F32).astype(o_ref.dtype)


def _expert_matmul(x, w, *, tm=1024, tn=512, out_dtype=F32):
    e, m, k = x.shape
    n = w.shape[-1]
    tm = min(tm, m)
    tn = min(tn, n)
    assert m % tm == 0 and n % tn == 0
    return pl.pallas_call(
        _bmm_body,
        out_shape=jax.ShapeDtypeStruct((e, m, n), out_dtype),
        grid=(e, m // tm, n // tn),
        in_specs=[pl.BlockSpec((1, tm, k), lambda g, i, j: (g, i, 0)),
                  pl.BlockSpec((1, k, tn), lambda g, i, j: (g, 0, j))],
        out_specs=pl.BlockSpec((1, tm, tn), lambda g, i, j: (g, i, j)),
        name="expert_matmul",
    )(x.astype(BF16), w.astype(BF16))


def _ln(x, g, b, eps=LN_EPS):
    xc = x - jnp.mean(x, -1, keepdims=True)
    var = jnp.mean(xc * xc, -1, keepdims=True)
    return xc * lax.rsqrt(var + eps) * g + b


def _prev(x):
    return jnp.pad(x, ((0, 0), (1, 0), (0, 0)))[:, :-1]


def _next(x):
    return jnp.pad(x, ((0, 0), (0, 1), (0, 0)))[:, 1:]


def _bishift(x, mu):
    return x + mu[0] * (_prev(x) - x) + mu[1] * (_next(x) - x)


RWKV_SUB = 16
RWKV_SLOTS = 128 // RWKV_SUB
LANES = 128


def _col_tile(xt, lane):
    col = xt[:, lane:lane + 1]
    top = jnp.broadcast_to(col[:HEAD_DIM], (HEAD_DIM, LANES))
    bot = jnp.broadcast_to(col[HEAD_DIM:], (HEAD_DIM, LANES))
    lane_id = lax.broadcasted_iota(jnp.int32, (HEAD_DIM, LANES), 1)
    return jnp.where(lane_id < HEAD_DIM, top, bot)


def _rwkv_subchunk(refs, y_ref, s_ref, d, base, tri, reverse):
    r_ref, v_ref, kk_ref, lw_ref, kd_ref, bb_ref = refs
    rows = pl.ds(base, RWKV_SUB)
    r_, v_, kk_ = r_ref[0, rows, :], v_ref[0, rows, :], kk_ref[0, rows, :]
    lw_, kd_, bb_ = lw_ref[0, 0, rows, :], kd_ref[0, 0, rows, :], bb_ref[0, 0, rows, :]
    cum = jnp.dot(tri, lw_, precision=lax.Precision.HIGHEST, preferred_element_type=F32)
    g = jnp.exp(cum)
    ginv = jnp.exp(-cum)
    gprev = jnp.exp(cum - lw_)
    stack = jnp.concatenate(
        [-kk_ * gprev, bb_ * ginv, kd_ * ginv, r_ * g, g,
         jnp.zeros(((RWKV_SLOTS - 5) * RWKV_SUB, LANES), F32)], axis=0)
    xt = stack.T
    st = s_ref[d]
    ys = [None] * RWKV_SUB
    for s in range(RWKV_SUB):
        t = RWKV_SUB - 1 - s if reverse else s
        sa = jnp.sum(st * _col_tile(xt, t), axis=0, keepdims=True)
        st = st + _col_tile(xt, RWKV_SUB + t) * sa + _col_tile(xt, 2 * RWKV_SUB + t) * v_[t:t + 1, :]
        ys[t] = jnp.sum(st * _col_tile(xt, 3 * RWKV_SUB + t), axis=0, keepdims=True)
    last = 0 if reverse else RWKV_SUB - 1
    s_ref[d] = st * _col_tile(xt, 4 * RWKV_SUB + last)
    y_ref[0, 0, rows, :] = jnp.concatenate(ys, axis=0)


def _rwkv_body(rf, vf, kkf, lwf, kdf, bbf, rb, vb, kkb, lwb, kdb, bbb, yf_ref, yb_ref, s_ref, *, tb):
    @pl.when(pl.program_id(2) == 0)
    def _():
        s_ref[...] = jnp.zeros_like(s_ref)

    row = lax.broadcasted_iota(jnp.int32, (RWKV_SUB, RWKV_SUB), 0)
    col = lax.broadcasted_iota(jnp.int32, (RWKV_SUB, RWKV_SUB), 1)
    tri_f = (col <= row).astype(F32)
    tri_b = (col >= row).astype(F32)
    nsub = tb // RWKV_SUB

    def sub(c, carry):
        _rwkv_subchunk((rf, vf, kkf, lwf, kdf, bbf), yf_ref, s_ref, 0,
                       pl.multiple_of(c * RWKV_SUB, RWKV_SUB), tri_f, False)
        _rwkv_subchunk((rb, vb, kkb, lwb, kdb, bbb), yb_ref, s_ref, 1,
                       pl.multiple_of((nsub - 1 - c) * RWKV_SUB, RWKV_SUB), tri_b, True)
        return carry

    lax.fori_loop(0, nsub, sub, 0)


def _rwkv7_scan(r, lw, kd, v, kk, bb, *, tb=512, interpret=False):
    B, T, C = r.shape
    tb = min(tb, T)
    nt = T // tb
    assert T % tb == 0 and tb % RWKV_SUB == 0 and C % LANES == 0
    fwd3 = pl.BlockSpec((1, tb, LANES), lambda b, j, i: (b, i, j))
    bwd3 = pl.BlockSpec((1, tb, LANES), lambda b, j, i: (b, nt - 1 - i, j))
    fwd4 = pl.BlockSpec((1, 1, tb, LANES), lambda b, j, i: (0, b, i, j))
    bwd4 = pl.BlockSpec((1, 1, tb, LANES), lambda b, j, i: (1, b, nt - 1 - i, j))
    yf, yb = pl.pallas_call(
        functools.partial(_rwkv_body, tb=tb),
        out_shape=[jax.ShapeDtypeStruct((1, B, T, C), F32)] * 2,
        grid=(B, C // LANES, nt),
        in_specs=[fwd3, fwd3, fwd3, fwd4, fwd4, fwd4, bwd3, bwd3, bwd3, bwd4, bwd4, bwd4],
        out_specs=[pl.BlockSpec((1, 1, tb, LANES), lambda b, j, i: (0, b, i, j)),
                   pl.BlockSpec((1, 1, tb, LANES), lambda b, j, i: (0, b, nt - 1 - i, j))],
        scratch_shapes=[pltpu.VMEM((2, HEAD_DIM, LANES), F32)],
        compiler_params=pltpu.CompilerParams(dimension_semantics=("parallel", "parallel", "arbitrary")),
        name="rwkv7_scan",
        interpret=interpret,
    )(r, v, kk, lw, kd, bb, r, v, kk, lw, kd, bb)
    return jnp.concatenate([yf, yb], axis=0)


def _rwkv7_mixer(u, r, k, v, p):
    B, T, _ = u.shape
    H, N, C = RWKV_HEADS, HEAD_DIM, D_RWKV
    r = _bishift(r, p['rwkv_mu_rkv'][0])
    k = _bishift(k, p['rwkv_mu_rkv'][1])
    v = _bishift(v, p['rwkv_mu_rkv'][2])
    xw = _bishift(u, p['rwkv_mu_x'][0])
    xa = _bishift(u, p['rwkv_mu_x'][1])
    xg = _bishift(u, p['rwkv_mu_x'][2])
    w_raw = p['rwkv_w0'][:, None, None, :] + jnp.einsum(
        'zbtr,zrc->zbtc', jnp.tanh(jnp.einsum('btd,zdr->zbtr', xw, p['rwkv_w1'])), p['rwkv_w2'])
    log_decay = -jnp.exp(-jax.nn.softplus(-w_raw) - 0.5)
    a = jax.nn.sigmoid(p['rwkv_a0'][:, None, None, :] + jnp.einsum(
        'zbtr,zrc->zbtc', jnp.einsum('btd,zdr->zbtr', xa, p['rwkv_a1']), p['rwkv_a2']))
    g = jnp.einsum('btr,rc->btc', jax.nn.sigmoid(xg @ p['rwkv_g1']), p['rwkv_g2'])
    kk = (k * p['rwkv_k_k']).reshape(B, T, H, N)
    kk = (kk / jnp.maximum(jnp.sqrt(jnp.sum(kk * kk, -1, keepdims=True)), 1e-12)).reshape(B, T, C)
    k_dir = k[None] * (1.0 + (a - 1.0) * p['rwkv_k_a'])

    ys = _rwkv7_scan(r, log_decay, k_dir, v, kk, kk[None] * a)
    y = (ys[0] + ys[1]).reshape(B, T, H, N)
    yc = y - jnp.mean(y, -1, keepdims=True)
    y = yc * lax.rsqrt(jnp.mean(yc * yc, -1, keepdims=True) + GN_EPS)
    y = y.reshape(B, T, C) * p['rwkv_lnx_g'] + p['rwkv_lnx_b']
    bonus = jnp.sum(r.reshape(1, B, T, H, N) * k_dir.reshape(2, B, T, H, N) * p['rwkv_r_k'],
                    -1, keepdims=True) * v.reshape(1, B, T, H, N)
    y = y + jnp.sum(bonus, 0).reshape(B, T, C)
    return y * g


NA_ROWS_PER_STEP = 8


def _na_body(q_ref, k_ref, v_ref, bias_ref, mask_ref, o_ref, *, rows):
    kr = NA_WIN_ROWS
    win = kr * GRID_W
    valid = mask_ref[...] != 0
    lane = lax.broadcasted_iota(jnp.int32, (GRID_W, LANES), 1)
    for ii in range(NA_ROWS_PER_STEP):
        i = pl.program_id(1) * NA_ROWS_PER_STEP + ii
        start = jnp.clip(i - kr // 2, 0, rows - kr)
        d = i - start
        krows = pl.ds(pl.multiple_of(start * GRID_W, GRID_W), win)
        for pair in range(D_NA // LANES):
            lanes = slice(pair * LANES, (pair + 1) * LANES)
            q2 = q_ref[0, ii * GRID_W:(ii + 1) * GRID_W, lanes] * (HEAD_DIM ** -0.5)
            k2 = k_ref[0, krows, lanes]
            v2 = v_ref[0, krows, lanes]
            outs = []
            for hh in range(LANES // HEAD_DIM):
                in_head = (lane >= hh * HEAD_DIM) & (lane < (hh + 1) * HEAD_DIM)
                qh = jnp.where(in_head, q2, 0.0).astype(BF16)
                s = lax.dot_general(qh, k2, (((1,), (1,)), ((), ())), preferred_element_type=F32)
                s = s + bias_ref[d, pair * (LANES // HEAD_DIM) + hh]
                s = jnp.where(valid, s, -1e30)
                m = jnp.max(s, axis=-1, keepdims=True)
                e = jnp.exp(s - m)
                p = e / jnp.sum(e, axis=-1, keepdims=True)
                outs.append(jnp.dot(p.astype(BF16), v2, preferred_element_type=F32))
            o_ref[0, ii * GRID_W:(ii + 1) * GRID_W, lanes] = jnp.where(lane < HEAD_DIM, outs[0], outs[1])


def _neighbourhood_attention_pallas(q, k, v, rpb, *, interpret=False):
    B, T, _ = q.shape
    rows = T // GRID_W
    kr, kc = NA_WIN_ROWS, NA_WIN_COLS
    assert rows >= kr and rows % NA_ROWS_PER_STEP == 0
    ci = jnp.arange(GRID_W)
    col_start = jnp.clip(ci - kc // 2, 0, GRID_W - kc)
    col_valid = (ci[None] >= col_start[:, None]) & (ci[None] < col_start[:, None] + kc)
    mask = jnp.tile(col_valid.astype(jnp.int32), (1, kr))
    dc_idx = jnp.clip(ci[None] - ci[:, None] + kc - 1, 0, 2 * kc - 2)
    dr_idx = jnp.arange(kr)[None, :] - jnp.arange(kr)[:, None] + kr - 1
    bias = rpb[:, dr_idx[:, None, :, None], dc_idx[None, :, None, :]]
    bias = jnp.moveaxis(bias, 1, 0).reshape(kr, NA_HEADS, GRID_W, kr * GRID_W)
    tq = NA_ROWS_PER_STEP * GRID_W
    return pl.pallas_call(
        functools.partial(_na_body, rows=rows),
        out_shape=jax.ShapeDtypeStruct((B, T, D_NA), F32),
        grid=(B, rows // NA_ROWS_PER_STEP),
        in_specs=[pl.BlockSpec((1, tq, D_NA), lambda b, i: (b, i, 0)),
                  pl.BlockSpec((1, T, D_NA), lambda b, i: (b, 0, 0)),
                  pl.BlockSpec((1, T, D_NA), lambda b, i: (b, 0, 0)),
                  pl.BlockSpec((kr, NA_HEADS, GRID_W, kr * GRID_W), lambda b, i: (0, 0, 0, 0)),
                  pl.BlockSpec((GRID_W, kr * GRID_W), lambda b, i: (0, 0))],
        out_specs=pl.BlockSpec((1, tq, D_NA), lambda b, i: (b, i, 0)),
        compiler_params=pltpu.CompilerParams(dimension_semantics=("parallel", "arbitrary")),
        name="neighbourhood_attention",
        interpret=interpret,
    )(q, k.astype(BF16), v.astype(BF16), bias, mask)


def _hyena_filter_spectra(L, p):
    t = jnp.linspace(0.0, 1.0, L, dtype=F32)[:, None]
    n_bands = (HY_EMB - 1) // 2
    omega = 2.0 * math.pi * jnp.arange(L, dtype=F32)[:, None] / L
    bands = jnp.linspace(1e-4, n_bands - 1, n_bands, dtype=F32)[None]
    z = jnp.concatenate([t, jnp.cos(bands * omega), -jnp.sin(bands * omega)], -1)
    freq = p['hy_freq']
    h = jnp.sin(freq * (z @ p['hy_w1'] + p['hy_b1']))
    h = jnp.sin(freq * (h @ p['hy_w2'] + p['hy_b2']))
    h = jnp.sin(freq * (h @ p['hy_w3'] + p['hy_b3']))
    h = (h @ p['hy_w4']).reshape(L, HY_ORDER, 2, D_HY)
    deltas = jnp.abs(jnp.linspace(math.log(HY_TOL) / HY_FAST_PCT, math.log(HY_TOL) / HY_SLOW_PCT, D_HY, dtype=F32))
    h = h * jnp.exp(-t * deltas)[:, None, None, :]
    h_f, h_b = h[:, :, 0], h[:, :, 1]
    kern = jnp.concatenate([h_f, jnp.zeros((1, HY_ORDER, D_HY), F32), h_b[1:][::-1]], 0)
    kern = kern / jnp.sum(jnp.abs(kern), 0, keepdims=True)
    return jnp.fft.rfft(kern, axis=0)


def _long_conv(z, spec, bias):
    L = z.shape[1]
    y = jnp.fft.irfft(jnp.fft.rfft(z, n=2 * L, axis=1) * spec[None], n=2 * L, axis=1)[:, :L]
    return y + z * bias


def _hyena_mixer(x1, x2, v, p):
    T = v.shape[1]
    u = jnp.concatenate([x1, x2, v], -1)
    sw = p['hy_short_w']
    u = _prev(u) * sw[0] + u * sw[1] + _next(u) * sw[2] + p['hy_short_b']
    x1, x2, v = jnp.split(u, 3, axis=-1)
    spec = _hyena_filter_spectra(T, p)
    z = v
    for o, gate in enumerate((x1, x2)):
        z = gate * _long_conv(z, spec[:, o], p['hy_bias'][o])
    return z


def _expert_choice_moe(x, w_router, w_gate, w_up, w_down):
    b, t, d = x.shape
    n = b * t
    xt = x.reshape(n, d)
    cap = (CAPACITY_FACTOR * n) // N_EXPERTS
    aff = jax.nn.softmax(jnp.dot(xt, w_router, precision=lax.Precision.HIGHEST), axis=-1)
    gate, idx = lax.top_k(aff.T, cap)
    xe = jnp.take(xt.astype(BF16), idx, axis=0)
    hg = _expert_matmul(xe, w_gate)
    hu = _expert_matmul(xe, w_up)
    h = jax.nn.silu(hg) * hu
    ye = _expert_matmul(h, w_down) * gate[..., None]
    out = jnp.zeros_like(xt).at[idx.reshape(-1)].add(ye.reshape(-1, d))
    return out.reshape(b, t, d)


def _encoder_layer(x, p):
    B, T, D = x.shape
    n = B * T
    proj = _matmul(x.reshape(n, D), p['w_in']).reshape(B, T, -1)
    sizes = [D_RWKV] * 3 + [D_NA] * 3 + [D_HY] * 3
    cuts = [int(c) for c in np.cumsum(sizes)]
    ra, ka, va, qb, kb, vb, x1c, x2c, vc, gates = jnp.split(proj, cuts, axis=-1)
    y_a = _rwkv7_mixer(x, ra, ka, va, p)
    y_b = _neighbourhood_attention_pallas(qb, kb, vb, p['na_rpb'])
    y_c = _hyena_mixer(x1c, x2c, vc, p)
    g = jax.nn.sigmoid(gates).reshape(B, T, N_BRANCH, D_MODEL)
    m = (g[:, :, 0] * _matmul(y_a.reshape(n, -1), p['w_branch_a']).reshape(B, T, D)
         + g[:, :, 1] * _matmul(y_b.reshape(n, -1), p['w_branch_b']).reshape(B, T, D)
         + g[:, :, 2] * _matmul(y_c.reshape(n, -1), p['w_branch_c']).reshape(B, T, D))
    x = _ln(ALPHA * x + _matmul(m.reshape(n, D), p['w_out']).reshape(B, T, D), p['ln1_g'], p['ln1_b'])
    ffn = _expert_choice_moe(x, p['w_router'], p['w_exp_gate'], p['w_exp_up'], p['w_exp_down'])
    return _ln(ALPHA * x + ffn, p['ln2_g'], p['ln2_b'])


def _trunk(x, ln_in_g, ln_in_b, stacked):
    x = _ln(x, ln_in_g, ln_in_b)
    for l in range(DEPTH):
        x = _encoder_layer(x, {name: arr[l] for name, arr in stacked.items()})
    return x


def kernel(x_prompt, x_sample, ln_in_g, ln_in_b, w_in, rwkv_mu_rkv, rwkv_mu_x, rwkv_w0, rwkv_w1, rwkv_w2,
           rwkv_a0, rwkv_a1, rwkv_a2, rwkv_g1, rwkv_g2, rwkv_k_k, rwkv_k_a, rwkv_r_k, rwkv_lnx_g, rwkv_lnx_b,
           na_rpb, hy_short_w, hy_short_b, hy_w1, hy_b1, hy_w2, hy_b2, hy_w3, hy_b3, hy_w4, hy_freq, hy_bias,
           w_branch_a, w_branch_b, w_branch_c, w_out, ln1_g, ln1_b, w_router, w_exp_gate, w_exp_up,
           w_exp_down, ln2_g, ln2_b):
    stacked = {
        'w_in': w_in, 'rwkv_mu_rkv': rwkv_mu_rkv, 'rwkv_mu_x': rwkv_mu_x, 'rwkv_w0': rwkv_w0,
        'rwkv_w1': rwkv_w1, 'rwkv_w2': rwkv_w2, 'rwkv_a0': rwkv_a0, 'rwkv_a1': rwkv_a1, 'rwkv_a2': rwkv_a2,
        'rwkv_g1': rwkv_g1, 'rwkv_g2': rwkv_g2, 'rwkv_k_k': rwkv_k_k, 'rwkv_k_a': rwkv_k_a,
        'rwkv_r_k': rwkv_r_k, 'rwkv_lnx_g': rwkv_lnx_g, 'rwkv_lnx_b': rwkv_lnx_b, 'na_rpb': na_rpb,
        'hy_short_w': hy_short_w, 'hy_short_b': hy_short_b, 'hy_w1': hy_w1, 'hy_b1': hy_b1,
        'hy_w2': hy_w2, 'hy_b2': hy_b2, 'hy_w3': hy_w3, 'hy_b3': hy_b3, 'hy_w4': hy_w4,
        'hy_freq': hy_freq, 'hy_bias': hy_bias, 'w_branch_a': w_branch_a, 'w_branch_b': w_branch_b,
        'w_branch_c': w_branch_c, 'w_out': w_out, 'ln1_g': ln1_g, 'ln1_b': ln1_b, 'w_router': w_router,
        'w_exp_gate': w_exp_gate, 'w_exp_up': w_exp_up, 'w_exp_down': w_exp_down,
        'ln2_g': ln2_g, 'ln2_b': ln2_b,
    }
    y_prompt = _trunk(x_prompt, ln_in_g, ln_in_b, stacked)
    y_sample = _trunk(x_sample, ln_in_g, ln_in_b, stacked)
    return (y_prompt, y_sample)
```

```python
import functools
import math

import jax
import jax.numpy as jnp
import numpy as np
from jax import lax
from jax.experimental import pallas as pl
from jax.experimental.pallas import tpu as pltpu

D_MODEL = 1024
DEPTH = 2
GRID_W = 64
HEAD_DIM = 64
D_RWKV = D_MODEL // 2
RWKV_HEADS = D_RWKV // HEAD_DIM
D_NA = D_MODEL // 4
NA_HEADS = D_NA // HEAD_DIM
D_HY = D_MODEL // 4
N_BRANCH = 3
GN_EPS = 64e-5
NA_WIN_ROWS = 8
NA_WIN_COLS = 16
HY_ORDER = 2
HY_EMB = 33
HY_TOL = 1e-2
HY_FAST_PCT = 0.3
HY_SLOW_PCT = 1.5
N_EXPERTS = 16
CAPACITY_FACTOR = 2
ALPHA = (2 * DEPTH) ** 0.25
LN_EPS = 1e-5

F32 = jnp.float32
BF16 = jnp.bfloat16


def _mm_body(x_ref, w_ref, o_ref):
    o_ref[...] = jnp.dot(x_ref[...], w_ref[...], preferred_element_type=F32).astype(o_ref.dtype)


def _matmul(x, w, *, tm=1024, tn=512, out_dtype=F32):
    m, k = x.shape
    _, n = w.shape
    tm = min(tm, m)
    tn = min(tn, n)
    assert m % tm == 0 and n % tn == 0
    return pl.pallas_call(
        _mm_body,
        out_shape=jax.ShapeDtypeStruct((m, n), out_dtype),
        grid=(m // tm, n // tn),
        in_specs=[pl.BlockSpec((tm, k), lambda i, j: (i, 0)),
                  pl.BlockSpec((k, tn), lambda i, j: (0, j))],
        out_specs=pl.BlockSpec((tm, tn), lambda i, j: (i, j)),
        name="dense_matmul",
    )(x.astype(BF16), w.astype(BF16))


def _bmm_body(x_ref, w_ref, o_ref):
    o_ref[0] = jnp.dot(x_ref[0], w_ref[0], preferred_element_type=F32).astype(o_ref.dtype)


def _expert_matmul(x, w, *, tm=1024, tn=512, out_dtype=F32):
    e, m, k = x.shape
    n = w.shape[-1]
    tm = min(tm, m)
    tn = min(tn, n)
    assert m % tm == 0 and n % tn == 0
    return pl.pallas_call(
        _bmm_body,
        out_shape=jax.ShapeDtypeStruct((e, m, n), out_dtype),
        grid=(e, m // tm, n // tn),
        in_specs=[pl.BlockSpec((1, tm, k), lambda g, i, j: (g, i, 0)),
                  pl.BlockSpec((1, k, tn), lambda g, i, j: (g, 0, j))],
        out_specs=pl.BlockSpec((1, tm, tn), lambda g, i, j: (g, i, j)),
        name="expert_matmul",
    )(x.astype(BF16), w.astype(BF16))


def _ln(x, g, b, eps=LN_EPS):
    xc = x - jnp.mean(x, -1, keepdims=True)
    var = jnp.mean(xc * xc, -1, keepdims=True)
    return xc * lax.rsqrt(var + eps) * g + b


def _prev(x):
    return jnp.pad(x, ((0, 0), (1, 0), (0, 0)))[:, :-1]


def _next(x):
    return jnp.pad(x, ((0, 0), (0, 1), (0, 0)))[:, 1:]


def _bishift(x, mu):
    return x + mu[0] * (_prev(x) - x) + mu[1] * (_next(x) - x)


RWKV_SUB = 16
LANES = 128
RWKV_PIECES = 3
RWKV_GAMMA_SLOT = 2 * RWKV_PIECES * RWKV_SUB


def _rwkv_selectors():
    z = np.zeros((RWKV_SUB + 1, LANES, LANES), np.float32)
    for h in range(LANES // HEAD_DIM):
        lanes = slice(h * HEAD_DIM, (h + 1) * HEAD_DIM)
        for p in range(RWKV_PIECES):
            for t in range(RWKV_SUB):
                z[t, (h * RWKV_PIECES + p) * RWKV_SUB + t, lanes] = 1.0
            z[RWKV_SUB, RWKV_GAMMA_SLOT + 8 * h + p, lanes] = 1.0
    return jnp.asarray(z, BF16)


def _split3(x):
    hi = x.astype(BF16).astype(F32)
    rem = x - hi
    mid = rem.astype(BF16).astype(F32)
    lo = (rem - mid).astype(BF16).astype(F32)
    return [hi, mid, lo]


def _operand_tile(xa, xb, extra):
    pa = jnp.concatenate(_split3(xa), axis=0)
    pb = jnp.concatenate(_split3(xb), axis=0)
    lane_lo = lax.broadcasted_iota(jnp.int32, pa.shape, 1) < HEAD_DIM
    head0 = jnp.where(lane_lo, pa, pltpu.roll(pb, HEAD_DIM, 1))
    head1 = jnp.where(lane_lo, pltpu.roll(pa, HEAD_DIM, 1), pb)
    tt = jnp.concatenate([head0, head1, extra], axis=0).T
    return tt[:HEAD_DIM].astype(BF16), tt[HEAD_DIM:].astype(BF16)


def _rwkv_prepare(refs, base, reverse):
    r_ref, _, kk_ref, lw_ref, kd_ref, bb_ref = refs
    rows = pl.ds(base, RWKV_SUB)
    r_, kk_ = r_ref[0, rows, :], kk_ref[0, rows, :]
    lw_, kd_, bb_ = lw_ref[0, 0, rows, :], kd_ref[0, 0, rows, :], bb_ref[0, 0, rows, :]
    row = lax.broadcasted_iota(jnp.int32, (RWKV_SUB, RWKV_SUB), 0)
    col = lax.broadcasted_iota(jnp.int32, (RWKV_SUB, RWKV_SUB), 1)
    tri = ((col >= row) if reverse else (col <= row)).astype(F32)
    cum = jnp.dot(tri, lw_, precision=lax.Precision.HIGHEST, preferred_element_type=F32)
    g = jnp.exp(cum)
    ginv = jnp.exp(-cum)
    gprev = jnp.exp(cum - lw_)
    last = 0 if reverse else RWKV_SUB - 1
    gam = jnp.concatenate(_split3(g[last:last + 1, :]) + [jnp.zeros((8 - RWKV_PIECES, LANES), F32)], axis=0)
    extra = jnp.concatenate([gam, pltpu.roll(gam, HEAD_DIM, 1), jnp.zeros((16, LANES), F32)], axis=0)
    g_a, g_b = _operand_tile(-kk_ * gprev, bb_ * ginv, extra)
    g_k, g_r = _operand_tile(kd_ * ginv, r_ * g, jnp.zeros((32, LANES), F32))
    return jnp.concatenate([g_a, g_b, g_k, g_r], axis=0)


def _rwkv_steps(g_all, v_ref, z_ref, y_ref, s_ref, d, base, reverse):
    rows = pl.ds(base, RWKV_SUB)
    v_ = v_ref[0, rows, :]
    st = s_ref[d]
    ys = [None] * RWKV_SUB
    order = list(range(RWKV_SUB))[::-1] if reverse else list(range(RWKV_SUB))
    for t0, t1 in zip(order[0::2], order[1::2]):
        sel = jnp.concatenate([z_ref[t0], z_ref[t1]], axis=1)
        cols = jnp.dot(g_all, sel, preferred_element_type=F32)
        for n, t in enumerate((t0, t1)):
            a_c, b_c, k_c, r_c = (cols[o * HEAD_DIM:(o + 1) * HEAD_DIM, n * LANES:(n + 1) * LANES]
                                  for o in range(4))
            sa = jnp.sum(st * a_c, axis=0, keepdims=True)
            st = st + b_c * sa + k_c * v_[t:t + 1, :]
            ys[t] = jnp.sum(st * r_c, axis=0, keepdims=True)
    s_ref[d] = st * jnp.dot(g_all[:HEAD_DIM], z_ref[RWKV_SUB], preferred_element_type=F32)
    y_ref[0, 0, rows, :] = jnp.concatenate(ys, axis=0)


def _rwkv_body(rf, vf, kkf, lwf, kdf, bbf, rb, vb, kkb, lwb, kdb, bbb, z_ref, yf_ref, yb_ref, s_ref, g_ref,
               *, tb):
    @pl.when(pl.program_id(2) == 0)
    def _():
        s_ref[...] = jnp.zeros_like(s_ref)

    nsub = tb // RWKV_SUB
    refs_f = (rf, vf, kkf, lwf, kdf, bbf)
    refs_b = (rb, vb, kkb, lwb, kdb, bbb)

    def base_f(c):
        return pl.multiple_of(c * RWKV_SUB, RWKV_SUB)

    def base_b(c):
        return pl.multiple_of((nsub - 1 - c) * RWKV_SUB, RWKV_SUB)

    g_ref[0, 0] = _rwkv_prepare(refs_f, base_f(0), False)
    g_ref[0, 1] = _rwkv_prepare(refs_b, base_b(0), True)

    def sub(c, carry):
        slot = c % 2
        nxt = jnp.minimum(c + 1, nsub - 1)
        g_f, g_b = g_ref[slot, 0], g_ref[slot, 1]
        g_ref[1 - slot, 0] = _rwkv_prepare(refs_f, base_f(nxt), False)
        g_ref[1 - slot, 1] = _rwkv_prepare(refs_b, base_b(nxt), True)
        _rwkv_steps(g_f, vf, z_ref, yf_ref, s_ref, 0, base_f(c), False)
        _rwkv_steps(g_b, vb, z_ref, yb_ref, s_ref, 1, base_b(c), True)
        return carry

    lax.fori_loop(0, nsub, sub, 0)


def _rwkv7_scan(r, lw, kd, v, kk, bb, *, tb=512, interpret=False):
    B, T, C = r.shape
    tb = min(tb, T)
    nt = T // tb
    assert T % tb == 0 and tb % RWKV_SUB == 0 and C % LANES == 0
    fwd3 = pl.BlockSpec((1, tb, LANES), lambda b, j, i: (b, i, j))
    bwd3 = pl.BlockSpec((1, tb, LANES), lambda b, j, i: (b, nt - 1 - i, j))
    fwd4 = pl.BlockSpec((1, 1, tb, LANES), lambda b, j, i: (0, b, i, j))
    bwd4 = pl.BlockSpec((1, 1, tb, LANES), lambda b, j, i: (1, b, nt - 1 - i, j))
    yf, yb = pl.pallas_call(
        functools.partial(_rwkv_body, tb=tb),
        out_shape=[jax.ShapeDtypeStruct((1, B, T, C), F32)] * 2,
        grid=(B, C // LANES, nt),
        in_specs=[fwd3, fwd3, fwd3, fwd4, fwd4, fwd4, bwd3, bwd3, bwd3, bwd4, bwd4, bwd4,
                  pl.BlockSpec((RWKV_SUB + 1, LANES, LANES), lambda b, j, i: (0, 0, 0))],
        out_specs=[pl.BlockSpec((1, 1, tb, LANES), lambda b, j, i: (0, b, i, j)),
                   pl.BlockSpec((1, 1, tb, LANES), lambda b, j, i: (0, b, nt - 1 - i, j))],
        scratch_shapes=[pltpu.VMEM((2, HEAD_DIM, LANES), F32),
                        pltpu.VMEM((2, 2, 4 * HEAD_DIM, LANES), BF16)],
        compiler_params=pltpu.CompilerParams(dimension_semantics=("parallel", "parallel", "arbitrary")),
        name="rwkv7_scan",
        interpret=interpret,
    )(r, v, kk, lw, kd, bb, r, v, kk, lw, kd, bb, _rwkv_selectors())
    return jnp.concatenate([yf, yb], axis=0)


def _rwkv7_mixer(u, r, k, v, p):
    B, T, _ = u.shape
    H, N, C = RWKV_HEADS, HEAD_DIM, D_RWKV
    r = _bishift(r, p['rwkv_mu_rkv'][0])
    k = _bishift(k, p['rwkv_mu_rkv'][1])
    v = _bishift(v, p['rwkv_mu_rkv'][2])
    xw = _bishift(u, p['rwkv_mu_x'][0])
    xa = _bishift(u, p['rwkv_mu_x'][1])
    xg = _bishift(u, p['rwkv_mu_x'][2])
    w_raw = p['rwkv_w0'][:, None, None, :] + jnp.einsum(
        'zbtr,zrc->zbtc', jnp.tanh(jnp.einsum('btd,zdr->zbtr', xw, p['rwkv_w1'])), p['rwkv_w2'])
    log_decay = -jnp.exp(-jax.nn.softplus(-w_raw) - 0.5)
    a = jax.nn.sigmoid(p['rwkv_a0'][:, None, None, :] + jnp.einsum(
        'zbtr,zrc->zbtc', jnp.einsum('btd,zdr->zbtr', xa, p['rwkv_a1']), p['rwkv_a2']))
    g = jnp.einsum('btr,rc->btc', jax.nn.sigmoid(xg @ p['rwkv_g1']), p['rwkv_g2'])
    kk = (k * p['rwkv_k_k']).reshape(B, T, H, N)
    kk = (kk / jnp.maximum(jnp.sqrt(jnp.sum(kk * kk, -1, keepdims=True)), 1e-12)).reshape(B, T, C)
    k_dir = k[None] * (1.0 + (a - 1.0) * p['rwkv_k_a'])

    ys = _rwkv7_scan(r, log_decay, k_dir, v, kk, kk[None] * a)
    y = (ys[0] + ys[1]).reshape(B, T, H, N)
    yc = y - jnp.mean(y, -1, keepdims=True)
    y = yc * lax.rsqrt(jnp.mean(yc * yc, -1, keepdims=True) + GN_EPS)
    y = y.reshape(B, T, C) * p['rwkv_lnx_g'] + p['rwkv_lnx_b']
    bonus = jnp.sum(r.reshape(1, B, T, H, N) * k_dir.reshape(2, B, T, H, N) * p['rwkv_r_k'],
                    -1, keepdims=True) * v.reshape(1, B, T, H, N)
    y = y + jnp.sum(bonus, 0).reshape(B, T, C)
    return y * g


NA_ROWS_PER_STEP = 8


def _na_body(q_ref, k_ref, v_ref, bias_ref, mask_ref, o_ref, *, rows):
    kr = NA_WIN_ROWS
    win = kr * GRID_W
    valid = mask_ref[...] != 0
    lane = lax.broadcasted_iota(jnp.int32, (GRID_W, LANES), 1)
    for ii in range(NA_ROWS_PER_STEP):
        i = pl.program_id(1) * NA_ROWS_PER_STEP + ii
        start = jnp.clip(i - kr // 2, 0, rows - kr)
        d = i - start
        krows = pl.ds(pl.multiple_of(start * GRID_W, GRID_W), win)
        for pair in range(D_NA // LANES):
            lanes = slice(pair * LANES, (pair + 1) * LANES)
            q2 = q_ref[0, ii * GRID_W:(ii + 1) * GRID_W, lanes] * (HEAD_DIM ** -0.5)
            k2 = k_ref[0, krows, lanes]
            v2 = v_ref[0, krows, lanes]
            outs = []
            for hh in range(LANES // HEAD_DIM):
                in_head = (lane >= hh * HEAD_DIM) & (lane < (hh + 1) * HEAD_DIM)
                qh = jnp.where(in_head, q2, 0.0).astype(BF16)
                s = lax.dot_general(qh, k2, (((1,), (1,)), ((), ())), preferred_element_type=F32)
                s = s + bias_ref[d, pair * (LANES // HEAD_DIM) + hh]
                s = jnp.where(valid, s, -1e30)
                m = jnp.max(s, axis=-1, keepdims=True)
                e = jnp.exp(s - m)
                p = e / jnp.sum(e, axis=-1, keepdims=True)
                outs.append(jnp.dot(p.astype(BF16), v2, preferred_element_type=F32))
            o_ref[0, ii * GRID_W:(ii + 1) * GRID_W, lanes] = jnp.where(lane < HEAD_DIM, outs[0], outs[1])


def _neighbourhood_attention_pallas(q, k, v, rpb, *, interpret=False):
    B, T, _ = q.shape
    rows = T // GRID_W
    kr, kc = NA_WIN_ROWS, NA_WIN_COLS
    assert rows >= kr and rows % NA_ROWS_PER_STEP == 0
    ci = jnp.arange(GRID_W)
    col_start = jnp.clip(ci - kc // 2, 0, GRID_W - kc)
    col_valid = (ci[None] >= col_start[:, None]) & (ci[None] < col_start[:, None] + kc)
    mask = jnp.tile(col_valid.astype(jnp.int32), (1, kr))
    dc_idx = jnp.clip(ci[None] - ci[:, None] + kc - 1, 0, 2 * kc - 2)
    dr_idx = jnp.arange(kr)[None, :] - jnp.arange(kr)[:, None] + kr - 1
    onehot = (jnp.arange(2 * kc - 1)[:, None] == dc_idx.reshape(1, -1)).astype(F32)
    bias = jnp.dot(rpb[:, dr_idx].reshape(-1, 2 * kc - 1), onehot, precision=lax.Precision.HIGHEST)
    bias = bias.reshape(NA_HEADS, kr, kr, GRID_W, GRID_W)
    bias = jnp.transpose(bias, (1, 0, 3, 2, 4)).reshape(kr, NA_HEADS, GRID_W, kr * GRID_W)
    tq = NA_ROWS_PER_STEP * GRID_W
    return pl.pallas_call(
        functools.partial(_na_body, rows=rows),
        out_shape=jax.ShapeDtypeStruct((B, T, D_NA), F32),
        grid=(B, rows // NA_ROWS_PER_STEP),
        in_specs=[pl.BlockSpec((1, tq, D_NA), lambda b, i: (b, i, 0)),
                  pl.BlockSpec((1, T, D_NA), lambda b, i: (b, 0, 0)),
                  pl.BlockSpec((1, T, D_NA), lambda b, i: (b, 0, 0)),
                  pl.BlockSpec((kr, NA_HEADS, GRID_W, kr * GRID_W), lambda b, i: (0, 0, 0, 0)),
                  pl.BlockSpec((GRID_W, kr * GRID_W), lambda b, i: (0, 0))],
        out_specs=pl.BlockSpec((1, tq, D_NA), lambda b, i: (b, i, 0)),
        compiler_params=pltpu.CompilerParams(dimension_semantics=("parallel", "arbitrary")),
        name="neighbourhood_attention",
        interpret=interpret,
    )(q, k.astype(BF16), v.astype(BF16), bias, mask)


def _hyena_filter_spectra(L, p):
    t = jnp.linspace(0.0, 1.0, L, dtype=F32)[:, None]
    n_bands = (HY_EMB - 1) // 2
    omega = 2.0 * math.pi * jnp.arange(L, dtype=F32)[:, None] / L
    bands = jnp.linspace(1e-4, n_bands - 1, n_bands, dtype=F32)[None]
    z = jnp.concatenate([t, jnp.cos(bands * omega), -jnp.sin(bands * omega)], -1)
    freq = p['hy_freq']
    h = jnp.sin(freq * (z @ p['hy_w1'] + p['hy_b1']))
    h = jnp.sin(freq * (h @ p['hy_w2'] + p['hy_b2']))
    h = jnp.sin(freq * (h @ p['hy_w3'] + p['hy_b3']))
    h = (h @ p['hy_w4']).reshape(L, HY_ORDER, 2, D_HY)
    deltas = jnp.abs(jnp.linspace(math.log(HY_TOL) / HY_FAST_PCT, math.log(HY_TOL) / HY_SLOW_PCT, D_HY, dtype=F32))
    h = h * jnp.exp(-t * deltas)[:, None, None, :]
    h_f, h_b = h[:, :, 0], h[:, :, 1]
    kern = jnp.concatenate([h_f, jnp.zeros((1, HY_ORDER, D_HY), F32), h_b[1:][::-1]], 0)
    kern = kern / jnp.sum(jnp.abs(kern), 0, keepdims=True)
    return jnp.fft.fft(kern, axis=0)


def _dft_tables(R):
    n = jnp.arange(R, dtype=jnp.int32)
    ang = (2.0 * math.pi / R) * ((n[:, None] * n[None, :]) % R).astype(F32)
    c, s = jnp.cos(ang), jnp.sin(ang)
    first = jnp.concatenate([c, -s], axis=0)[:, :R // 2]
    last = jnp.concatenate([c, -s], axis=1)[:R // 2]
    m = (R * n[None, :, None] * n[None, None, :] + n[:, None, None] * n[None, None, :]) % (R * R)
    phi = (2.0 * math.pi / (R * R)) * m.astype(F32)
    cp, sp = jnp.cos(phi), jnp.sin(phi)
    fwd = jnp.concatenate([jnp.concatenate([cp, sp], axis=2), jnp.concatenate([-sp, cp], axis=2)], axis=1)
    cpt, spt = jnp.swapaxes(cp, 1, 2), jnp.swapaxes(sp, 1, 2)
    inv = jnp.concatenate([jnp.concatenate([cpt, -spt], axis=2), jnp.concatenate([spt, cpt], axis=2)], axis=1)
    return first.astype(BF16), fwd.astype(BF16), inv.astype(BF16), last.astype(BF16)


def _conv_first_body(f_ref, z_ref, o_ref):
    o_ref[0] = jnp.dot(f_ref[...], z_ref[0].astype(BF16), preferred_element_type=F32).astype(o_ref.dtype)


def _conv_mid_body(a_ref, mf_ref, mi_ref, h_ref, o_ref, *, R, kb):
    for j in range(kb):
        a = jnp.concatenate([a_ref[0, 0, j], a_ref[0, 1, j]], axis=0)
        x = jnp.dot(mf_ref[j], a, preferred_element_type=F32)
        xr, xi = x[:R], x[R:]
        hr, hi = h_ref[0, j], h_ref[1, j]
        y = jnp.concatenate([xr * hr - xi * hi, xr * hi + xi * hr], axis=0).astype(BF16)
        b = jnp.dot(mi_ref[j], y, preferred_element_type=F32)
        o_ref[0, 0, j] = b[:R].astype(o_ref.dtype)
        o_ref[0, 1, j] = b[R:].astype(o_ref.dtype)


def _conv_last_body(f_ref, b_ref, z_ref, gate_ref, bias_ref, o_ref):
    y = jnp.dot(f_ref[...], b_ref[0], preferred_element_type=F32)
    o_ref[0] = gate_ref[0] * (y + z_ref[0] * bias_ref[...])


def _gated_long_conv(z, gate, hm, bias, tables, *, tn=4096, kb=4, interpret=False):
    B, L, C = z.shape
    R = math.isqrt(2 * L)
    assert R * R == 2 * L and R % (2 * kb) == 0
    first, fwd, inv, last = tables
    tn = min(tn, R * C)
    nt = (R * C) // tn
    z2 = z.reshape(B, R // 2, R * C)
    a = pl.pallas_call(
        _conv_first_body,
        out_shape=jax.ShapeDtypeStruct((B, 2 * R, R * C), BF16),
        grid=(B, nt),
        in_specs=[pl.BlockSpec((2 * R, R // 2), lambda b, j: (0, 0)),
                  pl.BlockSpec((1, R // 2, tn), lambda b, j: (b, 0, j))],
        out_specs=pl.BlockSpec((1, 2 * R, tn), lambda b, j: (b, 0, j)),
        name="long_conv_first",
        interpret=interpret,
    )(first, z2)
    bmid = pl.pallas_call(
        functools.partial(_conv_mid_body, R=R, kb=kb),
        out_shape=jax.ShapeDtypeStruct((B, 2, R, R, C), BF16),
        grid=(B, R // kb),
        in_specs=[pl.BlockSpec((1, 2, kb, R, C), lambda b, k: (b, 0, k, 0, 0)),
                  pl.BlockSpec((kb, 2 * R, 2 * R), lambda b, k: (k, 0, 0)),
                  pl.BlockSpec((kb, 2 * R, 2 * R), lambda b, k: (k, 0, 0)),
                  pl.BlockSpec((2, kb, R, C), lambda b, k: (0, k, 0, 0))],
        out_specs=pl.BlockSpec((1, 2, kb, R, C), lambda b, k: (b, 0, k, 0, 0)),
        name="long_conv_mid",
        interpret=interpret,
    )(a.reshape(B, 2, R, R, C), fwd, inv, hm)
    out = pl.pallas_call(
        _conv_last_body,
        out_shape=jax.ShapeDtypeStruct((B, R // 2, R * C), F32),
        grid=(B, nt),
        in_specs=[pl.BlockSpec((R // 2, 2 * R), lambda b, j: (0, 0)),
                  pl.BlockSpec((1, 2 * R, tn), lambda b, j: (b, 0, j)),
                  pl.BlockSpec((1, R // 2, tn), lambda b, j: (b, 0, j)),
                  pl.BlockSpec((1, R // 2, tn), lambda b, j: (b, 0, j)),
                  pl.BlockSpec((1, tn), lambda b, j: (0, j))],
        out_specs=pl.BlockSpec((1, R // 2, tn), lambda b, j: (b, 0, j)),
        name="long_conv_last",
        interpret=interpret,
    )(last, bmid.reshape(B, 2 * R, R * C), z2, gate.reshape(B, R // 2, R * C), jnp.tile(bias, R)[None])
    return out.reshape(B, L, C)


def _hyena_mixer(x1, x2, v, p):
    T = v.shape[1]
    R = math.isqrt(2 * T)
    u = jnp.concatenate([x1, x2, v], -1)
    sw = p['hy_short_w']
    u = _prev(u) * sw[0] + u * sw[1] + _next(u) * sw[2] + p['hy_short_b']
    x1, x2, v = jnp.split(u, 3, axis=-1)
    spec = _hyena_filter_spectra(T, p) * (1.0 / (2 * T))
    spec = jnp.swapaxes(spec.reshape(R, R, HY_ORDER, D_HY), 0, 1)
    tables = _dft_tables(R)
    z = v
    for o, gate in enumerate((x1, x2)):
        hm = jnp.stack([jnp.real(spec[:, :, o]), jnp.imag(spec[:, :, o])])
        z = _gated_long_conv(z, gate, hm, p['hy_bias'][o], tables)
    return z


def _expert_choice_moe(x, w_router, w_gate, w_up, w_down):
    b, t, d = x.shape
    n = b * t
    xt = x.reshape(n, d)
    cap = (CAPACITY_FACTOR * n) // N_EXPERTS
    aff = jax.nn.softmax(jnp.dot(xt, w_router, precision=lax.Precision.HIGHEST), axis=-1)
    gate, idx = lax.top_k(aff.T, cap)
    xe = jnp.take(xt.astype(BF16), idx, axis=0)
    hg = _expert_matmul(xe, w_gate)
    hu = _expert_matmul(xe, w_up)
    h = jax.nn.silu(hg) * hu
    ye = _expert_matmul(h, w_down) * gate[..., None]
    out = jnp.zeros_like(xt).at[idx.reshape(-1)].add(ye.reshape(-1, d))
    return out.reshape(b, t, d)


def _encoder_layer(x, p):
    B, T, D = x.shape
    n = B * T
    proj = _matmul(x.reshape(n, D), p['w_in']).reshape(B, T, -1)
    sizes = [D_RWKV] * 3 + [D_NA] * 3 + [D_HY] * 3
    cuts = [int(c) for c in np.cumsum(sizes)]
    ra, ka, va, qb, kb, vb, x1c, x2c, vc, gates = jnp.split(proj, cuts, axis=-1)
    y_a = _rwkv7_mixer(x, ra, ka, va, p)
    y_b = _neighbourhood_attention_pallas(qb, kb, vb, p['na_rpb'])
    y_c = _hyena_mixer(x1c, x2c, vc, p)
    g = jax.nn.sigmoid(gates).reshape(B, T, N_BRANCH, D_MODEL)
    m = (g[:, :, 0] * _matmul(y_a.reshape(n, -1), p['w_branch_a']).reshape(B, T, D)
         + g[:, :, 1] * _matmul(y_b.reshape(n, -1), p['w_branch_b']).reshape(B, T, D)
         + g[:, :, 2] * _matmul(y_c.reshape(n, -1), p['w_branch_c']).reshape(B, T, D))
    x = _ln(ALPHA * x + _matmul(m.reshape(n, D), p['w_out']).reshape(B, T, D), p['ln1_g'], p['ln1_b'])
    ffn = _expert_choice_moe(x, p['w_router'], p['w_exp_gate'], p['w_exp_up'], p['w_exp_down'])
    return _ln(ALPHA * x + ffn, p['ln2_g'], p['ln2_b'])


def _trunk(x, ln_in_g, ln_in_b, stacked):
    x = _ln(x, ln_in_g, ln_in_b)
    for l in range(DEPTH):
        x = _encoder_layer(x, {name: arr[l] for name, arr in stacked.items()})
    return x


def kernel(x_prompt, x_sample, ln_in_g, ln_in_b, w_in, rwkv_mu_rkv, rwkv_mu_x, rwkv_w0, rwkv_w1, rwkv_w2,
           rwkv_a0, rwkv_a1, rwkv_a2, rwkv_g1, rwkv_g2, rwkv_k_k, rwkv_k_a, rwkv_r_k, rwkv_lnx_g, rwkv_lnx_b,
           na_rpb, hy_short_w, hy_short_b, hy_w1, hy_b1, hy_w2, hy_b2, hy_w3, hy_b3, hy_w4, hy_freq, hy_bias,
           w_branch_a, w_branch_b, w_branch_c, w_out, ln1_g, ln1_b, w_router, w_exp_gate, w_exp_up,
           w_exp_down, ln2_g, ln2_b):
    stacked = {
        'w_in': w_in, 'rwkv_mu_rkv': rwkv_mu_rkv, 'rwkv_mu_x': rwkv_mu_x, 'rwkv_w0': rwkv_w0,
        'rwkv_w1': rwkv_w1, 'rwkv_w2': rwkv_w2, 'rwkv_a0': rwkv_a0, 'rwkv_a1': rwkv_a1, 'rwkv_a2': rwkv_a2,
        'rwkv_g1': rwkv_g1, 'rwkv_g2': rwkv_g2, 'rwkv_k_k': rwkv_k_k, 'rwkv_k_a': rwkv_k_a,
        'rwkv_r_k': rwkv_r_k, 'rwkv_lnx_g': rwkv_lnx_g, 'rwkv_lnx_b': rwkv_lnx_b, 'na_rpb': na_rpb,
        'hy_short_w': hy_short_w, 'hy_short_b': hy_short_b, 'hy_w1': hy_w1, 'hy_b1': hy_b1,
        'hy_w2': hy_w2, 'hy_b2': hy_b2, 'hy_w3': hy_w3, 'hy_b3': hy_b3, 'hy_w4': hy_w4,
        'hy_freq': hy_freq, 'hy_bias': hy_bias, 'w_branch_a': w_branch_a, 'w_branch_b': w_branch_b,
        'w_branch_c': w_branch_c, 'w_out': w_out, 'ln1_g': ln1_g, 'ln1_b': ln1_b, 'w_router': w_router,
        'w_exp_gate': w_exp_gate, 'w_exp_up': w_exp_up, 'w_exp_down': w_exp_down,
        'ln2_g': ln2_g, 'ln2_b': ln2_b,
    }
    y_prompt = _trunk(x_prompt, ln_in_g, ln_in_b, stacked)
    y_sample = _trunk(x_sample, ln_in_g, ln_in_b, stacked)
    return (y_prompt, y_sample)
```

```python
import functools
import math

import jax
import jax.numpy as jnp
import numpy as np
from jax import lax
from jax.experimental import pallas as pl
from jax.experimental.pallas import tpu as pltpu

D_MODEL = 1024
DEPTH = 2
GRID_W = 64
HEAD_DIM = 64
D_RWKV = D_MODEL // 2
RWKV_HEADS = D_RWKV // HEAD_DIM
D_NA = D_MODEL // 4
NA_HEADS = D_NA // HEAD_DIM
D_HY = D_MODEL // 4
N_BRANCH = 3
GN_EPS = 64e-5
NA_WIN_ROWS = 8
NA_WIN_COLS = 16
HY_ORDER = 2
HY_EMB = 33
HY_TOL = 1e-2
HY_FAST_PCT = 0.3
HY_SLOW_PCT = 1.5
N_EXPERTS = 16
CAPACITY_FACTOR = 2
ALPHA = (2 * DEPTH) ** 0.25
LN_EPS = 1e-5

F32 = jnp.float32
BF16 = jnp.bfloat16
EXPERT_FFN_VMEM_BYTES = 52 * 1024 * 1024


def _mm_body(x_ref, w_ref, o_ref):
    o_ref[...] = jnp.dot(x_ref[...], w_ref[...], preferred_element_type=F32).astype(o_ref.dtype)


def _matmul(x, w, *, tm=1024, tn=512, out_dtype=F32):
    m, k = x.shape
    _, n = w.shape
    tm = min(tm, m)
    tn = min(tn, n)
    assert m % tm == 0 and n % tn == 0
    return pl.pallas_call(
        _mm_body,
        out_shape=jax.ShapeDtypeStruct((m, n), out_dtype),
        grid=(m // tm, n // tn),
        in_specs=[pl.BlockSpec((tm, k), lambda i, j: (i, 0)),
                  pl.BlockSpec((k, tn), lambda i, j: (0, j))],
        out_specs=pl.BlockSpec((tm, tn), lambda i, j: (i, j)),
        name="dense_matmul",
    )(x.astype(BF16), w.astype(BF16))


def _expert_ffn_body(x_ref, wg_ref, wu_ref, wd_ref, gate_ref, o_ref, acc_ref):
    f = pl.program_id(2)

    @pl.when(f == 0)
    def _():
        acc_ref[...] = jnp.zeros_like(acc_ref)

    xb = x_ref[0]
    hg = jnp.dot(xb, wg_ref[0].astype(BF16), preferred_element_type=F32)
    hu = jnp.dot(xb, wu_ref[0].astype(BF16), preferred_element_type=F32)
    h = (hg * jax.nn.sigmoid(hg) * hu).astype(BF16)
    acc_ref[...] += jnp.dot(h, wd_ref[0].astype(BF16), preferred_element_type=F32)

    @pl.when(f == pl.num_programs(2) - 1)
    def _():
        o_ref[0] = acc_ref[...] * gate_ref[0]


def _expert_ffn(xe, gate, w_gate, w_up, w_down, *, tm=2048, tf=256):
    e, m, d = xe.shape
    ff = w_gate.shape[-1]
    tm = min(tm, m)
    assert m % tm == 0 and ff % tf == 0
    return pl.pallas_call(
        _expert_ffn_body,
        out_shape=jax.ShapeDtypeStruct((e, m, d), F32),
        grid=(e, m // tm, ff // tf),
        in_specs=[pl.BlockSpec((1, tm, d), lambda g, i, f: (g, i, 0)),
                  pl.BlockSpec((1, d, tf), lambda g, i, f: (g, 0, f)),
                  pl.BlockSpec((1, d, tf), lambda g, i, f: (g, 0, f)),
                  pl.BlockSpec((1, tf, d), lambda g, i, f: (g, f, 0)),
                  pl.BlockSpec((1, tm, 1), lambda g, i, f: (g, i, 0))],
        out_specs=pl.BlockSpec((1, tm, d), lambda g, i, f: (g, i, 0)),
        scratch_shapes=[pltpu.VMEM((tm, d), F32)],
        compiler_params=pltpu.CompilerParams(
            dimension_semantics=("parallel", "parallel", "arbitrary"), vmem_limit_bytes=EXPERT_FFN_VMEM_BYTES),
        name="expert_ffn",
    )(xe, w_gate, w_up, w_down, gate)


def _ln(x, g, b, eps=LN_EPS):
    xc = x - jnp.mean(x, -1, keepdims=True)
    var = jnp.mean(xc * xc, -1, keepdims=True)
    return xc * lax.rsqrt(var + eps) * g + b


def _prev(x):
    return jnp.pad(x, ((0, 0), (1, 0), (0, 0)))[:, :-1]


def _next(x):
    return jnp.pad(x, ((0, 0), (0, 1), (0, 0)))[:, 1:]


def _bishift(x, mu):
    return x + mu[0] * (_prev(x) - x) + mu[1] * (_next(x) - x)


RWKV_SUB = 16
LANES = 128
RWKV_PIECES = 3
RWKV_GAMMA_SLOT = 2 * RWKV_PIECES * RWKV_SUB


def _rwkv_selectors():
    z = np.zeros((RWKV_SUB + 1, LANES, LANES), np.float32)
    for h in range(LANES // HEAD_DIM):
        lanes = slice(h * HEAD_DIM, (h + 1) * HEAD_DIM)
        for p in range(RWKV_PIECES):
            for t in range(RWKV_SUB):
                z[t, (h * RWKV_PIECES + p) * RWKV_SUB + t, lanes] = 1.0
            z[RWKV_SUB, RWKV_GAMMA_SLOT + 8 * h + p, lanes] = 1.0
    return jnp.asarray(z, BF16)


def _split3(x):
    hi = x.astype(BF16).astype(F32)
    rem = x - hi
    mid = rem.astype(BF16).astype(F32)
    lo = (rem - mid).astype(BF16).astype(F32)
    return [hi, mid, lo]


def _operand_tile(xa, xb, extra):
    pa = jnp.concatenate(_split3(xa), axis=0)
    pb = jnp.concatenate(_split3(xb), axis=0)
    lane_lo = lax.broadcasted_iota(jnp.int32, pa.shape, 1) < HEAD_DIM
    head0 = jnp.where(lane_lo, pa, pltpu.roll(pb, HEAD_DIM, 1))
    head1 = jnp.where(lane_lo, pltpu.roll(pa, HEAD_DIM, 1), pb)
    tt = jnp.concatenate([head0, head1, extra], axis=0).T
    return tt[:HEAD_DIM].astype(BF16), tt[HEAD_DIM:].astype(BF16)


def _rwkv_prepare(refs, base, reverse):
    r_ref, _, kk_ref, lw_ref, kd_ref, bb_ref = refs
    rows = pl.ds(base, RWKV_SUB)
    r_, kk_ = r_ref[0, rows, :], kk_ref[0, rows, :]
    lw_, kd_, bb_ = lw_ref[0, 0, rows, :], kd_ref[0, 0, rows, :], bb_ref[0, 0, rows, :]
    row = lax.broadcasted_iota(jnp.int32, (RWKV_SUB, RWKV_SUB), 0)
    col = lax.broadcasted_iota(jnp.int32, (RWKV_SUB, RWKV_SUB), 1)
    tri = ((col >= row) if reverse else (col <= row)).astype(F32)
    cum = jnp.dot(tri, lw_, precision=lax.Precision.HIGHEST, preferred_element_type=F32)
    g = jnp.exp(cum)
    ginv = jnp.exp(-cum)
    gprev = jnp.exp(cum - lw_)
    last = 0 if reverse else RWKV_SUB - 1
    gam = jnp.concatenate(_split3(g[last:last + 1, :]) + [jnp.zeros((8 - RWKV_PIECES, LANES), F32)], axis=0)
    extra = jnp.concatenate([gam, pltpu.roll(gam, HEAD_DIM, 1), jnp.zeros((16, LANES), F32)], axis=0)
    g_a, g_b = _operand_tile(-kk_ * gprev, bb_ * ginv, extra)
    g_k, g_r = _operand_tile(kd_ * ginv, r_ * g, jnp.zeros((32, LANES), F32))
    return jnp.concatenate([g_a, g_b, g_k, g_r], axis=0)


def _rwkv_steps(g_all, v_ref, z_ref, y_ref, s_ref, d, base, reverse):
    rows = pl.ds(base, RWKV_SUB)
    v_ = v_ref[0, rows, :]
    st = s_ref[d]
    ys = [None] * RWKV_SUB
    order = list(range(RWKV_SUB))[::-1] if reverse else list(range(RWKV_SUB))
    for t0, t1 in zip(order[0::2], order[1::2]):
        sel = jnp.concatenate([z_ref[t0], z_ref[t1]], axis=1)
        cols = jnp.dot(g_all, sel, preferred_element_type=F32)
        for n, t in enumerate((t0, t1)):
            a_c, b_c, k_c, r_c = (cols[o * HEAD_DIM:(o + 1) * HEAD_DIM, n * LANES:(n + 1) * LANES]
                                  for o in range(4))
            sa = jnp.sum(st * a_c, axis=0, keepdims=True)
            st = st + b_c * sa + k_c * v_[t:t + 1, :]
            ys[t] = jnp.sum(st * r_c, axis=0, keepdims=True)
    s_ref[d] = st * jnp.dot(g_all[:HEAD_DIM], z_ref[RWKV_SUB], preferred_element_type=F32)
    y_ref[0, 0, rows, :] = jnp.concatenate(ys, axis=0)


def _rwkv_body(rf, vf, kkf, lwf, kdf, bbf, rb, vb, kkb, lwb, kdb, bbb, z_ref, yf_ref, yb_ref, s_ref, g_ref,
               *, tb):
    @pl.when(pl.program_id(2) == 0)
    def _():
        s_ref[...] = jnp.zeros_like(s_ref)

    nsub = tb // RWKV_SUB
    refs_f = (rf, vf, kkf, lwf, kdf, bbf)
    refs_b = (rb, vb, kkb, lwb, kdb, bbb)

    def base_f(c):
        return pl.multiple_of(c * RWKV_SUB, RWKV_SUB)

    def base_b(c):
        return pl.multiple_of((nsub - 1 - c) * RWKV_SUB, RWKV_SUB)

    g_ref[0, 0] = _rwkv_prepare(refs_f, base_f(0), False)
    g_ref[0, 1] = _rwkv_prepare(refs_b, base_b(0), True)

    def sub(c, carry):
        slot = c % 2
        nxt = jnp.minimum(c + 1, nsub - 1)
        g_f, g_b = g_ref[slot, 0], g_ref[slot, 1]
        g_ref[1 - slot, 0] = _rwkv_prepare(refs_f, base_f(nxt), False)
        g_ref[1 - slot, 1] = _rwkv_prepare(refs_b, base_b(nxt), True)
        _rwkv_steps(g_f, vf, z_ref, yf_ref, s_ref, 0, base_f(c), False)
        _rwkv_steps(g_b, vb, z_ref, yb_ref, s_ref, 1, base_b(c), True)
        return carry

    lax.fori_loop(0, nsub, sub, 0)


def _rwkv7_scan(r, lw, kd, v, kk, bb, *, tb=512, interpret=False):
    B, T, C = r.shape
    tb = min(tb, T)
    nt = T // tb
    assert T % tb == 0 and tb % RWKV_SUB == 0 and C % LANES == 0
    fwd3 = pl.BlockSpec((1, tb, LANES), lambda b, j, i: (b, i, j))
    bwd3 = pl.BlockSpec((1, tb, LANES), lambda b, j, i: (b, nt - 1 - i, j))
    fwd4 = pl.BlockSpec((1, 1, tb, LANES), lambda b, j, i: (0, b, i, j))
    bwd4 = pl.BlockSpec((1, 1, tb, LANES), lambda b, j, i: (1, b, nt - 1 - i, j))
    yf, yb = pl.pallas_call(
        functools.partial(_rwkv_body, tb=tb),
        out_shape=[jax.ShapeDtypeStruct((1, B, T, C), F32)] * 2,
        grid=(B, C // LANES, nt),
        in_specs=[fwd3, fwd3, fwd3, fwd4, fwd4, fwd4, bwd3, bwd3, bwd3, bwd4, bwd4, bwd4,
                  pl.BlockSpec((RWKV_SUB + 1, LANES, LANES), lambda b, j, i: (0, 0, 0))],
        out_specs=[pl.BlockSpec((1, 1, tb, LANES), lambda b, j, i: (0, b, i, j)),
                   pl.BlockSpec((1, 1, tb, LANES), lambda b, j, i: (0, b, nt - 1 - i, j))],
        scratch_shapes=[pltpu.VMEM((2, HEAD_DIM, LANES), F32),
                        pltpu.VMEM((2, 2, 4 * HEAD_DIM, LANES), BF16)],
        compiler_params=pltpu.CompilerParams(dimension_semantics=("parallel", "parallel", "arbitrary")),
        name="rwkv7_scan",
        interpret=interpret,
    )(r, v, kk, lw, kd, bb, r, v, kk, lw, kd, bb, _rwkv_selectors())
    return jnp.concatenate([yf, yb], axis=0)


def _rwkv7_mixer(u, r, k, v, p):
    B, T, _ = u.shape
    H, N, C = RWKV_HEADS, HEAD_DIM, D_RWKV
    r = _bishift(r, p['rwkv_mu_rkv'][0])
    k = _bishift(k, p['rwkv_mu_rkv'][1])
    v = _bishift(v, p['rwkv_mu_rkv'][2])
    xw = _bishift(u, p['rwkv_mu_x'][0])
    xa = _bishift(u, p['rwkv_mu_x'][1])
    xg = _bishift(u, p['rwkv_mu_x'][2])
    w_raw = p['rwkv_w0'][:, None, None, :] + jnp.einsum(
        'zbtr,zrc->zbtc', jnp.tanh(jnp.einsum('btd,zdr->zbtr', xw, p['rwkv_w1'])), p['rwkv_w2'])
    log_decay = -jnp.exp(-jax.nn.softplus(-w_raw) - 0.5)
    a = jax.nn.sigmoid(p['rwkv_a0'][:, None, None, :] + jnp.einsum(
        'zbtr,zrc->zbtc', jnp.einsum('btd,zdr->zbtr', xa, p['rwkv_a1']), p['rwkv_a2']))
    g = jnp.einsum('btr,rc->btc', jax.nn.sigmoid(xg @ p['rwkv_g1']), p['rwkv_g2'])
    kk = (k * p['rwkv_k_k']).reshape(B, T, H, N)
    kk = (kk / jnp.maximum(jnp.sqrt(jnp.sum(kk * kk, -1, keepdims=True)), 1e-12)).reshape(B, T, C)
    k_dir = k[None] * (1.0 + (a - 1.0) * p['rwkv_k_a'])

    ys = _rwkv7_scan(r, log_decay, k_dir, v, kk, kk[None] * a)
    y = (ys[0] + ys[1]).reshape(B, T, H, N)
    yc = y - jnp.mean(y, -1, keepdims=True)
    y = yc * lax.rsqrt(jnp.mean(yc * yc, -1, keepdims=True) + GN_EPS)
    y = y.reshape(B, T, C) * p['rwkv_lnx_g'] + p['rwkv_lnx_b']
    bonus = jnp.sum(r.reshape(1, B, T, H, N) * k_dir.reshape(2, B, T, H, N) * p['rwkv_r_k'],
                    -1, keepdims=True) * v.reshape(1, B, T, H, N)
    y = y + jnp.sum(bonus, 0).reshape(B, T, C)
    return y * g


NA_ROWS_PER_STEP = 8


def _na_body(q_ref, k_ref, v_ref, bias_ref, mask_ref, o_ref, *, rows):
    kr = NA_WIN_ROWS
    win = kr * GRID_W
    valid = mask_ref[...] != 0
    lane = lax.broadcasted_iota(jnp.int32, (GRID_W, LANES), 1)
    for ii in range(NA_ROWS_PER_STEP):
        i = pl.program_id(1) * NA_ROWS_PER_STEP + ii
        start = jnp.clip(i - kr // 2, 0, rows - kr)
        d = i - start
        krows = pl.ds(pl.multiple_of(start * GRID_W, GRID_W), win)
        for pair in range(D_NA // LANES):
            lanes = slice(pair * LANES, (pair + 1) * LANES)
            q2 = q_ref[0, ii * GRID_W:(ii + 1) * GRID_W, lanes] * (HEAD_DIM ** -0.5)
            k2 = k_ref[0, krows, lanes]
            v2 = v_ref[0, krows, lanes]
            outs = []
            for hh in range(LANES // HEAD_DIM):
                in_head = (lane >= hh * HEAD_DIM) & (lane < (hh + 1) * HEAD_DIM)
                qh = jnp.where(in_head, q2, 0.0).astype(BF16)
                s = lax.dot_general(qh, k2, (((1,), (1,)), ((), ())), preferred_element_type=F32)
                s = s + bias_ref[d, pair * (LANES // HEAD_DIM) + hh]
                s = jnp.where(valid, s, -1e30)
                m = jnp.max(s, axis=-1, keepdims=True)
                e = jnp.exp(s - m)
                p = e / jnp.sum(e, axis=-1, keepdims=True)
                outs.append(jnp.dot(p.astype(BF16), v2, preferred_element_type=F32))
            o_ref[0, ii * GRID_W:(ii + 1) * GRID_W, lanes] = jnp.where(lane < HEAD_DIM, outs[0], outs[1])


def _neighbourhood_attention_pallas(q, k, v, rpb, *, interpret=False):
    B, T, _ = q.shape
    rows = T // GRID_W
    kr, kc = NA_WIN_ROWS, NA_WIN_COLS
    assert rows >= kr and rows % NA_ROWS_PER_STEP == 0
    ci = jnp.arange(GRID_W)
    col_start = jnp.clip(ci - kc // 2, 0, GRID_W - kc)
    col_valid = (ci[None] >= col_start[:, None]) & (ci[None] < col_start[:, None] + kc)
    mask = jnp.tile(col_valid.astype(jnp.int32), (1, kr))
    dc_idx = jnp.clip(ci[None] - ci[:, None] + kc - 1, 0, 2 * kc - 2)
    dr_idx = jnp.arange(kr)[None, :] - jnp.arange(kr)[:, None] + kr - 1
    onehot = (jnp.arange(2 * kc - 1)[:, None] == dc_idx.reshape(1, -1)).astype(F32)
    bias = jnp.dot(rpb[:, dr_idx].reshape(-1, 2 * kc - 1), onehot, precision=lax.Precision.HIGHEST)
    bias = bias.reshape(NA_HEADS, kr, kr, GRID_W, GRID_W)
    bias = jnp.transpose(bias, (1, 0, 3, 2, 4)).reshape(kr, NA_HEADS, GRID_W, kr * GRID_W)
    tq = NA_ROWS_PER_STEP * GRID_W
    return pl.pallas_call(
        functools.partial(_na_body, rows=rows),
        out_shape=jax.ShapeDtypeStruct((B, T, D_NA), F32),
        grid=(B, rows // NA_ROWS_PER_STEP),
        in_specs=[pl.BlockSpec((1, tq, D_NA), lambda b, i: (b, i, 0)),
                  pl.BlockSpec((1, T, D_NA), lambda b, i: (b, 0, 0)),
                  pl.BlockSpec((1, T, D_NA), lambda b, i: (b, 0, 0)),
                  pl.BlockSpec((kr, NA_HEADS, GRID_W, kr * GRID_W), lambda b, i: (0, 0, 0, 0)),
                  pl.BlockSpec((GRID_W, kr * GRID_W), lambda b, i: (0, 0))],
        out_specs=pl.BlockSpec((1, tq, D_NA), lambda b, i: (b, i, 0)),
        compiler_params=pltpu.CompilerParams(dimension_semantics=("parallel", "arbitrary")),
        name="neighbourhood_attention",
        interpret=interpret,
    )(q, k.astype(BF16), v.astype(BF16), bias, mask)


def _hyena_filter_spectra(L, p):
    t = jnp.linspace(0.0, 1.0, L, dtype=F32)[:, None]
    n_bands = (HY_EMB - 1) // 2
    omega = 2.0 * math.pi * jnp.arange(L, dtype=F32)[:, None] / L
    bands = jnp.linspace(1e-4, n_bands - 1, n_bands, dtype=F32)[None]
    z = jnp.concatenate([t, jnp.cos(bands * omega), -jnp.sin(bands * omega)], -1)
    freq = p['hy_freq']
    h = jnp.sin(freq * (z @ p['hy_w1'] + p['hy_b1']))
    h = jnp.sin(freq * (h @ p['hy_w2'] + p['hy_b2']))
    h = jnp.sin(freq * (h @ p['hy_w3'] + p['hy_b3']))
    h = (h @ p['hy_w4']).reshape(L, HY_ORDER, 2, D_HY)
    deltas = jnp.abs(jnp.linspace(math.log(HY_TOL) / HY_FAST_PCT, math.log(HY_TOL) / HY_SLOW_PCT, D_HY, dtype=F32))
    h = h * jnp.exp(-t * deltas)[:, None, None, :]
    h_f, h_b = h[:, :, 0], h[:, :, 1]
    kern = jnp.concatenate([h_f, jnp.zeros((1, HY_ORDER, D_HY), F32), h_b[1:][::-1]], 0)
    kern = kern / jnp.sum(jnp.abs(kern), 0, keepdims=True)
    return jnp.fft.fft(kern, axis=0)


def _dft_tables(R):
    n = jnp.arange(R, dtype=jnp.int32)
    ang = (2.0 * math.pi / R) * ((n[:, None] * n[None, :]) % R).astype(F32)
    c, s = jnp.cos(ang), jnp.sin(ang)
    first = jnp.concatenate([c, -s], axis=0)[:, :R // 2]
    last = jnp.concatenate([c, -s], axis=1)[:R // 2]
    m = (R * n[None, :, None] * n[None, None, :] + n[:, None, None] * n[None, None, :]) % (R * R)
    phi = (2.0 * math.pi / (R * R)) * m.astype(F32)
    cp, sp = jnp.cos(phi), jnp.sin(phi)
    fwd = jnp.concatenate([jnp.concatenate([cp, sp], axis=2), jnp.concatenate([-sp, cp], axis=2)], axis=1)
    cpt, spt = jnp.swapaxes(cp, 1, 2), jnp.swapaxes(sp, 1, 2)
    inv = jnp.concatenate([jnp.concatenate([cpt, -spt], axis=2), jnp.concatenate([spt, cpt], axis=2)], axis=1)
    return first.astype(BF16), fwd.astype(BF16), inv.astype(BF16), last.astype(BF16)


def _conv_first_body(f_ref, z_ref, o_ref):
    o_ref[0] = jnp.dot(f_ref[...], z_ref[0].astype(BF16), preferred_element_type=F32).astype(o_ref.dtype)


def _conv_mid_body(a_ref, mf_ref, mi_ref, h_ref, o_ref, *, R, kb):
    for j in range(kb):
        a = jnp.concatenate([a_ref[0, 0, j], a_ref[0, 1, j]], axis=0)
        x = jnp.dot(mf_ref[j], a, preferred_element_type=F32)
        xr, xi = x[:R], x[R:]
        hr, hi = h_ref[0, j], h_ref[1, j]
        y = jnp.concatenate([xr * hr - xi * hi, xr * hi + xi * hr], axis=0).astype(BF16)
        b = jnp.dot(mi_ref[j], y, preferred_element_type=F32)
        o_ref[0, 0, j] = b[:R].astype(o_ref.dtype)
        o_ref[0, 1, j] = b[R:].astype(o_ref.dtype)


def _conv_last_body(f_ref, b_ref, z_ref, gate_ref, bias_ref, o_ref):
    y = jnp.dot(f_ref[...], b_ref[0], preferred_element_type=F32)
    o_ref[0] = gate_ref[0] * (y + z_ref[0] * bias_ref[...])


def _gated_long_conv(z, gate, hm, bias, tables, *, tn=4096, kb=4, interpret=False):
    B, L, C = z.shape
    R = math.isqrt(2 * L)
    assert R * R == 2 * L and R % (2 * kb) == 0
    first, fwd, inv, last = tables
    tn = min(tn, R * C)
    nt = (R * C) // tn
    z2 = z.reshape(B, R // 2, R * C)
    a = pl.pallas_call(
        _conv_first_body,
        out_shape=jax.ShapeDtypeStruct((B, 2 * R, R * C), BF16),
        grid=(B, nt),
        in_specs=[pl.BlockSpec((2 * R, R // 2), lambda b, j: (0, 0)),
                  pl.BlockSpec((1, R // 2, tn), lambda b, j: (b, 0, j))],
        out_specs=pl.BlockSpec((1, 2 * R, tn), lambda b, j: (b, 0, j)),
        name="long_conv_first",
        interpret=interpret,
    )(first, z2)
    bmid = pl.pallas_call(
        functools.partial(_conv_mid_body, R=R, kb=kb),
        out_shape=jax.ShapeDtypeStruct((B, 2, R, R, C), BF16),
        grid=(B, R // kb),
        in_specs=[pl.BlockSpec((1, 2, kb, R, C), lambda b, k: (b, 0, k, 0, 0)),
                  pl.BlockSpec((kb, 2 * R, 2 * R), lambda b, k: (k, 0, 0)),
                  pl.BlockSpec((kb, 2 * R, 2 * R), lambda b, k: (k, 0, 0)),
                  pl.BlockSpec((2, kb, R, C), lambda b, k: (0, k, 0, 0))],
        out_specs=pl.BlockSpec((1, 2, kb, R, C), lambda b, k: (b, 0, k, 0, 0)),
        name="long_conv_mid",
        interpret=interpret,
    )(a.reshape(B, 2, R, R, C), fwd, inv, hm)
    out = pl.pallas_call(
        _conv_last_body,
        out_shape=jax.ShapeDtypeStruct((B, R // 2, R * C), F32),
        grid=(B, nt),
        in_specs=[pl.BlockSpec((R // 2, 2 * R), lambda b, j: (0, 0)),
                  pl.BlockSpec((1, 2 * R, tn), lambda b, j: (b, 0, j)),
                  pl.BlockSpec((1, R // 2, tn), lambda b, j: (b, 0, j)),
                  pl.BlockSpec((1, R // 2, tn), lambda b, j: (b, 0, j)),
                  pl.BlockSpec((1, tn), lambda b, j: (0, j))],
        out_specs=pl.BlockSpec((1, R // 2, tn), lambda b, j: (b, 0, j)),
        name="long_conv_last",
        interpret=interpret,
    )(last, bmid.reshape(B, 2 * R, R * C), z2, gate.reshape(B, R // 2, R * C), jnp.tile(bias, R)[None])
    return out.reshape(B, L, C)


def _hyena_mixer(x1, x2, v, p):
    T = v.shape[1]
    R = math.isqrt(2 * T)
    u = jnp.concatenate([x1, x2, v], -1)
    sw = p['hy_short_w']
    u = _prev(u) * sw[0] + u * sw[1] + _next(u) * sw[2] + p['hy_short_b']
    x1, x2, v = jnp.split(u, 3, axis=-1)
    spec = _hyena_filter_spectra(T, p) * (1.0 / (2 * T))
    spec = jnp.swapaxes(spec.reshape(R, R, HY_ORDER, D_HY), 0, 1)
    tables = _dft_tables(R)
    z = v
    for o, gate in enumerate((x1, x2)):
        hm = jnp.stack([jnp.real(spec[:, :, o]), jnp.imag(spec[:, :, o])])
        z = _gated_long_conv(z, gate, hm, p['hy_bias'][o], tables)
    return z


def _route(aff):
    n = aff.shape[0]
    cap = (CAPACITY_FACTOR * n) // N_EXPERTS
    return lax.top_k(aff.T, cap)


def _mix_body(x_ref, ya_ref, yb_ref, yc_ref, g_ref, wa_ref, wb_ref, wc_ref, wo_ref, lg_ref, lb_ref, wr_ref,
              o_ref, ob_ref, aff_ref):
    g = jax.nn.sigmoid(g_ref[...])
    m = (g[:, :D_MODEL] * jnp.dot(ya_ref[...].astype(BF16), wa_ref[...], preferred_element_type=F32)
         + g[:, D_MODEL:2 * D_MODEL] * jnp.dot(yb_ref[...].astype(BF16), wb_ref[...], preferred_element_type=F32)
         + g[:, 2 * D_MODEL:] * jnp.dot(yc_ref[...].astype(BF16), wc_ref[...], preferred_element_type=F32))
    h = ALPHA * x_ref[...] + jnp.dot(m.astype(BF16), wo_ref[...], preferred_element_type=F32)
    hc = h - jnp.mean(h, -1, keepdims=True)
    var = jnp.mean(hc * hc, -1, keepdims=True)
    y = hc * lax.rsqrt(var + LN_EPS) * lg_ref[...] + lb_ref[...]
    o_ref[...] = y
    ob_ref[...] = y.astype(BF16)
    logits = jnp.dot(y, wr_ref[...], precision=lax.Precision.HIGHEST, preferred_element_type=F32)
    e = jnp.exp(logits - jnp.max(logits, -1, keepdims=True))
    aff_ref[...] = e / jnp.sum(e, -1, keepdims=True)


def _mix_and_norm(x, y_a, y_b, y_c, proj, p, *, tm=512):
    n, d = x.shape
    assert n % tm == 0 and proj.shape[1] == 2 * N_BRANCH * d
    row = lambda w: pl.BlockSpec((tm, w), lambda i: (i, 0))
    full = lambda a: pl.BlockSpec(a.shape, lambda i: (0,) * a.ndim)
    wa, wb, wc, wo = (p[k].astype(BF16) for k in ('w_branch_a', 'w_branch_b', 'w_branch_c', 'w_out'))
    lg, lb = p['ln1_g'][None], p['ln1_b'][None]
    return pl.pallas_call(
        _mix_body,
        out_shape=[jax.ShapeDtypeStruct((n, d), F32), jax.ShapeDtypeStruct((n, d), BF16),
                   jax.ShapeDtypeStruct((n, N_EXPERTS), F32)],
        grid=(n // tm,),
        in_specs=[row(d), row(D_RWKV), row(D_NA), row(D_HY),
                  pl.BlockSpec((tm, N_BRANCH * d), lambda i: (i, 1)),
                  full(wa), full(wb), full(wc), full(wo), full(lg), full(lb), full(p['w_router'])],
        out_specs=[row(d), row(d), row(N_EXPERTS)],
        compiler_params=pltpu.CompilerParams(dimension_semantics=("parallel",)),
        name="mix_norm_route",
    )(x, y_a, y_b, y_c, proj, wa, wb, wc, wo, lg, lb, p['w_router'])


def _mixers(x, p):
    B, T, D = x.shape
    n = B * T
    proj2 = _matmul(x.reshape(n, D), p['w_in'])
    proj = proj2.reshape(B, T, -1)
    sizes = [D_RWKV] * 3 + [D_NA] * 3 + [D_HY] * 3
    cuts = [int(c) for c in np.cumsum(sizes)]
    ra, ka, va, qb, kb, vb, x1c, x2c, vc, _ = jnp.split(proj, cuts, axis=-1)
    y_a = _rwkv7_mixer(x, ra, ka, va, p)
    y_b = _neighbourhood_attention_pallas(qb, kb, vb, p['na_rpb'])
    y_c = _hyena_mixer(x1c, x2c, vc, p)
    return _mix_and_norm(x.reshape(n, D), y_a.reshape(n, -1), y_b.reshape(n, -1), y_c.reshape(n, -1), proj2, p)


def _encoder_layer(xs, p):
    mixed = [_mixers(x, p) for x in xs]
    routes = [_route(aff) for _, _, aff in mixed]
    xe = jnp.concatenate([jnp.take(xb, idx, axis=0) for (_, xb, _), (_, idx) in zip(mixed, routes)], axis=1)
    gate = jnp.concatenate([g for g, _ in routes], axis=1)[..., None]
    ye = _expert_ffn(xe, gate, p['w_exp_gate'], p['w_exp_up'], p['w_exp_down'])
    outs, off = [], 0
    for x, (x1, _, _), (_, idx) in zip(xs, mixed, routes):
        cap = idx.shape[1]
        ffn = jnp.zeros_like(x1).at[idx.reshape(-1)].add(ye[:, off:off + cap].reshape(-1, x1.shape[-1]))
        off += cap
        outs.append(_ln(ALPHA * x1 + ffn, p['ln2_g'], p['ln2_b']).reshape(x.shape))
    return outs


def kernel(x_prompt, x_sample, ln_in_g, ln_in_b, w_in, rwkv_mu_rkv, rwkv_mu_x, rwkv_w0, rwkv_w1, rwkv_w2,
           rwkv_a0, rwkv_a1, rwkv_a2, rwkv_g1, rwkv_g2, rwkv_k_k, rwkv_k_a, rwkv_r_k, rwkv_lnx_g, rwkv_lnx_b,
           na_rpb, hy_short_w, hy_short_b, hy_w1, hy_b1, hy_w2, hy_b2, hy_w3, hy_b3, hy_w4, hy_freq, hy_bias,
           w_branch_a, w_branch_b, w_branch_c, w_out, ln1_g, ln1_b, w_router, w_exp_gate, w_exp_up,
           w_exp_down, ln2_g, ln2_b):
    stacked = {
        'w_in': w_in, 'rwkv_mu_rkv': rwkv_mu_rkv, 'rwkv_mu_x': rwkv_mu_x, 'rwkv_w0': rwkv_w0,
        'rwkv_w1': rwkv_w1, 'rwkv_w2': rwkv_w2, 'rwkv_a0': rwkv_a0, 'rwkv_a1': rwkv_a1, 'rwkv_a2': rwkv_a2,
        'rwkv_g1': rwkv_g1, 'rwkv_g2': rwkv_g2, 'rwkv_k_k': rwkv_k_k, 'rwkv_k_a': rwkv_k_a,
        'rwkv_r_k': rwkv_r_k, 'rwkv_lnx_g': rwkv_lnx_g, 'rwkv_lnx_b': rwkv_lnx_b, 'na_rpb': na_rpb,
        'hy_short_w': hy_short_w, 'hy_short_b': hy_short_b, 'hy_w1': hy_w1, 'hy_b1': hy_b1,
        'hy_w2': hy_w2, 'hy_b2': hy_b2, 'hy_w3': hy_w3, 'hy_b3': hy_b3, 'hy_w4': hy_w4,
        'hy_freq': hy_freq, 'hy_bias': hy_bias, 'w_branch_a': w_branch_a, 'w_branch_b': w_branch_b,
        'w_branch_c': w_branch_c, 'w_out': w_out, 'ln1_g': ln1_g, 'ln1_b': ln1_b, 'w_router': w_router,
        'w_exp_gate': w_exp_gate, 'w_exp_up': w_exp_up, 'w_exp_down': w_exp_down,
        'ln2_g': ln2_g, 'ln2_b': ln2_b,
    }
    xs = [_ln(x, ln_in_g, ln_in_b) for x in (x_prompt, x_sample)]
    for l in range(DEPTH):
        xs = _encoder_layer(xs, {name: arr[l] for name, arr in stacked.items()})
    return tuple(xs)
```

```python
import functools
import math

import jax
import jax.numpy as jnp
import numpy as np
from jax import lax
from jax.experimental import pallas as pl
from jax.experimental.pallas import tpu as pltpu

D_MODEL = 1024
DEPTH = 2
GRID_W = 64
HEAD_DIM = 64
D_RWKV = D_MODEL // 2
RWKV_HEADS = D_RWKV // HEAD_DIM
D_NA = D_MODEL // 4
NA_HEADS = D_NA // HEAD_DIM
D_HY = D_MODEL // 4
N_BRANCH = 3
GN_EPS = 64e-5
NA_WIN_ROWS = 8
NA_WIN_COLS = 16
HY_ORDER = 2
HY_EMB = 33
HY_TOL = 1e-2
HY_FAST_PCT = 0.3
HY_SLOW_PCT = 1.5
N_EXPERTS = 16
CAPACITY_FACTOR = 2
ALPHA = (2 * DEPTH) ** 0.25
LN_EPS = 1e-5

F32 = jnp.float32
BF16 = jnp.bfloat16
EXPERT_FFN_VMEM_BYTES = 52 * 1024 * 1024


def _mm_body(x_ref, w_ref, o_ref):
    o_ref[...] = jnp.dot(x_ref[...], w_ref[...], preferred_element_type=F32).astype(o_ref.dtype)


def _matmul(x, w, *, tm=1024, tn=512, out_dtype=F32):
    m, k = x.shape
    _, n = w.shape
    tm = min(tm, m)
    tn = min(tn, n)
    assert m % tm == 0 and n % tn == 0
    return pl.pallas_call(
        _mm_body,
        out_shape=jax.ShapeDtypeStruct((m, n), out_dtype),
        grid=(m // tm, n // tn),
        in_specs=[pl.BlockSpec((tm, k), lambda i, j: (i, 0)),
                  pl.BlockSpec((k, tn), lambda i, j: (0, j))],
        out_specs=pl.BlockSpec((tm, tn), lambda i, j: (i, j)),
        name="dense_matmul",
    )(x.astype(BF16), w.astype(BF16))


def _expert_ffn_body(x_ref, wg_ref, wu_ref, wd_ref, gate_ref, o_ref, acc_ref):
    f = pl.program_id(2)

    @pl.when(f == 0)
    def _():
        acc_ref[...] = jnp.zeros_like(acc_ref)

    xb = x_ref[0]
    hg = jnp.dot(xb, wg_ref[0].astype(BF16), preferred_element_type=F32)
    hu = jnp.dot(xb, wu_ref[0].astype(BF16), preferred_element_type=F32)
    h = (hg * jax.nn.sigmoid(hg) * hu).astype(BF16)
    acc_ref[...] += jnp.dot(h, wd_ref[0].astype(BF16), preferred_element_type=F32)

    @pl.when(f == pl.num_programs(2) - 1)
    def _():
        o_ref[0] = acc_ref[...] * gate_ref[0]


def _expert_ffn(xe, gate, w_gate, w_up, w_down, *, tm=2048, tf=256):
    e, m, d = xe.shape
    ff = w_gate.shape[-1]
    tm = min(tm, m)
    assert m % tm == 0 and ff % tf == 0
    return pl.pallas_call(
        _expert_ffn_body,
        out_shape=jax.ShapeDtypeStruct((e, m, d), F32),
        grid=(e, m // tm, ff // tf),
        in_specs=[pl.BlockSpec((1, tm, d), lambda g, i, f: (g, i, 0)),
                  pl.BlockSpec((1, d, tf), lambda g, i, f: (g, 0, f)),
                  pl.BlockSpec((1, d, tf), lambda g, i, f: (g, 0, f)),
                  pl.BlockSpec((1, tf, d), lambda g, i, f: (g, f, 0)),
                  pl.BlockSpec((1, tm, 1), lambda g, i, f: (g, i, 0))],
        out_specs=pl.BlockSpec((1, tm, d), lambda g, i, f: (g, i, 0)),
        scratch_shapes=[pltpu.VMEM((tm, d), F32)],
        compiler_params=pltpu.CompilerParams(
            dimension_semantics=("parallel", "parallel", "arbitrary"), vmem_limit_bytes=EXPERT_FFN_VMEM_BYTES),
        name="expert_ffn",
    )(xe, w_gate, w_up, w_down, gate)


def _layer_norm(h, g, b):
    hc = h - jnp.mean(h, -1, keepdims=True)
    var = jnp.mean(hc * hc, -1, keepdims=True)
    return hc * lax.rsqrt(var + LN_EPS) * g + b


def _ln_body(x_ref, g_ref, b_ref, o_ref):
    o_ref[...] = _layer_norm(x_ref[...], g_ref[...], b_ref[...])


def _ln_residual_body(x_ref, r_ref, g_ref, b_ref, o_ref):
    o_ref[...] = _layer_norm(ALPHA * x_ref[...] + r_ref[...], g_ref[...], b_ref[...])


def _ln(x, g, b, residual=None, *, tm=1024):
    n, d = x.shape
    tm = min(tm, n)
    assert n % tm == 0
    row = pl.BlockSpec((tm, d), lambda i: (i, 0))
    vec = pl.BlockSpec((1, d), lambda i: (0, 0))
    args = (x,) if residual is None else (x, residual)
    return pl.pallas_call(
        _ln_body if residual is None else _ln_residual_body,
        out_shape=jax.ShapeDtypeStruct((n, d), F32),
        grid=(n // tm,),
        in_specs=[row] * len(args) + [vec, vec],
        out_specs=row,
        compiler_params=pltpu.CompilerParams(dimension_semantics=("parallel",)),
        name="layer_norm",
    )(*args, g[None], b[None])


RWKV_SUB = 16
LANES = 128
RWKV_PIECES = 3
RWKV_GAMMA_SLOT = 2 * RWKV_PIECES * RWKV_SUB


def _rwkv_selectors():
    z = np.zeros((RWKV_SUB + 1, LANES, LANES), np.float32)
    for h in range(LANES // HEAD_DIM):
        lanes = slice(h * HEAD_DIM, (h + 1) * HEAD_DIM)
        for p in range(RWKV_PIECES):
            for t in range(RWKV_SUB):
                z[t, (h * RWKV_PIECES + p) * RWKV_SUB + t, lanes] = 1.0
            z[RWKV_SUB, RWKV_GAMMA_SLOT + 8 * h + p, lanes] = 1.0
    return jnp.asarray(z, BF16)


def _split3(x):
    hi = x.astype(BF16).astype(F32)
    rem = x - hi
    mid = rem.astype(BF16).astype(F32)
    lo = (rem - mid).astype(BF16).astype(F32)
    return [hi, mid, lo]


def _operand_tile(xa, xb, extra):
    pa = jnp.concatenate(_split3(xa), axis=0)
    pb = jnp.concatenate(_split3(xb), axis=0)
    lane_lo = lax.broadcasted_iota(jnp.int32, pa.shape, 1) < HEAD_DIM
    head0 = jnp.where(lane_lo, pa, pltpu.roll(pb, HEAD_DIM, 1))
    head1 = jnp.where(lane_lo, pltpu.roll(pa, HEAD_DIM, 1), pb)
    tt = jnp.concatenate([head0, head1, extra], axis=0).T
    return tt[:HEAD_DIM].astype(BF16), tt[HEAD_DIM:].astype(BF16)


def _rwkv_prepare(refs, base, reverse):
    r_ref, _, kk_ref, lw_ref, kd_ref, bb_ref = refs
    rows = pl.ds(base, RWKV_SUB)
    r_, kk_ = r_ref[0, rows, :], kk_ref[0, rows, :]
    lw_, kd_, bb_ = lw_ref[0, 0, rows, :], kd_ref[0, 0, rows, :], bb_ref[0, 0, rows, :]
    row = lax.broadcasted_iota(jnp.int32, (RWKV_SUB, RWKV_SUB), 0)
    col = lax.broadcasted_iota(jnp.int32, (RWKV_SUB, RWKV_SUB), 1)
    tri = ((col >= row) if reverse else (col <= row)).astype(F32)
    cum = jnp.dot(tri, lw_, precision=lax.Precision.HIGHEST, preferred_element_type=F32)
    g = jnp.exp(cum)
    ginv = jnp.exp(-cum)
    gprev = jnp.exp(cum - lw_)
    last = 0 if reverse else RWKV_SUB - 1
    gam = jnp.concatenate(_split3(g[last:last + 1, :]) + [jnp.zeros((8 - RWKV_PIECES, LANES), F32)], axis=0)
    extra = jnp.concatenate([gam, pltpu.roll(gam, HEAD_DIM, 1), jnp.zeros((16, LANES), F32)], axis=0)
    g_a, g_b = _operand_tile(-kk_ * gprev, bb_ * ginv, extra)
    g_k, g_r = _operand_tile(kd_ * ginv, r_ * g, jnp.zeros((32, LANES), F32))
    return jnp.concatenate([g_a, g_b, g_k, g_r], axis=0)


def _rwkv_steps(g_all, v_ref, z_ref, y_ref, s_ref, d, base, reverse):
    rows = pl.ds(base, RWKV_SUB)
    v_ = v_ref[0, rows, :]
    st = s_ref[d]
    ys = [None] * RWKV_SUB
    order = list(range(RWKV_SUB))[::-1] if reverse else list(range(RWKV_SUB))
    for t0, t1 in zip(order[0::2], order[1::2]):
        sel = jnp.concatenate([z_ref[t0], z_ref[t1]], axis=1)
        cols = jnp.dot(g_all, sel, preferred_element_type=F32)
        for n, t in enumerate((t0, t1)):
            a_c, b_c, k_c, r_c = (cols[o * HEAD_DIM:(o + 1) * HEAD_DIM, n * LANES:(n + 1) * LANES]
                                  for o in range(4))
            sa = jnp.sum(st * a_c, axis=0, keepdims=True)
            st = st + b_c * sa + k_c * v_[t:t + 1, :]
            ys[t] = jnp.sum(st * r_c, axis=0, keepdims=True)
    s_ref[d] = st * jnp.dot(g_all[:HEAD_DIM], z_ref[RWKV_SUB], preferred_element_type=F32)
    y_ref[0, 0, rows, :] = jnp.concatenate(ys, axis=0)


def _rwkv_body(rf, vf, kkf, lwf, kdf, bbf, rb, vb, kkb, lwb, kdb, bbb, z_ref, yf_ref, yb_ref, s_ref, g_ref,
               *, tb):
    @pl.when(pl.program_id(2) == 0)
    def _():
        s_ref[...] = jnp.zeros_like(s_ref)

    nsub = tb // RWKV_SUB
    refs_f = (rf, vf, kkf, lwf, kdf, bbf)
    refs_b = (rb, vb, kkb, lwb, kdb, bbb)

    def base_f(c):
        return pl.multiple_of(c * RWKV_SUB, RWKV_SUB)

    def base_b(c):
        return pl.multiple_of((nsub - 1 - c) * RWKV_SUB, RWKV_SUB)

    g_ref[0, 0] = _rwkv_prepare(refs_f, base_f(0), False)
    g_ref[0, 1] = _rwkv_prepare(refs_b, base_b(0), True)

    def sub(c, carry):
        slot = c % 2
        nxt = jnp.minimum(c + 1, nsub - 1)
        g_f, g_b = g_ref[slot, 0], g_ref[slot, 1]
        g_ref[1 - slot, 0] = _rwkv_prepare(refs_f, base_f(nxt), False)
        g_ref[1 - slot, 1] = _rwkv_prepare(refs_b, base_b(nxt), True)
        _rwkv_steps(g_f, vf, z_ref, yf_ref, s_ref, 0, base_f(c), False)
        _rwkv_steps(g_b, vb, z_ref, yb_ref, s_ref, 1, base_b(c), True)
        return carry

    lax.fori_loop(0, nsub, sub, 0)


def _rwkv7_scan(r, lw, kd, v, kk, bb, *, tb=512, interpret=False):
    B, T, C = r.shape
    tb = min(tb, T)
    nt = T // tb
    assert T % tb == 0 and tb % RWKV_SUB == 0 and C % LANES == 0
    fwd3 = pl.BlockSpec((1, tb, LANES), lambda b, j, i: (b, i, j))
    bwd3 = pl.BlockSpec((1, tb, LANES), lambda b, j, i: (b, nt - 1 - i, j))
    fwd4 = pl.BlockSpec((1, 1, tb, LANES), lambda b, j, i: (0, b, i, j))
    bwd4 = pl.BlockSpec((1, 1, tb, LANES), lambda b, j, i: (1, b, nt - 1 - i, j))
    yf, yb = pl.pallas_call(
        functools.partial(_rwkv_body, tb=tb),
        out_shape=[jax.ShapeDtypeStruct((1, B, T, C), F32)] * 2,
        grid=(B, C // LANES, nt),
        in_specs=[fwd3, fwd3, fwd3, fwd4, fwd4, fwd4, bwd3, bwd3, bwd3, bwd4, bwd4, bwd4,
                  pl.BlockSpec((RWKV_SUB + 1, LANES, LANES), lambda b, j, i: (0, 0, 0))],
        out_specs=[pl.BlockSpec((1, 1, tb, LANES), lambda b, j, i: (0, b, i, j)),
                   pl.BlockSpec((1, 1, tb, LANES), lambda b, j, i: (0, b, nt - 1 - i, j))],
        scratch_shapes=[pltpu.VMEM((2, HEAD_DIM, LANES), F32),
                        pltpu.VMEM((2, 2, 4 * HEAD_DIM, LANES), BF16)],
        compiler_params=pltpu.CompilerParams(dimension_semantics=("parallel", "parallel", "arbitrary")),
        name="rwkv7_scan",
        interpret=interpret,
    )(r, v, kk, lw, kd, bb, r, v, kk, lw, kd, bb, _rwkv_selectors())
    return jnp.concatenate([yf, yb], axis=0)


HALO = 8


def _head_sums(x, bd_ref):
    bd = bd_ref[...]
    return sum(jnp.dot(piece.astype(BF16), bd, preferred_element_type=F32) for piece in _split3(x))


def _token_shift(cur, prev_row, next_row, mu):
    tb = cur.shape[0]
    rid = lax.broadcasted_iota(jnp.int32, cur.shape, 0)
    prev = jnp.where(rid == 0, prev_row, pltpu.roll(cur, 1, 0))
    nxt = jnp.where(rid == tb - 1, next_row, pltpu.roll(cur, tb - 1, 0))
    return cur + mu[0:1] * (prev - cur) + mu[1:2] * (nxt - cur)


def _rwkv_prep_body(x_ref, xp_ref, xn_ref, r_ref, rp_ref, rn_ref, k_ref, kp_ref, kn_ref, v_ref, vp_ref, vn_ref,
                    mux_ref, mur_ref, w1_ref, w2_ref, w0_ref, a1_ref, a2_ref, a0_ref, g1_ref, g2_ref,
                    kk_w_ref, ka_w_ref, bd_ref,
                    ro_ref, vo_ref, kko_ref, go_ref, lw_ref, kd_ref, bb_ref):
    i = pl.program_id(1)
    first = i == 0
    last = i == pl.num_programs(1) - 1

    def shifted(cur_ref, p_ref, n_ref, mu):
        prev_row = jnp.where(first, 0.0, p_ref[0, HALO - 1:HALO, :])
        next_row = jnp.where(last, 0.0, n_ref[0, 0:1, :])
        return _token_shift(cur_ref[0], prev_row, next_row, mu)

    r = shifted(r_ref, rp_ref, rn_ref, mur_ref[0])
    k = shifted(k_ref, kp_ref, kn_ref, mur_ref[1])
    v = shifted(v_ref, vp_ref, vn_ref, mur_ref[2])
    xw = shifted(x_ref, xp_ref, xn_ref, mux_ref[0]).astype(BF16)
    xa = shifted(x_ref, xp_ref, xn_ref, mux_ref[1]).astype(BF16)
    xg = shifted(x_ref, xp_ref, xn_ref, mux_ref[2]).astype(BF16)
    hg = jax.nn.sigmoid(jnp.dot(xg, g1_ref[...], preferred_element_type=F32)).astype(BF16)
    go_ref[0] = jnp.dot(hg, g2_ref[...], preferred_element_type=F32)
    kk = k * kk_w_ref[...]
    kk = kk / jnp.maximum(jnp.sqrt(_head_sums(kk * kk, bd_ref)), 1e-12)
    ro_ref[0] = r
    vo_ref[0] = v
    kko_ref[0] = kk
    for z in range(2):
        hw = jnp.tanh(jnp.dot(xw, w1_ref[z], preferred_element_type=F32)).astype(BF16)
        ha = jnp.dot(xa, a1_ref[z], preferred_element_type=F32).astype(BF16)
        w_raw = w0_ref[z:z + 1] + jnp.dot(hw, w2_ref[z], preferred_element_type=F32)
        lw_ref[z, 0] = -math.exp(-0.5) * jax.nn.sigmoid(w_raw)
        a = jax.nn.sigmoid(a0_ref[z:z + 1]
                           + jnp.dot(ha, a2_ref[z], preferred_element_type=F32))
        kd_ref[z, 0] = k * (1.0 + (a - 1.0) * ka_w_ref[...])
        bb_ref[z, 0] = kk * a


def _rwkv_post_body(ys_ref, r_ref, v_ref, kd_ref, g_ref, lng_ref, lnb_ref, rk_ref, bd_ref, o_ref):
    y = ys_ref[0, 0] + ys_ref[1, 0]
    inv_n = 1.0 / HEAD_DIM
    yc = y - _head_sums(y, bd_ref) * inv_n
    yn = yc * lax.rsqrt(_head_sums(yc * yc, bd_ref) * inv_n + GN_EPS) * lng_ref[...] + lnb_ref[...]
    r, v = r_ref[0], v_ref[0]
    bonus = (_head_sums(r * kd_ref[0, 0] * rk_ref[...], bd_ref)
             + _head_sums(r * kd_ref[1, 0] * rk_ref[...], bd_ref)) * v
    o_ref[0] = ((yn + bonus) * g_ref[0]).astype(o_ref.dtype)


def _rwkv7_mixer(u, proj, p, *, tb=256):
    B, T, D = u.shape
    C = D_RWKV
    tb = min(tb, T)
    nt = T // tb
    assert T % tb == 0 and tb % HALO == 0
    hb = tb // HALO

    def cur(width, col):
        return pl.BlockSpec((1, tb, width), lambda b, i: (b, i, col))

    def prev(width, col):
        return pl.BlockSpec((1, HALO, width), lambda b, i: (b, jnp.maximum(i * hb - 1, 0), col))

    def nxt(width, col):
        return pl.BlockSpec((1, HALO, width), lambda b, i: (b, jnp.minimum((i + 1) * hb, T // HALO - 1), col))

    def full(a):
        return pl.BlockSpec(a.shape, lambda b, i: (0,) * a.ndim)

    lane = np.arange(C) // HEAD_DIM
    bd = jnp.asarray(lane[:, None] == lane[None, :], BF16)
    weights = [p['rwkv_mu_x'], p['rwkv_mu_rkv'], p['rwkv_w1'].astype(BF16), p['rwkv_w2'].astype(BF16), p['rwkv_w0'],
               p['rwkv_a1'].astype(BF16), p['rwkv_a2'].astype(BF16), p['rwkv_a0'], p['rwkv_g1'].astype(BF16), p['rwkv_g2'].astype(BF16),
               p['rwkv_k_k'][None], p['rwkv_k_a'][None], bd]
    one = jax.ShapeDtypeStruct((B, T, C), F32)
    two = jax.ShapeDtypeStruct((2, B, T, C), F32)
    out1 = pl.BlockSpec((1, tb, C), lambda b, i: (b, i, 0))
    out2 = pl.BlockSpec((2, 1, tb, C), lambda b, i: (0, b, i, 0))
    r, v, kk, g, lw, kd, bb = pl.pallas_call(
        _rwkv_prep_body,
        out_shape=[one, one, one, one, two, two, two],
        grid=(B, nt),
        in_specs=[cur(D, 0), prev(D, 0), nxt(D, 0)]
        + [spec(C, col) for col in range(3) for spec in (cur, prev, nxt)]
        + [full(w) for w in weights],
        out_specs=[out1, out1, out1, out1, out2, out2, out2],
        compiler_params=pltpu.CompilerParams(dimension_semantics=("parallel", "parallel")),
        name="rwkv7_prepare",
    )(u, u, u, *([proj] * 9), *weights)
    ys = _rwkv7_scan(r, lw, kd, v, kk, bb)
    post_w = [p['rwkv_lnx_g'][None], p['rwkv_lnx_b'][None], p['rwkv_r_k'].reshape(1, C), bd]
    return pl.pallas_call(
        _rwkv_post_body,
        out_shape=jax.ShapeDtypeStruct((B, T, C), BF16),
        grid=(B, nt),
        in_specs=[out2, out1, out1, out2, out1] + [full(w) for w in post_w],
        out_specs=out1,
        compiler_params=pltpu.CompilerParams(dimension_semantics=("parallel", "parallel")),
        name="rwkv7_output",
    )(ys, r, v, kd, g, *post_w)


NA_ROWS_PER_STEP = 8


def _na_body(q_ref, k_ref, v_ref, bias_ref, mask_ref, o_ref, *, rows):
    kr = NA_WIN_ROWS
    win = kr * GRID_W
    valid = mask_ref[...] != 0
    lane = lax.broadcasted_iota(jnp.int32, (GRID_W, LANES), 1)
    for ii in range(NA_ROWS_PER_STEP):
        i = pl.program_id(1) * NA_ROWS_PER_STEP + ii
        start = jnp.clip(i - kr // 2, 0, rows - kr)
        d = i - start
        krows = pl.ds(pl.multiple_of(start * GRID_W, GRID_W), win)
        for pair in range(D_NA // LANES):
            lanes = slice(pair * LANES, (pair + 1) * LANES)
            q2 = q_ref[0, ii * GRID_W:(ii + 1) * GRID_W, lanes] * (HEAD_DIM ** -0.5)
            k2 = k_ref[0, krows, lanes]
            v2 = v_ref[0, krows, lanes]
            outs = []
            for hh in range(LANES // HEAD_DIM):
                in_head = (lane >= hh * HEAD_DIM) & (lane < (hh + 1) * HEAD_DIM)
                qh = jnp.where(in_head, q2, 0.0).astype(BF16)
                s = lax.dot_general(qh, k2, (((1,), (1,)), ((), ())), preferred_element_type=F32)
                s = s + bias_ref[d, pair * (LANES // HEAD_DIM) + hh]
                s = jnp.where(valid, s, -1e30)
                m = jnp.max(s, axis=-1, keepdims=True)
                e = jnp.exp(s - m)
                p = e / jnp.sum(e, axis=-1, keepdims=True)
                outs.append(jnp.dot(p.astype(BF16), v2, preferred_element_type=F32))
            o_ref[0, ii * GRID_W:(ii + 1) * GRID_W, lanes] = jnp.where(lane < HEAD_DIM, outs[0], outs[1])


def _neighbourhood_attention_pallas(q, k, v, rpb, *, interpret=False):
    B, T, _ = q.shape
    rows = T // GRID_W
    kr, kc = NA_WIN_ROWS, NA_WIN_COLS
    assert rows >= kr and rows % NA_ROWS_PER_STEP == 0
    ci = jnp.arange(GRID_W)
    col_start = jnp.clip(ci - kc // 2, 0, GRID_W - kc)
    col_valid = (ci[None] >= col_start[:, None]) & (ci[None] < col_start[:, None] + kc)
    mask = jnp.tile(col_valid.astype(jnp.int32), (1, kr))
    dc_idx = jnp.clip(ci[None] - ci[:, None] + kc - 1, 0, 2 * kc - 2)
    dr_idx = jnp.arange(kr)[None, :] - jnp.arange(kr)[:, None] + kr - 1
    onehot = (jnp.arange(2 * kc - 1)[:, None] == dc_idx.reshape(1, -1)).astype(F32)
    bias = jnp.dot(rpb[:, dr_idx].reshape(-1, 2 * kc - 1), onehot, precision=lax.Precision.HIGHEST)
    bias = bias.reshape(NA_HEADS, kr, kr, GRID_W, GRID_W)
    bias = jnp.transpose(bias, (1, 0, 3, 2, 4)).reshape(kr, NA_HEADS, GRID_W, kr * GRID_W)
    tq = NA_ROWS_PER_STEP * GRID_W
    return pl.pallas_call(
        functools.partial(_na_body, rows=rows),
        out_shape=jax.ShapeDtypeStruct((B, T, D_NA), F32),
        grid=(B, rows // NA_ROWS_PER_STEP),
        in_specs=[pl.BlockSpec((1, tq, D_NA), lambda b, i: (b, i, 0)),
                  pl.BlockSpec((1, T, D_NA), lambda b, i: (b, 0, 0)),
                  pl.BlockSpec((1, T, D_NA), lambda b, i: (b, 0, 0)),
                  pl.BlockSpec((kr, NA_HEADS, GRID_W, kr * GRID_W), lambda b, i: (0, 0, 0, 0)),
                  pl.BlockSpec((GRID_W, kr * GRID_W), lambda b, i: (0, 0))],
        out_specs=pl.BlockSpec((1, tq, D_NA), lambda b, i: (b, i, 0)),
        compiler_params=pltpu.CompilerParams(dimension_semantics=("parallel", "arbitrary")),
        name="neighbourhood_attention",
        interpret=interpret,
    )(q, k.astype(BF16), v.astype(BF16), bias, mask)


def _hyena_filter_spectra(L, p):
    t = jnp.linspace(0.0, 1.0, L, dtype=F32)[:, None]
    n_bands = (HY_EMB - 1) // 2
    omega = 2.0 * math.pi * jnp.arange(L, dtype=F32)[:, None] / L
    bands = jnp.linspace(1e-4, n_bands - 1, n_bands, dtype=F32)[None]
    z = jnp.concatenate([t, jnp.cos(bands * omega), -jnp.sin(bands * omega)], -1)
    freq = p['hy_freq']
    h = jnp.sin(freq * (z @ p['hy_w1'] + p['hy_b1']))
    h = jnp.sin(freq * (h @ p['hy_w2'] + p['hy_b2']))
    h = jnp.sin(freq * (h @ p['hy_w3'] + p['hy_b3']))
    h = (h @ p['hy_w4']).reshape(L, HY_ORDER, 2, D_HY)
    deltas = jnp.abs(jnp.linspace(math.log(HY_TOL) / HY_FAST_PCT, math.log(HY_TOL) / HY_SLOW_PCT, D_HY, dtype=F32))
    h = h * jnp.exp(-t * deltas)[:, None, None, :]
    h_f, h_b = h[:, :, 0], h[:, :, 1]
    kern = jnp.concatenate([h_f, jnp.zeros((1, HY_ORDER, D_HY), F32), h_b[1:][::-1]], 0)
    kern = kern / jnp.sum(jnp.abs(kern), 0, keepdims=True)
    return jnp.fft.fft(kern, axis=0)


def _dft_tables(R):
    n = jnp.arange(R, dtype=jnp.int32)
    ang = (2.0 * math.pi / R) * ((n[:, None] * n[None, :]) % R).astype(F32)
    c, s = jnp.cos(ang), jnp.sin(ang)
    first = jnp.concatenate([c, -s], axis=0)[:, :R // 2]
    last = jnp.concatenate([c, -s], axis=1)[:R // 2]
    m = (R * n[None, :, None] * n[None, None, :] + n[:, None, None] * n[None, None, :]) % (R * R)
    phi = (2.0 * math.pi / (R * R)) * m.astype(F32)
    cp, sp = jnp.cos(phi), jnp.sin(phi)
    fwd = jnp.concatenate([jnp.concatenate([cp, sp], axis=2), jnp.concatenate([-sp, cp], axis=2)], axis=1)
    cpt, spt = jnp.swapaxes(cp, 1, 2), jnp.swapaxes(sp, 1, 2)
    inv = jnp.concatenate([jnp.concatenate([cpt, -spt], axis=2), jnp.concatenate([spt, cpt], axis=2)], axis=1)
    return first.astype(BF16), fwd.astype(BF16), inv.astype(BF16), last.astype(BF16)


def _conv_first_body(f_ref, z_ref, o_ref):
    o_ref[0] = jnp.dot(f_ref[...], z_ref[0].astype(BF16), preferred_element_type=F32).astype(o_ref.dtype)


def _conv_mid_body(a_ref, mf_ref, mi_ref, h_ref, o_ref, *, R, kb):
    for j in range(kb):
        a = jnp.concatenate([a_ref[0, 0, j], a_ref[0, 1, j]], axis=0)
        x = jnp.dot(mf_ref[j], a, preferred_element_type=F32)
        xr, xi = x[:R], x[R:]
        hr, hi = h_ref[0, j], h_ref[1, j]
        y = jnp.concatenate([xr * hr - xi * hi, xr * hi + xi * hr], axis=0).astype(BF16)
        b = jnp.dot(mi_ref[j], y, preferred_element_type=F32)
        o_ref[0, 0, j] = b[:R].astype(o_ref.dtype)
        o_ref[0, 1, j] = b[R:].astype(o_ref.dtype)


def _conv_last_body(f_ref, b_ref, z_ref, gate_ref, bias_ref, o_ref):
    y = jnp.dot(f_ref[...], b_ref[0], preferred_element_type=F32)
    o_ref[0] = gate_ref[0] * (y + z_ref[0] * bias_ref[...])


def _gated_long_conv(z, gate, hm, bias, tables, *, tn=4096, kb=4, interpret=False):
    B, L, C = z.shape
    R = math.isqrt(2 * L)
    assert R * R == 2 * L and R % (2 * kb) == 0
    first, fwd, inv, last = tables
    tn = min(tn, R * C)
    nt = (R * C) // tn
    z2 = z.reshape(B, R // 2, R * C)
    a = pl.pallas_call(
        _conv_first_body,
        out_shape=jax.ShapeDtypeStruct((B, 2 * R, R * C), BF16),
        grid=(B, nt),
        in_specs=[pl.BlockSpec((2 * R, R // 2), lambda b, j: (0, 0)),
                  pl.BlockSpec((1, R // 2, tn), lambda b, j: (b, 0, j))],
        out_specs=pl.BlockSpec((1, 2 * R, tn), lambda b, j: (b, 0, j)),
        name="long_conv_first",
        interpret=interpret,
    )(first, z2)
    bmid = pl.pallas_call(
        functools.partial(_conv_mid_body, R=R, kb=kb),
        out_shape=jax.ShapeDtypeStruct((B, 2, R, R, C), BF16),
        grid=(B, R // kb),
        in_specs=[pl.BlockSpec((1, 2, kb, R, C), lambda b, k: (b, 0, k, 0, 0)),
                  pl.BlockSpec((kb, 2 * R, 2 * R), lambda b, k: (k, 0, 0)),
                  pl.BlockSpec((kb, 2 * R, 2 * R), lambda b, k: (k, 0, 0)),
                  pl.BlockSpec((2, kb, R, C), lambda b, k: (0, k, 0, 0))],
        out_specs=pl.BlockSpec((1, 2, kb, R, C), lambda b, k: (b, 0, k, 0, 0)),
        name="long_conv_mid",
        interpret=interpret,
    )(a.reshape(B, 2, R, R, C), fwd, inv, hm)
    out = pl.pallas_call(
        _conv_last_body,
        out_shape=jax.ShapeDtypeStruct((B, R // 2, R * C), F32),
        grid=(B, nt),
        in_specs=[pl.BlockSpec((R // 2, 2 * R), lambda b, j: (0, 0)),
                  pl.BlockSpec((1, 2 * R, tn), lambda b, j: (b, 0, j)),
                  pl.BlockSpec((1, R // 2, tn), lambda b, j: (b, 0, j)),
                  pl.BlockSpec((1, R // 2, tn), lambda b, j: (b, 0, j)),
                  pl.BlockSpec((1, tn), lambda b, j: (0, j))],
        out_specs=pl.BlockSpec((1, R // 2, tn), lambda b, j: (b, 0, j)),
        name="long_conv_last",
        interpret=interpret,
    )(last, bmid.reshape(B, 2 * R, R * C), z2, gate.reshape(B, R // 2, R * C), jnp.tile(bias, R)[None])
    return out.reshape(B, L, C)


def _short_conv_body(u_ref, up_ref, un_ref, sw_ref, sb_ref, x1_ref, x2_ref, v_ref):
    i = pl.program_id(1)
    cur = u_ref[0]
    tb = cur.shape[0]
    prev_row = jnp.where(i == 0, 0.0, up_ref[0, HALO - 1:HALO, :])
    next_row = jnp.where(i == pl.num_programs(1) - 1, 0.0, un_ref[0, 0:1, :])
    rid = lax.broadcasted_iota(jnp.int32, cur.shape, 0)
    prev = jnp.where(rid == 0, prev_row, pltpu.roll(cur, 1, 0))
    nxt = jnp.where(rid == tb - 1, next_row, pltpu.roll(cur, tb - 1, 0))
    u = prev * sw_ref[0:1] + cur * sw_ref[1:2] + nxt * sw_ref[2:3] + sb_ref[...]
    x1_ref[0] = u[:, :D_HY]
    x2_ref[0] = u[:, D_HY:2 * D_HY]
    v_ref[0] = u[:, 2 * D_HY:]


def _hyena_short_conv(proj, sw, sb, *, tb=512):
    B, T, n_in = proj.shape
    width = 3 * D_HY
    col = (3 * D_RWKV + 3 * D_NA) // width
    assert col * width == 3 * D_RWKV + 3 * D_NA
    tb = min(tb, T)
    hb = tb // HALO
    out = jax.ShapeDtypeStruct((B, T, D_HY), F32)
    ospec = pl.BlockSpec((1, tb, D_HY), lambda b, i: (b, i, 0))
    return pl.pallas_call(
        _short_conv_body,
        out_shape=[out, out, out],
        grid=(B, T // tb),
        in_specs=[pl.BlockSpec((1, tb, width), lambda b, i: (b, i, col)),
                  pl.BlockSpec((1, HALO, width), lambda b, i: (b, jnp.maximum(i * hb - 1, 0), col)),
                  pl.BlockSpec((1, HALO, width), lambda b, i: (b, jnp.minimum((i + 1) * hb, T // HALO - 1), col)),
                  pl.BlockSpec(sw.shape, lambda b, i: (0, 0)),
                  pl.BlockSpec((1, width), lambda b, i: (0, 0))],
        out_specs=[ospec, ospec, ospec],
        compiler_params=pltpu.CompilerParams(dimension_semantics=("parallel", "parallel")),
        name="hyena_short_conv",
    )(proj, proj, proj, sw, sb[None])


def _hyena_mixer(proj, p):
    T = proj.shape[1]
    R = math.isqrt(2 * T)
    x1, x2, v = _hyena_short_conv(proj, p['hy_short_w'], p['hy_short_b'])
    spec = _hyena_filter_spectra(T, p) * (1.0 / (2 * T))
    spec = jnp.swapaxes(spec.reshape(R, R, HY_ORDER, D_HY), 0, 1)
    tables = _dft_tables(R)
    z = v
    for o, gate in enumerate((x1, x2)):
        hm = jnp.stack([jnp.real(spec[:, :, o]), jnp.imag(spec[:, :, o])])
        z = _gated_long_conv(z, gate, hm, p['hy_bias'][o], tables)
    return z


def _route(aff):
    n = aff.shape[0]
    cap = (CAPACITY_FACTOR * n) // N_EXPERTS
    return lax.top_k(aff.T, cap)


def _mix_body(x_ref, ya_ref, yb_ref, yc_ref, g_ref, wa_ref, wb_ref, wc_ref, wo_ref, lg_ref, lb_ref, wr_ref,
              o_ref, ob_ref, aff_ref):
    g = jax.nn.sigmoid(g_ref[...])
    m = (g[:, :D_MODEL] * jnp.dot(ya_ref[...].astype(BF16), wa_ref[...], preferred_element_type=F32)
         + g[:, D_MODEL:2 * D_MODEL] * jnp.dot(yb_ref[...].astype(BF16), wb_ref[...], preferred_element_type=F32)
         + g[:, 2 * D_MODEL:] * jnp.dot(yc_ref[...].astype(BF16), wc_ref[...], preferred_element_type=F32))
    h = ALPHA * x_ref[...] + jnp.dot(m.astype(BF16), wo_ref[...], preferred_element_type=F32)
    y = _layer_norm(h, lg_ref[...], lb_ref[...])
    o_ref[...] = y
    ob_ref[...] = y.astype(BF16)
    logits = jnp.dot(y, wr_ref[...], precision=lax.Precision.HIGHEST, preferred_element_type=F32)
    e = jnp.exp(logits - jnp.max(logits, -1, keepdims=True))
    aff_ref[...] = e / jnp.sum(e, -1, keepdims=True)


def _mix_and_norm(x, y_a, y_b, y_c, proj, p, *, tm=512):
    n, d = x.shape
    assert n % tm == 0 and proj.shape[1] == 2 * N_BRANCH * d
    row = lambda w: pl.BlockSpec((tm, w), lambda i: (i, 0))
    full = lambda a: pl.BlockSpec(a.shape, lambda i: (0,) * a.ndim)
    wa, wb, wc, wo = (p[k].astype(BF16) for k in ('w_branch_a', 'w_branch_b', 'w_branch_c', 'w_out'))
    lg, lb = p['ln1_g'][None], p['ln1_b'][None]
    return pl.pallas_call(
        _mix_body,
        out_shape=[jax.ShapeDtypeStruct((n, d), F32), jax.ShapeDtypeStruct((n, d), BF16),
                   jax.ShapeDtypeStruct((n, N_EXPERTS), F32)],
        grid=(n // tm,),
        in_specs=[row(d), row(D_RWKV), row(D_NA), row(D_HY),
                  pl.BlockSpec((tm, N_BRANCH * d), lambda i: (i, 1)),
                  full(wa), full(wb), full(wc), full(wo), full(lg), full(lb), full(p['w_router'])],
        out_specs=[row(d), row(d), row(N_EXPERTS)],
        compiler_params=pltpu.CompilerParams(dimension_semantics=("parallel",)),
        name="mix_norm_route",
    )(x, y_a, y_b, y_c, proj, wa, wb, wc, wo, lg, lb, p['w_router'])


def _mixers(x, p):
    B, T, D = x.shape
    n = B * T
    proj2 = _matmul(x.reshape(n, D), p['w_in'])
    proj = proj2.reshape(B, T, -1)
    qb, kb, vb = (proj[..., 3 * D_RWKV + j * D_NA:3 * D_RWKV + (j + 1) * D_NA] for j in range(3))
    y_a = _rwkv7_mixer(x, proj, p)
    y_b = _neighbourhood_attention_pallas(qb, kb, vb, p['na_rpb'])
    y_c = _hyena_mixer(proj, p)
    return _mix_and_norm(x.reshape(n, D), y_a.reshape(n, -1), y_b.reshape(n, -1), y_c.reshape(n, -1), proj2, p)


def _encoder_layer(xs, p):
    mixed = [_mixers(x, p) for x in xs]
    routes = [_route(aff) for _, _, aff in mixed]
    xe = jnp.concatenate([jnp.take(xb, idx, axis=0) for (_, xb, _), (_, idx) in zip(mixed, routes)], axis=1)
    gate = jnp.concatenate([g for g, _ in routes], axis=1)[..., None]
    ye = _expert_ffn(xe, gate, p['w_exp_gate'], p['w_exp_up'], p['w_exp_down'])
    outs, off = [], 0
    for x, (x1, _, _), (_, idx) in zip(xs, mixed, routes):
        cap = idx.shape[1]
        ffn = jnp.zeros_like(x1).at[idx.reshape(-1)].add(ye[:, off:off + cap].reshape(-1, x1.shape[-1]))
        off += cap
        outs.append(_ln(x1, p['ln2_g'], p['ln2_b'], residual=ffn).reshape(x.shape))
    return outs


def kernel(x_prompt, x_sample, ln_in_g, ln_in_b, w_in, rwkv_mu_rkv, rwkv_mu_x, rwkv_w0, rwkv_w1, rwkv_w2,
           rwkv_a0, rwkv_a1, rwkv_a2, rwkv_g1, rwkv_g2, rwkv_k_k, rwkv_k_a, rwkv_r_k, rwkv_lnx_g, rwkv_lnx_b,
           na_rpb, hy_short_w, hy_short_b, hy_w1, hy_b1, hy_w2, hy_b2, hy_w3, hy_b3, hy_w4, hy_freq, hy_bias,
           w_branch_a, w_branch_b, w_branch_c, w_out, ln1_g, ln1_b, w_router, w_exp_gate, w_exp_up,
           w_exp_down, ln2_g, ln2_b):
    stacked = {
        'w_in': w_in, 'rwkv_mu_rkv': rwkv_mu_rkv, 'rwkv_mu_x': rwkv_mu_x, 'rwkv_w0': rwkv_w0,
        'rwkv_w1': rwkv_w1, 'rwkv_w2': rwkv_w2, 'rwkv_a0': rwkv_a0, 'rwkv_a1': rwkv_a1, 'rwkv_a2': rwkv_a2,
        'rwkv_g1': rwkv_g1, 'rwkv_g2': rwkv_g2, 'rwkv_k_k': rwkv_k_k, 'rwkv_k_a': rwkv_k_a,
        'rwkv_r_k': rwkv_r_k, 'rwkv_lnx_g': rwkv_lnx_g, 'rwkv_lnx_b': rwkv_lnx_b, 'na_rpb': na_rpb,
        'hy_short_w': hy_short_w, 'hy_short_b': hy_short_b, 'hy_w1': hy_w1, 'hy_b1': hy_b1,
        'hy_w2': hy_w2, 'hy_b2': hy_b2, 'hy_w3': hy_w3, 'hy_b3': hy_b3, 'hy_w4': hy_w4,
        'hy_freq': hy_freq, 'hy_bias': hy_bias, 'w_branch_a': w_branch_a, 'w_branch_b': w_branch_b,
        'w_branch_c': w_branch_c, 'w_out': w_out, 'ln1_g': ln1_g, 'ln1_b': ln1_b, 'w_router': w_router,
        'w_exp_gate': w_exp_gate, 'w_exp_up': w_exp_up, 'w_exp_down': w_exp_down,
        'ln2_g': ln2_g, 'ln2_b': ln2_b,
    }
    xs = [_ln(x.reshape(-1, D_MODEL), ln_in_g, ln_in_b).reshape(x.shape) for x in (x_prompt, x_sample)]
    for l in range(DEPTH):
        xs = _encoder_layer(xs, {name: arr[l] for name, arr in stacked.items()})
    return tuple(xs)
```

```python
import functools
import math

import jax
import jax.numpy as jnp
import numpy as np
from jax import lax
from jax.experimental import pallas as pl
from jax.experimental.pallas import tpu as pltpu

D_MODEL = 1024
DEPTH = 2
GRID_W = 64
HEAD_DIM = 64
D_RWKV = D_MODEL // 2
RWKV_HEADS = D_RWKV // HEAD_DIM
D_NA = D_MODEL // 4
NA_HEADS = D_NA // HEAD_DIM
D_HY = D_MODEL // 4
N_BRANCH = 3
GN_EPS = 64e-5
NA_WIN_ROWS = 8
NA_WIN_COLS = 16
HY_ORDER = 2
HY_EMB = 33
HY_TOL = 1e-2
HY_FAST_PCT = 0.3
HY_SLOW_PCT = 1.5
N_EXPERTS = 16
CAPACITY_FACTOR = 2
ALPHA = (2 * DEPTH) ** 0.25
LN_EPS = 1e-5

F32 = jnp.float32
BF16 = jnp.bfloat16
EXPERT_FFN_VMEM_BYTES = 52 * 1024 * 1024


def _mm_body(x_ref, w_ref, o_ref):
    o_ref[...] = jnp.dot(x_ref[...], w_ref[...], preferred_element_type=F32).astype(o_ref.dtype)


def _matmul(x, w, *, tm=1024, tn=512, out_dtype=F32):
    m, k = x.shape
    _, n = w.shape
    tm = min(tm, m)
    tn = min(tn, n)
    assert m % tm == 0 and n % tn == 0
    return pl.pallas_call(
        _mm_body,
        out_shape=jax.ShapeDtypeStruct((m, n), out_dtype),
        grid=(m // tm, n // tn),
        in_specs=[pl.BlockSpec((tm, k), lambda i, j: (i, 0)),
                  pl.BlockSpec((k, tn), lambda i, j: (0, j))],
        out_specs=pl.BlockSpec((tm, tn), lambda i, j: (i, j)),
        name="dense_matmul",
    )(x.astype(BF16), w.astype(BF16))


def _expert_ffn_body(x_ref, wg_ref, wu_ref, wd_ref, gate_ref, o_ref, acc_ref):
    f = pl.program_id(2)

    @pl.when(f == 0)
    def _():
        acc_ref[...] = jnp.zeros_like(acc_ref)

    xb = x_ref[0]
    hg = jnp.dot(xb, wg_ref[0].astype(BF16), preferred_element_type=F32)
    hu = jnp.dot(xb, wu_ref[0].astype(BF16), preferred_element_type=F32)
    h = (hg * jax.nn.sigmoid(hg) * hu).astype(BF16)
    acc_ref[...] += jnp.dot(h, wd_ref[0].astype(BF16), preferred_element_type=F32)

    @pl.when(f == pl.num_programs(2) - 1)
    def _():
        o_ref[0] = acc_ref[...] * gate_ref[0]


def _expert_ffn(xe, gate, w_gate, w_up, w_down, *, tm=2048, tf=256):
    e, m, d = xe.shape
    ff = w_gate.shape[-1]
    tm = min(tm, m)
    assert m % tm == 0 and ff % tf == 0
    return pl.pallas_call(
        _expert_ffn_body,
        out_shape=jax.ShapeDtypeStruct((e, m, d), F32),
        grid=(e, m // tm, ff // tf),
        in_specs=[pl.BlockSpec((1, tm, d), lambda g, i, f: (g, i, 0)),
                  pl.BlockSpec((1, d, tf), lambda g, i, f: (g, 0, f)),
                  pl.BlockSpec((1, d, tf), lambda g, i, f: (g, 0, f)),
                  pl.BlockSpec((1, tf, d), lambda g, i, f: (g, f, 0)),
                  pl.BlockSpec((1, tm, 1), lambda g, i, f: (g, i, 0))],
        out_specs=pl.BlockSpec((1, tm, d), lambda g, i, f: (g, i, 0)),
        scratch_shapes=[pltpu.VMEM((tm, d), F32)],
        compiler_params=pltpu.CompilerParams(
            dimension_semantics=("parallel", "parallel", "arbitrary"), vmem_limit_bytes=EXPERT_FFN_VMEM_BYTES),
        name="expert_ffn",
    )(xe, w_gate, w_up, w_down, gate)


def _layer_norm(h, g, b):
    hc = h - jnp.mean(h, -1, keepdims=True)
    var = jnp.mean(hc * hc, -1, keepdims=True)
    return hc * lax.rsqrt(var + LN_EPS) * g + b


def _ln_body(x_ref, g_ref, b_ref, o_ref):
    o_ref[...] = _layer_norm(x_ref[...], g_ref[...], b_ref[...])


def _ln_residual_body(x_ref, r_ref, g_ref, b_ref, o_ref):
    o_ref[...] = _layer_norm(ALPHA * x_ref[...] + r_ref[...], g_ref[...], b_ref[...])


def _ln(x, g, b, residual=None, *, tm=1024):
    n, d = x.shape
    tm = min(tm, n)
    assert n % tm == 0
    row = pl.BlockSpec((tm, d), lambda i: (i, 0))
    vec = pl.BlockSpec((1, d), lambda i: (0, 0))
    args = (x,) if residual is None else (x, residual)
    return pl.pallas_call(
        _ln_body if residual is None else _ln_residual_body,
        out_shape=jax.ShapeDtypeStruct((n, d), F32),
        grid=(n // tm,),
        in_specs=[row] * len(args) + [vec, vec],
        out_specs=row,
        compiler_params=pltpu.CompilerParams(dimension_semantics=("parallel",)),
        name="layer_norm",
    )(*args, g[None], b[None])


RWKV_SUB = 16
RWKV_PAIRS = 2
LANES = 128
RWKV_PIECES = 3
RWKV_GAMMA_SLOT = 2 * RWKV_PIECES * RWKV_SUB


def _rwkv_selectors():
    z = np.zeros((RWKV_SUB + 1, LANES, LANES), np.float32)
    for h in range(LANES // HEAD_DIM):
        lanes = slice(h * HEAD_DIM, (h + 1) * HEAD_DIM)
        for p in range(RWKV_PIECES):
            for t in range(RWKV_SUB):
                z[t, (h * RWKV_PIECES + p) * RWKV_SUB + t, lanes] = 1.0
            z[RWKV_SUB, RWKV_GAMMA_SLOT + 8 * h + p, lanes] = 1.0
    return jnp.asarray(z, BF16)


def _split3(x):
    hi = x.astype(BF16).astype(F32)
    rem = x - hi
    mid = rem.astype(BF16).astype(F32)
    lo = (rem - mid).astype(BF16).astype(F32)
    return [hi, mid, lo]


def _operand_tile(xa, xb, extra):
    pa = jnp.concatenate(_split3(xa), axis=0)
    pb = pa if xb is None else jnp.concatenate(_split3(xb), axis=0)
    lane_lo = lax.broadcasted_iota(jnp.int32, pa.shape, 1) < HEAD_DIM
    head0 = jnp.where(lane_lo, pa, pltpu.roll(pb, HEAD_DIM, 1))
    head1 = jnp.where(lane_lo, pltpu.roll(pa, HEAD_DIM, 1), pb)
    tt = jnp.concatenate([head0, head1, extra], axis=0).T
    return tt[:HEAD_DIM].astype(BF16), tt[HEAD_DIM:].astype(BF16)


def _rwkv_prepare(refs, base, lanes, reverse):
    a_ref, b_ref, k_ref, r_ref, g_ref = refs
    rows = pl.ds(base, RWKV_SUB)
    last = 0 if reverse else RWKV_SUB - 1
    g_last = g_ref[0, 0, rows, lanes][last:last + 1]
    gam = jnp.concatenate(_split3(g_last) + [jnp.zeros((8 - RWKV_PIECES, LANES), F32)], axis=0)
    extra = jnp.concatenate([gam, pltpu.roll(gam, HEAD_DIM, 1), jnp.zeros((16, LANES), F32)], axis=0)
    g_a, g_b = _operand_tile(a_ref[0, 0, rows, lanes], b_ref[0, 0, rows, lanes], extra)
    g_k, _ = _operand_tile(k_ref[0, 0, rows, lanes], None, jnp.zeros((32, LANES), F32))
    rg = r_ref[0, 0, rows, lanes]
    pad = jnp.zeros((HEAD_DIM - RWKV_SUB, LANES), F32)
    r_t = jnp.concatenate([rg, pad, pltpu.roll(rg, HEAD_DIM, 1), pad], axis=0).T[:HEAD_DIM]
    return jnp.concatenate([g_a, g_b, g_k], axis=0), r_t


def _rwkv_steps(streams, z_ref, s_ref):
    head_lane = (lax.broadcasted_iota(jnp.int32, (HEAD_DIM, LANES), 1) // HEAD_DIM) * HEAD_DIM
    lanes = [slice(q * LANES, (q + 1) * LANES) for _, _, _, _, _, q, _, _ in streams]
    vs = [s[2][0, pl.ds(s[6], RWKV_SUB), ln] for s, ln in zip(streams, lanes)]
    sts = [s_ref[d, q] for _, _, _, _, d, q, _, _ in streams]
    ys = [[None] * RWKV_SUB for _ in streams]
    for p in range(RWKV_SUB // 2):
        cols = []
        for g_all, _, _, _, _, _, _, reverse in streams:
            ts = (RWKV_SUB - 1 - 2 * p, RWKV_SUB - 2 - 2 * p) if reverse else (2 * p, 2 * p + 1)
            sel = jnp.concatenate([z_ref[ts[0]], z_ref[ts[1]]], axis=1)
            cols.append((ts, jnp.dot(g_all, sel, preferred_element_type=F32)))
        for n in range(2):
            for i, (_, r_t, _, _, _, _, _, _) in enumerate(streams):
                ts, c = cols[i]
                t = ts[n]
                a_c, b_c, k_c = (c[o * HEAD_DIM:(o + 1) * HEAD_DIM, n * LANES:(n + 1) * LANES] for o in range(3))
                r_c = jnp.take_along_axis(r_t, head_lane + t, axis=1)
                sa = jnp.sum(sts[i] * a_c, axis=0, keepdims=True)
                sts[i] = sts[i] + b_c * sa + k_c * vs[i][t:t + 1, :]
                ys[i][t] = jnp.sum(sts[i] * r_c, axis=0, keepdims=True)
    for i, (g_all, _, _, y_ref, d, q, base, _) in enumerate(streams):
        s_ref[d, q] = sts[i] * jnp.dot(g_all[:HEAD_DIM], z_ref[RWKV_SUB], preferred_element_type=F32)
        y_ref[0, 0, pl.ds(base, RWKV_SUB), lanes[i]] = jnp.concatenate(ys[i], axis=0)


def _rwkv_body(af, bf, kf, rf, gf, vf, ab, bb, kb, rb, gb, vb, z_ref, yf_ref, yb_ref, s_ref, g_ref,
               rt_ref, *, tb):
    @pl.when(pl.program_id(2) == 0)
    def _():
        s_ref[...] = jnp.zeros_like(s_ref)

    nsub = tb // RWKV_SUB
    refs_f = (af, bf, kf, rf, gf)
    refs_b = (ab, bb, kb, rb, gb)

    def base_f(c):
        return pl.multiple_of(c * RWKV_SUB, RWKV_SUB)

    def base_b(c):
        return pl.multiple_of((nsub - 1 - c) * RWKV_SUB, RWKV_SUB)

    def prepare(slot, c):
        for q in range(RWKV_PAIRS):
            lanes = slice(q * LANES, (q + 1) * LANES)
            g_ref[slot, 0, q], rt_ref[slot, 0, q] = _rwkv_prepare(refs_f, base_f(c), lanes, False)
            g_ref[slot, 1, q], rt_ref[slot, 1, q] = _rwkv_prepare(refs_b, base_b(c), lanes, True)

    prepare(0, 0)

    def sub(c, carry):
        slot = c % 2
        nxt = jnp.minimum(c + 1, nsub - 1)
        streams = []
        for q in range(RWKV_PAIRS):
            streams.append((g_ref[slot, 0, q], rt_ref[slot, 0, q], vf, yf_ref, 0, q, base_f(c), False))
            streams.append((g_ref[slot, 1, q], rt_ref[slot, 1, q], vb, yb_ref, 1, q, base_b(c), True))
        prepare(1 - slot, nxt)
        _rwkv_steps(streams, z_ref, s_ref)
        return carry

    lax.fori_loop(0, nsub, sub, 0)


def _rwkv7_scan(a, b, k, r, g, v, *, tb=512, interpret=False):
    B, T, C = v.shape
    tb = min(tb, T)
    nt = T // tb
    width = RWKV_PAIRS * LANES
    assert T % tb == 0 and tb % RWKV_SUB == 0 and C % width == 0
    fwd3 = pl.BlockSpec((1, tb, width), lambda b, j, i: (b, i, j))
    bwd3 = pl.BlockSpec((1, tb, width), lambda b, j, i: (b, nt - 1 - i, j))
    fwd4 = pl.BlockSpec((1, 1, tb, width), lambda b, j, i: (0, b, i, j))
    bwd4 = pl.BlockSpec((1, 1, tb, width), lambda b, j, i: (1, b, nt - 1 - i, j))
    yf, yb = pl.pallas_call(
        functools.partial(_rwkv_body, tb=tb),
        out_shape=[jax.ShapeDtypeStruct((1, B, T, C), F32)] * 2,
        grid=(B, C // width, nt),
        in_specs=[fwd4] * 5 + [fwd3] + [bwd4] * 5 + [bwd3]
        + [pl.BlockSpec((RWKV_SUB + 1, LANES, LANES), lambda b, j, i: (0, 0, 0))],
        out_specs=[pl.BlockSpec((1, 1, tb, width), lambda b, j, i: (0, b, i, j)),
                   pl.BlockSpec((1, 1, tb, width), lambda b, j, i: (0, b, nt - 1 - i, j))],
        scratch_shapes=[pltpu.VMEM((2, RWKV_PAIRS, HEAD_DIM, LANES), F32),
                        pltpu.VMEM((2, 2, RWKV_PAIRS, 3 * HEAD_DIM, LANES), BF16),
                        pltpu.VMEM((2, 2, RWKV_PAIRS, HEAD_DIM, LANES), F32)],
        compiler_params=pltpu.CompilerParams(dimension_semantics=("parallel", "parallel", "arbitrary")),
        name="rwkv7_scan",
        interpret=interpret,
    )(a, b, k, r, g, v, a, b, k, r, g, v, _rwkv_selectors())
    return jnp.concatenate([yf, yb], axis=0)


HALO = 8


def _head_sums(x, bd_ref):
    bd = bd_ref[...]
    return sum(jnp.dot(piece.astype(BF16), bd, preferred_element_type=F32) for piece in _split3(x))


def _token_shift(cur, prev_row, next_row, mu):
    tb = cur.shape[0]
    rid = lax.broadcasted_iota(jnp.int32, cur.shape, 0)
    prev = jnp.where(rid == 0, prev_row, pltpu.roll(cur, 1, 0))
    nxt = jnp.where(rid == tb - 1, next_row, pltpu.roll(cur, tb - 1, 0))
    return cur + mu[0:1] * (prev - cur) + mu[1:2] * (nxt - cur)


def _rwkv_prep_body(x_ref, xp_ref, xn_ref, r_ref, rp_ref, rn_ref, k_ref, kp_ref, kn_ref, v_ref, vp_ref, vn_ref,
                    mux_ref, mur_ref, w1_ref, w2_ref, w0_ref, a1_ref, a2_ref, a0_ref, g1_ref, g2_ref,
                    kk_w_ref, ka_w_ref, bd_ref, tri_ref,
                    ro_ref, vo_ref, go_ref, kd_ref, sa_ref, sb_ref, sk_ref, sr_ref, sg_ref):
    i = pl.program_id(1)
    first = i == 0
    last = i == pl.num_programs(1) - 1

    def shifted(cur_ref, p_ref, n_ref, mu):
        prev_row = jnp.where(first, 0.0, p_ref[0, HALO - 1:HALO, :])
        next_row = jnp.where(last, 0.0, n_ref[0, 0:1, :])
        return _token_shift(cur_ref[0], prev_row, next_row, mu)

    r = shifted(r_ref, rp_ref, rn_ref, mur_ref[0])
    k = shifted(k_ref, kp_ref, kn_ref, mur_ref[1])
    v = shifted(v_ref, vp_ref, vn_ref, mur_ref[2])
    xw = shifted(x_ref, xp_ref, xn_ref, mux_ref[0]).astype(BF16)
    xa = shifted(x_ref, xp_ref, xn_ref, mux_ref[1]).astype(BF16)
    xg = shifted(x_ref, xp_ref, xn_ref, mux_ref[2]).astype(BF16)
    hg = jax.nn.sigmoid(jnp.dot(xg, g1_ref[...], preferred_element_type=F32)).astype(BF16)
    go_ref[0] = jnp.dot(hg, g2_ref[...], preferred_element_type=F32)
    kk = k * kk_w_ref[...]
    kk = kk / jnp.maximum(jnp.sqrt(_head_sums(kk * kk, bd_ref)), 1e-12)
    ro_ref[0] = r
    vo_ref[0] = v
    for z in range(2):
        hw = jnp.tanh(jnp.dot(xw, w1_ref[z], preferred_element_type=F32)).astype(BF16)
        ha = jnp.dot(xa, a1_ref[z], preferred_element_type=F32).astype(BF16)
        w_raw = w0_ref[z:z + 1] + jnp.dot(hw, w2_ref[z], preferred_element_type=F32)
        lw = -math.exp(-0.5) * jax.nn.sigmoid(w_raw)
        a = jax.nn.sigmoid(a0_ref[z:z + 1]
                           + jnp.dot(ha, a2_ref[z], preferred_element_type=F32))
        kd = k * (1.0 + (a - 1.0) * ka_w_ref[...])
        kd_ref[z, 0] = kd
        cum = sum(jnp.dot(tri_ref[z], piece.astype(BF16), preferred_element_type=F32) for piece in _split3(lw))
        gamma = jnp.exp(cum)
        inv_gamma = jnp.exp(-cum)
        sa_ref[z, 0] = -kk * jnp.exp(cum - lw)
        sb_ref[z, 0] = kk * a * inv_gamma
        sk_ref[z, 0] = kd * inv_gamma
        sr_ref[z, 0] = r * gamma
        sg_ref[z, 0] = gamma


def _rwkv_post_body(ys_ref, r_ref, v_ref, kd_ref, g_ref, lng_ref, lnb_ref, rk_ref, bd_ref, o_ref):
    y = ys_ref[0, 0] + ys_ref[1, 0]
    inv_n = 1.0 / HEAD_DIM
    yc = y - _head_sums(y, bd_ref) * inv_n
    yn = yc * lax.rsqrt(_head_sums(yc * yc, bd_ref) * inv_n + GN_EPS) * lng_ref[...] + lnb_ref[...]
    r, v = r_ref[0], v_ref[0]
    bonus = (_head_sums(r * kd_ref[0, 0] * rk_ref[...], bd_ref)
             + _head_sums(r * kd_ref[1, 0] * rk_ref[...], bd_ref)) * v
    o_ref[0] = ((yn + bonus) * g_ref[0]).astype(o_ref.dtype)


def _rwkv7_mixer(u, proj, p, *, tb=256):
    B, T, D = u.shape
    C = D_RWKV
    tb = min(tb, T)
    nt = T // tb
    assert T % tb == 0 and tb % HALO == 0
    hb = tb // HALO

    def cur(width, col):
        return pl.BlockSpec((1, tb, width), lambda b, i: (b, i, col))

    def prev(width, col):
        return pl.BlockSpec((1, HALO, width), lambda b, i: (b, jnp.maximum(i * hb - 1, 0), col))

    def nxt(width, col):
        return pl.BlockSpec((1, HALO, width), lambda b, i: (b, jnp.minimum((i + 1) * hb, T // HALO - 1), col))

    def full(a):
        return pl.BlockSpec(a.shape, lambda b, i: (0,) * a.ndim)

    lane = np.arange(C) // HEAD_DIM
    bd = jnp.asarray(lane[:, None] == lane[None, :], BF16)
    t_id = np.arange(tb)
    same = (t_id[:, None] // RWKV_SUB) == (t_id[None, :] // RWKV_SUB)
    tri = jnp.asarray(np.stack([same & (t_id[None, :] <= t_id[:, None]),
                                same & (t_id[None, :] >= t_id[:, None])]), BF16)
    weights = [p['rwkv_mu_x'], p['rwkv_mu_rkv'], p['rwkv_w1'].astype(BF16), p['rwkv_w2'].astype(BF16), p['rwkv_w0'],
               p['rwkv_a1'].astype(BF16), p['rwkv_a2'].astype(BF16), p['rwkv_a0'], p['rwkv_g1'].astype(BF16), p['rwkv_g2'].astype(BF16),
               p['rwkv_k_k'][None], p['rwkv_k_a'][None], bd, tri]
    one = jax.ShapeDtypeStruct((B, T, C), F32)
    two = jax.ShapeDtypeStruct((2, B, T, C), F32)
    out1 = pl.BlockSpec((1, tb, C), lambda b, i: (b, i, 0))
    out2 = pl.BlockSpec((2, 1, tb, C), lambda b, i: (0, b, i, 0))
    r, v, g, kd, sa, sb, sk, sr, sg = pl.pallas_call(
        _rwkv_prep_body,
        out_shape=[one, one, one, two, two, two, two, two, two],
        grid=(B, nt),
        in_specs=[cur(D, 0), prev(D, 0), nxt(D, 0)]
        + [spec(C, col) for col in range(3) for spec in (cur, prev, nxt)]
        + [full(w) for w in weights],
        out_specs=[out1, out1, out1, out2, out2, out2, out2, out2, out2],
        compiler_params=pltpu.CompilerParams(dimension_semantics=("parallel", "parallel")),
        name="rwkv7_prepare",
    )(u, u, u, *([proj] * 9), *weights)
    ys = _rwkv7_scan(sa, sb, sk, sr, sg, v)
    post_w = [p['rwkv_lnx_g'][None], p['rwkv_lnx_b'][None], p['rwkv_r_k'].reshape(1, C), bd]
    return pl.pallas_call(
        _rwkv_post_body,
        out_shape=jax.ShapeDtypeStruct((B, T, C), BF16),
        grid=(B, nt),
        in_specs=[out2, out1, out1, out2, out1] + [full(w) for w in post_w],
        out_specs=out1,
        compiler_params=pltpu.CompilerParams(dimension_semantics=("parallel", "parallel")),
        name="rwkv7_output",
    )(ys, r, v, kd, g, *post_w)


NA_ROWS_PER_STEP = 8


def _na_body(q_ref, k_ref, v_ref, bias_ref, mask_ref, o_ref, *, rows):
    kr = NA_WIN_ROWS
    win = kr * GRID_W
    valid = mask_ref[...] != 0
    lane = lax.broadcasted_iota(jnp.int32, (GRID_W, LANES), 1)
    for ii in range(NA_ROWS_PER_STEP):
        i = pl.program_id(1) * NA_ROWS_PER_STEP + ii
        start = jnp.clip(i - kr // 2, 0, rows - kr)
        d = i - start
        krows = pl.ds(pl.multiple_of(start * GRID_W, GRID_W), win)
        for pair in range(D_NA // LANES):
            lanes = slice(pair * LANES, (pair + 1) * LANES)
            q2 = q_ref[0, ii * GRID_W:(ii + 1) * GRID_W, lanes] * (HEAD_DIM ** -0.5)
            k2 = k_ref[0, krows, lanes]
            v2 = v_ref[0, krows, lanes]
            outs = []
            for hh in range(LANES // HEAD_DIM):
                in_head = (lane >= hh * HEAD_DIM) & (lane < (hh + 1) * HEAD_DIM)
                qh = jnp.where(in_head, q2, 0.0).astype(BF16)
                s = lax.dot_general(qh, k2, (((1,), (1,)), ((), ())), preferred_element_type=F32)
                s = s + bias_ref[d, pair * (LANES // HEAD_DIM) + hh]
                s = jnp.where(valid, s, -1e30)
                m = jnp.max(s, axis=-1, keepdims=True)
                e = jnp.exp(s - m)
                p = e / jnp.sum(e, axis=-1, keepdims=True)
                outs.append(jnp.dot(p.astype(BF16), v2, preferred_element_type=F32))
            o_ref[0, ii * GRID_W:(ii + 1) * GRID_W, lanes] = jnp.where(lane < HEAD_DIM, outs[0], outs[1])


def _neighbourhood_attention_pallas(q, k, v, rpb, *, interpret=False):
    B, T, _ = q.shape
    rows = T // GRID_W
    kr, kc = NA_WIN_ROWS, NA_WIN_COLS
    assert rows >= kr and rows % NA_ROWS_PER_STEP == 0
    ci = jnp.arange(GRID_W)
    col_start = jnp.clip(ci - kc // 2, 0, GRID_W - kc)
    col_valid = (ci[None] >= col_start[:, None]) & (ci[None] < col_start[:, None] + kc)
    mask = jnp.tile(col_valid.astype(jnp.int32), (1, kr))
    dc_idx = jnp.clip(ci[None] - ci[:, None] + kc - 1, 0, 2 * kc - 2)
    dr_idx = jnp.arange(kr)[None, :] - jnp.arange(kr)[:, None] + kr - 1
    onehot = (jnp.arange(2 * kc - 1)[:, None] == dc_idx.reshape(1, -1)).astype(F32)
    bias = jnp.dot(rpb[:, dr_idx].reshape(-1, 2 * kc - 1), onehot, precision=lax.Precision.HIGHEST)
    bias = bias.reshape(NA_HEADS, kr, kr, GRID_W, GRID_W)
    bias = jnp.transpose(bias, (1, 0, 3, 2, 4)).reshape(kr, NA_HEADS, GRID_W, kr * GRID_W)
    tq = NA_ROWS_PER_STEP * GRID_W
    return pl.pallas_call(
        functools.partial(_na_body, rows=rows),
        out_shape=jax.ShapeDtypeStruct((B, T, D_NA), F32),
        grid=(B, rows // NA_ROWS_PER_STEP),
        in_specs=[pl.BlockSpec((1, tq, D_NA), lambda b, i: (b, i, 0)),
                  pl.BlockSpec((1, T, D_NA), lambda b, i: (b, 0, 0)),
                  pl.BlockSpec((1, T, D_NA), lambda b, i: (b, 0, 0)),
                  pl.BlockSpec((kr, NA_HEADS, GRID_W, kr * GRID_W), lambda b, i: (0, 0, 0, 0)),
                  pl.BlockSpec((GRID_W, kr * GRID_W), lambda b, i: (0, 0))],
        out_specs=pl.BlockSpec((1, tq, D_NA), lambda b, i: (b, i, 0)),
        compiler_params=pltpu.CompilerParams(dimension_semantics=("parallel", "arbitrary")),
        name="neighbourhood_attention",
        interpret=interpret,
    )(q, k.astype(BF16), v.astype(BF16), bias, mask)


def _hyena_filter_spectra(L, p):
    t = jnp.linspace(0.0, 1.0, L, dtype=F32)[:, None]
    n_bands = (HY_EMB - 1) // 2
    omega = 2.0 * math.pi * jnp.arange(L, dtype=F32)[:, None] / L
    bands = jnp.linspace(1e-4, n_bands - 1, n_bands, dtype=F32)[None]
    z = jnp.concatenate([t, jnp.cos(bands * omega), -jnp.sin(bands * omega)], -1)
    freq = p['hy_freq']
    h = jnp.sin(freq * (z @ p['hy_w1'] + p['hy_b1']))
    h = jnp.sin(freq * (h @ p['hy_w2'] + p['hy_b2']))
    h = jnp.sin(freq * (h @ p['hy_w3'] + p['hy_b3']))
    h = (h @ p['hy_w4']).reshape(L, HY_ORDER, 2, D_HY)
    deltas = jnp.abs(jnp.linspace(math.log(HY_TOL) / HY_FAST_PCT, math.log(HY_TOL) / HY_SLOW_PCT, D_HY, dtype=F32))
    h = h * jnp.exp(-t * deltas)[:, None, None, :]
    h_f, h_b = h[:, :, 0], h[:, :, 1]
    kern = jnp.concatenate([h_f, jnp.zeros((1, HY_ORDER, D_HY), F32), h_b[1:][::-1]], 0)
    kern = kern / jnp.sum(jnp.abs(kern), 0, keepdims=True)
    return jnp.fft.fft(kern, axis=0)


def _dft_tables(R):
    n = jnp.arange(R, dtype=jnp.int32)
    ang = (2.0 * math.pi / R) * ((n[:, None] * n[None, :]) % R).astype(F32)
    c, s = jnp.cos(ang), jnp.sin(ang)
    first = jnp.concatenate([c, -s], axis=0)[:, :R // 2]
    last = jnp.concatenate([c, -s], axis=1)[:R // 2]
    m = (R * n[None, :, None] * n[None, None, :] + n[:, None, None] * n[None, None, :]) % (R * R)
    phi = (2.0 * math.pi / (R * R)) * m.astype(F32)
    cp, sp = jnp.cos(phi), jnp.sin(phi)
    fwd = jnp.concatenate([jnp.concatenate([cp, sp], axis=2), jnp.concatenate([-sp, cp], axis=2)], axis=1)
    cpt, spt = jnp.swapaxes(cp, 1, 2), jnp.swapaxes(sp, 1, 2)
    inv = jnp.concatenate([jnp.concatenate([cpt, -spt], axis=2), jnp.concatenate([spt, cpt], axis=2)], axis=1)
    return first.astype(BF16), fwd.astype(BF16), inv.astype(BF16), last.astype(BF16)


def _conv_first_body(f_ref, z_ref, o_ref):
    o_ref[0] = jnp.dot(f_ref[...], z_ref[0].astype(BF16), preferred_element_type=F32).astype(o_ref.dtype)


def _conv_mid_body(a_ref, mf_ref, mi_ref, h_ref, o_ref, *, R, kb):
    for j in range(kb):
        a = jnp.concatenate([a_ref[0, 0, j], a_ref[0, 1, j]], axis=0)
        x = jnp.dot(mf_ref[j], a, preferred_element_type=F32)
        xr, xi = x[:R], x[R:]
        hr, hi = h_ref[0, j], h_ref[1, j]
        y = jnp.concatenate([xr * hr - xi * hi, xr * hi + xi * hr], axis=0).astype(BF16)
        b = jnp.dot(mi_ref[j], y, preferred_element_type=F32)
        o_ref[0, 0, j] = b[:R].astype(o_ref.dtype)
        o_ref[0, 1, j] = b[R:].astype(o_ref.dtype)


def _conv_last_body(f_ref, b_ref, z_ref, gate_ref, bias_ref, o_ref):
    y = jnp.dot(f_ref[...], b_ref[0], preferred_element_type=F32)
    o_ref[0] = gate_ref[0] * (y + z_ref[0] * bias_ref[...])


def _gated_long_conv(z, gate, hm, bias, tables, *, tn=4096, kb=4, interpret=False):
    B, L, C = z.shape
    R = math.isqrt(2 * L)
    assert R * R == 2 * L and R % (2 * kb) == 0
    first, fwd, inv, last = tables
    tn = min(tn, R * C)
    nt = (R * C) // tn
    z2 = z.reshape(B, R // 2, R * C)
    a = pl.pallas_call(
        _conv_first_body,
        out_shape=jax.ShapeDtypeStruct((B, 2 * R, R * C), BF16),
        grid=(B, nt),
        in_specs=[pl.BlockSpec((2 * R, R // 2), lambda b, j: (0, 0)),
                  pl.BlockSpec((1, R // 2, tn), lambda b, j: (b, 0, j))],
        out_specs=pl.BlockSpec((1, 2 * R, tn), lambda b, j: (b, 0, j)),
        name="long_conv_first",
        interpret=interpret,
    )(first, z2)
    bmid = pl.pallas_call(
        functools.partial(_conv_mid_body, R=R, kb=kb),
        out_shape=jax.ShapeDtypeStruct((B, 2, R, R, C), BF16),
        grid=(B, R // kb),
        in_specs=[pl.BlockSpec((1, 2, kb, R, C), lambda b, k: (b, 0, k, 0, 0)),
                  pl.BlockSpec((kb, 2 * R, 2 * R), lambda b, k: (k, 0, 0)),
                  pl.BlockSpec((kb, 2 * R, 2 * R), lambda b, k: (k, 0, 0)),
                  pl.BlockSpec((2, kb, R, C), lambda b, k: (0, k, 0, 0))],
        out_specs=pl.BlockSpec((1, 2, kb, R, C), lambda b, k: (b, 0, k, 0, 0)),
        name="long_conv_mid",
        interpret=interpret,
    )(a.reshape(B, 2, R, R, C), fwd, inv, hm)
    out = pl.pallas_call(
        _conv_last_body,
        out_shape=jax.ShapeDtypeStruct((B, R // 2, R * C), F32),
        grid=(B, nt),
        in_specs=[pl.BlockSpec((R // 2, 2 * R), lambda b, j: (0, 0)),
                  pl.BlockSpec((1, 2 * R, tn), lambda b, j: (b, 0, j)),
                  pl.BlockSpec((1, R // 2, tn), lambda b, j: (b, 0, j)),
                  pl.BlockSpec((1, R // 2, tn), lambda b, j: (b, 0, j)),
                  pl.BlockSpec((1, tn), lambda b, j: (0, j))],
        out_specs=pl.BlockSpec((1, R // 2, tn), lambda b, j: (b, 0, j)),
        name="long_conv_last",
        interpret=interpret,
    )(last, bmid.reshape(B, 2 * R, R * C), z2, gate.reshape(B, R // 2, R * C), jnp.tile(bias, R)[None])
    return out.reshape(B, L, C)


def _short_conv_body(u_ref, up_ref, un_ref, sw_ref, sb_ref, x1_ref, x2_ref, v_ref):
    i = pl.program_id(1)
    cur = u_ref[0]
    tb = cur.shape[0]
    prev_row = jnp.where(i == 0, 0.0, up_ref[0, HALO - 1:HALO, :])
    next_row = jnp.where(i == pl.num_programs(1) - 1, 0.0, un_ref[0, 0:1, :])
    rid = lax.broadcasted_iota(jnp.int32, cur.shape, 0)
    prev = jnp.where(rid == 0, prev_row, pltpu.roll(cur, 1, 0))
    nxt = jnp.where(rid == tb - 1, next_row, pltpu.roll(cur, tb - 1, 0))
    u = prev * sw_ref[0:1] + cur * sw_ref[1:2] + nxt * sw_ref[2:3] + sb_ref[...]
    x1_ref[0] = u[:, :D_HY]
    x2_ref[0] = u[:, D_HY:2 * D_HY]
    v_ref[0] = u[:, 2 * D_HY:]


def _hyena_short_conv(proj, sw, sb, *, tb=512):
    B, T, n_in = proj.shape
    width = 3 * D_HY
    col = (3 * D_RWKV + 3 * D_NA) // width
    assert col * width == 3 * D_RWKV + 3 * D_NA
    tb = min(tb, T)
    hb = tb // HALO
    out = jax.ShapeDtypeStruct((B, T, D_HY), F32)
    ospec = pl.BlockSpec((1, tb, D_HY), lambda b, i: (b, i, 0))
    return pl.pallas_call(
        _short_conv_body,
        out_shape=[out, out, out],
        grid=(B, T // tb),
        in_specs=[pl.BlockSpec((1, tb, width), lambda b, i: (b, i, col)),
                  pl.BlockSpec((1, HALO, width), lambda b, i: (b, jnp.maximum(i * hb - 1, 0), col)),
                  pl.BlockSpec((1, HALO, width), lambda b, i: (b, jnp.minimum((i + 1) * hb, T // HALO - 1), col)),
                  pl.BlockSpec(sw.shape, lambda b, i: (0, 0)),
                  pl.BlockSpec((1, width), lambda b, i: (0, 0))],
        out_specs=[ospec, ospec, ospec],
        compiler_params=pltpu.CompilerParams(dimension_semantics=("parallel", "parallel")),
        name="hyena_short_conv",
    )(proj, proj, proj, sw, sb[None])


def _hyena_mixer(proj, p):
    T = proj.shape[1]
    R = math.isqrt(2 * T)
    x1, x2, v = _hyena_short_conv(proj, p['hy_short_w'], p['hy_short_b'])
    spec = _hyena_filter_spectra(T, p) * (1.0 / (2 * T))
    spec = jnp.swapaxes(spec.reshape(R, R, HY_ORDER, D_HY), 0, 1)
    tables = _dft_tables(R)
    z = v
    for o, gate in enumerate((x1, x2)):
        hm = jnp.stack([jnp.real(spec[:, :, o]), jnp.imag(spec[:, :, o])])
        z = _gated_long_conv(z, gate, hm, p['hy_bias'][o], tables)
    return z


def _route(aff):
    n = aff.shape[0]
    cap = (CAPACITY_FACTOR * n) // N_EXPERTS
    return lax.top_k(aff.T, cap)


def _mix_body(x_ref, ya_ref, yb_ref, yc_ref, g_ref, wa_ref, wb_ref, wc_ref, wo_ref, lg_ref, lb_ref, wr_ref,
              o_ref, ob_ref, aff_ref):
    g = jax.nn.sigmoid(g_ref[...])
    m = (g[:, :D_MODEL] * jnp.dot(ya_ref[...].astype(BF16), wa_ref[...], preferred_element_type=F32)
         + g[:, D_MODEL:2 * D_MODEL] * jnp.dot(yb_ref[...].astype(BF16), wb_ref[...], preferred_element_type=F32)
         + g[:, 2 * D_MODEL:] * jnp.dot(yc_ref[...].astype(BF16), wc_ref[...], preferred_element_type=F32))
    h = ALPHA * x_ref[...] + jnp.dot(m.astype(BF16), wo_ref[...], preferred_element_type=F32)
    y = _layer_norm(h, lg_ref[...], lb_ref[...])
    o_ref[...] = y
    ob_ref[...] = y.astype(BF16)
    logits = jnp.dot(y, wr_ref[...], precision=lax.Precision.HIGHEST, preferred_element_type=F32)
    e = jnp.exp(logits - jnp.max(logits, -1, keepdims=True))
    aff_ref[...] = e / jnp.sum(e, -1, keepdims=True)


def _mix_and_norm(x, y_a, y_b, y_c, proj, p, *, tm=512):
    n, d = x.shape
    assert n % tm == 0 and proj.shape[1] == 2 * N_BRANCH * d
    row = lambda w: pl.BlockSpec((tm, w), lambda i: (i, 0))
    full = lambda a: pl.BlockSpec(a.shape, lambda i: (0,) * a.ndim)
    wa, wb, wc, wo = (p[k].astype(BF16) for k in ('w_branch_a', 'w_branch_b', 'w_branch_c', 'w_out'))
    lg, lb = p['ln1_g'][None], p['ln1_b'][None]
    return pl.pallas_call(
        _mix_body,
        out_shape=[jax.ShapeDtypeStruct((n, d), F32), jax.ShapeDtypeStruct((n, d), BF16),
                   jax.ShapeDtypeStruct((n, N_EXPERTS), F32)],
        grid=(n // tm,),
        in_specs=[row(d), row(D_RWKV), row(D_NA), row(D_HY),
                  pl.BlockSpec((tm, N_BRANCH * d), lambda i: (i, 1)),
                  full(wa), full(wb), full(wc), full(wo), full(lg), full(lb), full(p['w_router'])],
        out_specs=[row(d), row(d), row(N_EXPERTS)],
        compiler_params=pltpu.CompilerParams(dimension_semantics=("parallel",)),
        name="mix_norm_route",
    )(x, y_a, y_b, y_c, proj, wa, wb, wc, wo, lg, lb, p['w_router'])


def _mixers(x, p):
    B, T, D = x.shape
    n = B * T
    proj2 = _matmul(x.reshape(n, D), p['w_in'])
    proj = proj2.reshape(B, T, -1)
    qb, kb, vb = (proj[..., 3 * D_RWKV + j * D_NA:3 * D_RWKV + (j + 1) * D_NA] for j in range(3))
    y_a = _rwkv7_mixer(x, proj, p)
    y_b = _neighbourhood_attention_pallas(qb, kb, vb, p['na_rpb'])
    y_c = _hyena_mixer(proj, p)
    return _mix_and_norm(x.reshape(n, D), y_a.reshape(n, -1), y_b.reshape(n, -1), y_c.reshape(n, -1), proj2, p)


def _encoder_layer(xs, p):
    mixed = [_mixers(x, p) for x in xs]
    routes = [_route(aff) for _, _, aff in mixed]
    xe = jnp.concatenate([jnp.take(xb, idx, axis=0) for (_, xb, _), (_, idx) in zip(mixed, routes)], axis=1)
    gate = jnp.concatenate([g for g, _ in routes], axis=1)[..., None]
    ye = _expert_ffn(xe, gate, p['w_exp_gate'], p['w_exp_up'], p['w_exp_down'])
    outs, off = [], 0
    for x, (x1, _, _), (_, idx) in zip(xs, mixed, routes):
        cap = idx.shape[1]
        ffn = jnp.zeros_like(x1).at[idx.reshape(-1)].add(ye[:, off:off + cap].reshape(-1, x1.shape[-1]))
        off += cap
        outs.append(_ln(x1, p['ln2_g'], p['ln2_b'], residual=ffn).reshape(x.shape))
    return outs


def kernel(x_prompt, x_sample, ln_in_g, ln_in_b, w_in, rwkv_mu_rkv, rwkv_mu_x, rwkv_w0, rwkv_w1, rwkv_w2,
           rwkv_a0, rwkv_a1, rwkv_a2, rwkv_g1, rwkv_g2, rwkv_k_k, rwkv_k_a, rwkv_r_k, rwkv_lnx_g, rwkv_lnx_b,
           na_rpb, hy_short_w, hy_short_b, hy_w1, hy_b1, hy_w2, hy_b2, hy_w3, hy_b3, hy_w4, hy_freq, hy_bias,
           w_branch_a, w_branch_b, w_branch_c, w_out, ln1_g, ln1_b, w_router, w_exp_gate, w_exp_up,
           w_exp_down, ln2_g, ln2_b):
    stacked = {
        'w_in': w_in, 'rwkv_mu_rkv': rwkv_mu_rkv, 'rwkv_mu_x': rwkv_mu_x, 'rwkv_w0': rwkv_w0,
        'rwkv_w1': rwkv_w1, 'rwkv_w2': rwkv_w2, 'rwkv_a0': rwkv_a0, 'rwkv_a1': rwkv_a1, 'rwkv_a2': rwkv_a2,
        'rwkv_g1': rwkv_g1, 'rwkv_g2': rwkv_g2, 'rwkv_k_k': rwkv_k_k, 'rwkv_k_a': rwkv_k_a,
        'rwkv_r_k': rwkv_r_k, 'rwkv_lnx_g': rwkv_lnx_g, 'rwkv_lnx_b': rwkv_lnx_b, 'na_rpb': na_rpb,
        'hy_short_w': hy_short_w, 'hy_short_b': hy_short_b, 'hy_w1': hy_w1, 'hy_b1': hy_b1,
        'hy_w2': hy_w2, 'hy_b2': hy_b2, 'hy_w3': hy_w3, 'hy_b3': hy_b3, 'hy_w4': hy_w4,
        'hy_freq': hy_freq, 'hy_bias': hy_bias, 'w_branch_a': w_branch_a, 'w_branch_b': w_branch_b,
        'w_branch_c': w_branch_c, 'w_out': w_out, 'ln1_g': ln1_g, 'ln1_b': ln1_b, 'w_router': w_router,
        'w_exp_gate': w_exp_gate, 'w_exp_up': w_exp_up, 'w_exp_down': w_exp_down,
        'ln2_g': ln2_g, 'ln2_b': ln2_b,
    }
    xs = [_ln(x.reshape(-1, D_MODEL), ln_in_g, ln_in_b).reshape(x.shape) for x in (x_prompt, x_sample)]
    for l in range(DEPTH):
        xs = _encoder_layer(xs, {name: arr[l] for name, arr in stacked.items()})
    return tuple(xs)
```

```python
import functools
import math

import jax
import jax.numpy as jnp
import numpy as np
from jax import lax
from jax.experimental import pallas as pl
from jax.experimental.pallas import tpu as pltpu

D_MODEL = 1024
DEPTH = 2
GRID_W = 64
HEAD_DIM = 64
D_RWKV = D_MODEL // 2
RWKV_HEADS = D_RWKV // HEAD_DIM
D_NA = D_MODEL // 4
NA_HEADS = D_NA // HEAD_DIM
D_HY = D_MODEL // 4
N_BRANCH = 3
GN_EPS = 64e-5
NA_WIN_ROWS = 8
NA_WIN_COLS = 16
HY_ORDER = 2
HY_EMB = 33
HY_TOL = 1e-2
HY_FAST_PCT = 0.3
HY_SLOW_PCT = 1.5
N_EXPERTS = 16
CAPACITY_FACTOR = 2
ALPHA = (2 * DEPTH) ** 0.25
LN_EPS = 1e-5

F32 = jnp.float32
BF16 = jnp.bfloat16
EXPERT_FFN_VMEM_BYTES = 52 * 1024 * 1024


def _mm_body(x_ref, w_ref, o_ref):
    o_ref[...] = jnp.dot(x_ref[...].astype(BF16), w_ref[...].astype(BF16),
                         preferred_element_type=F32).astype(o_ref.dtype)


def _matmul(x, w, *, tm=1024, tn=512, out_dtype=F32):
    m, k = x.shape
    _, n = w.shape
    tm = min(tm, m)
    tn = min(tn, n)
    assert m % tm == 0 and n % tn == 0
    return pl.pallas_call(
        _mm_body,
        out_shape=jax.ShapeDtypeStruct((m, n), out_dtype),
        grid=(m // tm, n // tn),
        in_specs=[pl.BlockSpec((tm, k), lambda i, j: (i, 0)),
                  pl.BlockSpec((k, tn), lambda i, j: (0, j))],
        out_specs=pl.BlockSpec((tm, tn), lambda i, j: (i, j)),
        name="dense_matmul",
    )(x, w)


def _expert_ffn_body(x_ref, wg_ref, wu_ref, wd_ref, gate_ref, o_ref, acc_ref):
    f = pl.program_id(2)

    @pl.when(f == 0)
    def _():
        acc_ref[...] = jnp.zeros_like(acc_ref)

    xb = x_ref[0]
    hg = jnp.dot(xb, wg_ref[0].astype(BF16), preferred_element_type=F32)
    hu = jnp.dot(xb, wu_ref[0].astype(BF16), preferred_element_type=F32)
    h = (hg * jax.nn.sigmoid(hg) * hu).astype(BF16)
    acc_ref[...] += jnp.dot(h, wd_ref[0].astype(BF16), preferred_element_type=F32)

    @pl.when(f == pl.num_programs(2) - 1)
    def _():
        o_ref[0] = acc_ref[...] * gate_ref[0]


def _expert_ffn(xe, gate, w_gate, w_up, w_down, *, tm=2048, tf=256):
    e, m, d = xe.shape
    ff = w_gate.shape[-1]
    tm = min(tm, m)
    assert m % tm == 0 and ff % tf == 0
    return pl.pallas_call(
        _expert_ffn_body,
        out_shape=jax.ShapeDtypeStruct((e, m, d), F32),
        grid=(e, m // tm, ff // tf),
        in_specs=[pl.BlockSpec((1, tm, d), lambda g, i, f: (g, i, 0)),
                  pl.BlockSpec((1, d, tf), lambda g, i, f: (g, 0, f)),
                  pl.BlockSpec((1, d, tf), lambda g, i, f: (g, 0, f)),
                  pl.BlockSpec((1, tf, d), lambda g, i, f: (g, f, 0)),
                  pl.BlockSpec((1, tm, 1), lambda g, i, f: (g, i, 0))],
        out_specs=pl.BlockSpec((1, tm, d), lambda g, i, f: (g, i, 0)),
        scratch_shapes=[pltpu.VMEM((tm, d), F32)],
        compiler_params=pltpu.CompilerParams(
            dimension_semantics=("parallel", "parallel", "arbitrary"), vmem_limit_bytes=EXPERT_FFN_VMEM_BYTES),
        name="expert_ffn",
    )(xe, w_gate, w_up, w_down, gate)


def _layer_norm(h, g, b):
    hc = h - jnp.mean(h, -1, keepdims=True)
    var = jnp.mean(hc * hc, -1, keepdims=True)
    return hc * lax.rsqrt(var + LN_EPS) * g + b


def _ln_body(x_ref, g_ref, b_ref, o_ref):
    o_ref[...] = _layer_norm(x_ref[...], g_ref[...], b_ref[...])


def _ln_residual_body(x_ref, r_ref, g_ref, b_ref, o_ref):
    o_ref[...] = _layer_norm(ALPHA * x_ref[...] + r_ref[...], g_ref[...], b_ref[...])


def _ln(x, g, b, residual=None, *, tm=1024):
    n, d = x.shape
    tm = min(tm, n)
    assert n % tm == 0
    row = pl.BlockSpec((tm, d), lambda i: (i, 0))
    vec = pl.BlockSpec((1, d), lambda i: (0, 0))
    args = (x,) if residual is None else (x, residual)
    return pl.pallas_call(
        _ln_body if residual is None else _ln_residual_body,
        out_shape=jax.ShapeDtypeStruct((n, d), F32),
        grid=(n // tm,),
        in_specs=[row] * len(args) + [vec, vec],
        out_specs=row,
        compiler_params=pltpu.CompilerParams(dimension_semantics=("parallel",)),
        name="layer_norm",
    )(*args, g[None], b[None])


RWKV_SUB = 16
RWKV_PAIRS = 2
LANES = 128
RWKV_PIECES = 3
RWKV_GAMMA_SLOT = 2 * RWKV_PIECES * RWKV_SUB


def _rwkv_selectors():
    z = np.zeros((RWKV_SUB + 1, LANES, LANES), np.float32)
    for h in range(LANES // HEAD_DIM):
        lanes = slice(h * HEAD_DIM, (h + 1) * HEAD_DIM)
        for p in range(RWKV_PIECES):
            for t in range(RWKV_SUB):
                z[t, (h * RWKV_PIECES + p) * RWKV_SUB + t, lanes] = 1.0
            z[RWKV_SUB, RWKV_GAMMA_SLOT + 8 * h + p, lanes] = 1.0
    return jnp.asarray(z, BF16)


def _split3(x):
    hi = x.astype(BF16).astype(F32)
    rem = x - hi
    mid = rem.astype(BF16).astype(F32)
    lo = (rem - mid).astype(BF16).astype(F32)
    return [hi, mid, lo]


def _operand_tile(xa, xb, extra):
    pa = jnp.concatenate(_split3(xa), axis=0)
    pb = pa if xb is None else jnp.concatenate(_split3(xb), axis=0)
    lane_lo = lax.broadcasted_iota(jnp.int32, pa.shape, 1) < HEAD_DIM
    head0 = jnp.where(lane_lo, pa, pltpu.roll(pb, HEAD_DIM, 1))
    head1 = jnp.where(lane_lo, pltpu.roll(pa, HEAD_DIM, 1), pb)
    tt = jnp.concatenate([head0, head1, extra], axis=0).T
    return tt[:HEAD_DIM].astype(BF16), tt[HEAD_DIM:].astype(BF16)


def _rwkv_prepare(refs, base, lanes, reverse):
    a_ref, b_ref, k_ref, r_ref, g_ref = refs
    rows = pl.ds(base, RWKV_SUB)
    last = 0 if reverse else RWKV_SUB - 1
    g_last = g_ref[0, 0, rows, lanes][last:last + 1]
    gam = jnp.concatenate(_split3(g_last) + [jnp.zeros((8 - RWKV_PIECES, LANES), F32)], axis=0)
    extra = jnp.concatenate([gam, pltpu.roll(gam, HEAD_DIM, 1), jnp.zeros((16, LANES), F32)], axis=0)
    g_a, g_b = _operand_tile(a_ref[0, 0, rows, lanes], b_ref[0, 0, rows, lanes], extra)
    g_k, _ = _operand_tile(k_ref[0, 0, rows, lanes], None, jnp.zeros((32, LANES), F32))
    rg = r_ref[0, 0, rows, lanes]
    pad = jnp.zeros((HEAD_DIM - RWKV_SUB, LANES), F32)
    r_t = jnp.concatenate([rg, pad, pltpu.roll(rg, HEAD_DIM, 1), pad], axis=0).T[:HEAD_DIM]
    return jnp.concatenate([g_a, g_b, g_k], axis=0), r_t


def _rwkv_steps(streams, z_ref, s_ref):
    head_lane = (lax.broadcasted_iota(jnp.int32, (HEAD_DIM, LANES), 1) // HEAD_DIM) * HEAD_DIM
    lanes = [slice(q * LANES, (q + 1) * LANES) for _, _, _, _, _, q, _, _ in streams]
    vs = [s[2][0, pl.ds(s[6], RWKV_SUB), ln] for s, ln in zip(streams, lanes)]
    sts = [s_ref[d, q] for _, _, _, _, d, q, _, _ in streams]
    ys = [[None] * RWKV_SUB for _ in streams]
    for p in range(RWKV_SUB // 2):
        cols = []
        for g_all, _, _, _, _, _, _, reverse in streams:
            ts = (RWKV_SUB - 1 - 2 * p, RWKV_SUB - 2 - 2 * p) if reverse else (2 * p, 2 * p + 1)
            sel = jnp.concatenate([z_ref[ts[0]], z_ref[ts[1]]], axis=1)
            cols.append((ts, jnp.dot(g_all, sel, preferred_element_type=F32)))
        for n in range(2):
            for i, (_, r_t, _, _, _, _, _, _) in enumerate(streams):
                ts, c = cols[i]
                t = ts[n]
                a_c, b_c, k_c = (c[o * HEAD_DIM:(o + 1) * HEAD_DIM, n * LANES:(n + 1) * LANES] for o in range(3))
                r_c = jnp.take_along_axis(r_t, head_lane + t, axis=1)
                sa = jnp.sum(sts[i] * a_c, axis=0, keepdims=True)
                sts[i] = sts[i] + b_c * sa + k_c * vs[i][t:t + 1, :]
                ys[i][t] = jnp.sum(sts[i] * r_c, axis=0, keepdims=True)
    for i, (g_all, _, _, y_ref, d, q, base, _) in enumerate(streams):
        s_ref[d, q] = sts[i] * jnp.dot(g_all[:HEAD_DIM], z_ref[RWKV_SUB], preferred_element_type=F32)
        y_ref[0, 0, pl.ds(base, RWKV_SUB), lanes[i]] = jnp.concatenate(ys[i], axis=0)


def _rwkv_body(af, bf, kf, rf, gf, vf, ab, bb, kb, rb, gb, vb, z_ref, yf_ref, yb_ref, s_ref, g_ref,
               rt_ref, *, tb):
    @pl.when(pl.program_id(2) == 0)
    def _():
        s_ref[...] = jnp.zeros_like(s_ref)

    nsub = tb // RWKV_SUB
    refs_f = (af, bf, kf, rf, gf)
    refs_b = (ab, bb, kb, rb, gb)

    def base_f(c):
        return pl.multiple_of(c * RWKV_SUB, RWKV_SUB)

    def base_b(c):
        return pl.multiple_of((nsub - 1 - c) * RWKV_SUB, RWKV_SUB)

    def prepare(slot, c):
        for q in range(RWKV_PAIRS):
            lanes = slice(q * LANES, (q + 1) * LANES)
            g_ref[slot, 0, q], rt_ref[slot, 0, q] = _rwkv_prepare(refs_f, base_f(c), lanes, False)
            g_ref[slot, 1, q], rt_ref[slot, 1, q] = _rwkv_prepare(refs_b, base_b(c), lanes, True)

    prepare(0, 0)

    def sub(c, carry):
        slot = c % 2
        nxt = jnp.minimum(c + 1, nsub - 1)
        streams = []
        for q in range(RWKV_PAIRS):
            streams.append((g_ref[slot, 0, q], rt_ref[slot, 0, q], vf, yf_ref, 0, q, base_f(c), False))
            streams.append((g_ref[slot, 1, q], rt_ref[slot, 1, q], vb, yb_ref, 1, q, base_b(c), True))
        prepare(1 - slot, nxt)
        _rwkv_steps(streams, z_ref, s_ref)
        return carry

    lax.fori_loop(0, nsub, sub, 0)


def _rwkv7_scan(a, b, k, r, g, v, *, tb=512, interpret=False):
    B, T, C = v.shape
    tb = min(tb, T)
    nt = T // tb
    width = RWKV_PAIRS * LANES
    assert T % tb == 0 and tb % RWKV_SUB == 0 and C % width == 0
    fwd3 = pl.BlockSpec((1, tb, width), lambda b, j, i: (b, i, j))
    bwd3 = pl.BlockSpec((1, tb, width), lambda b, j, i: (b, nt - 1 - i, j))
    fwd4 = pl.BlockSpec((1, 1, tb, width), lambda b, j, i: (0, b, i, j))
    bwd4 = pl.BlockSpec((1, 1, tb, width), lambda b, j, i: (1, b, nt - 1 - i, j))
    yf, yb = pl.pallas_call(
        functools.partial(_rwkv_body, tb=tb),
        out_shape=[jax.ShapeDtypeStruct((1, B, T, C), F32)] * 2,
        grid=(B, C // width, nt),
        in_specs=[fwd4] * 5 + [fwd3] + [bwd4] * 5 + [bwd3]
        + [pl.BlockSpec((RWKV_SUB + 1, LANES, LANES), lambda b, j, i: (0, 0, 0))],
        out_specs=[pl.BlockSpec((1, 1, tb, width), lambda b, j, i: (0, b, i, j)),
                   pl.BlockSpec((1, 1, tb, width), lambda b, j, i: (0, b, nt - 1 - i, j))],
        scratch_shapes=[pltpu.VMEM((2, RWKV_PAIRS, HEAD_DIM, LANES), F32),
                        pltpu.VMEM((2, 2, RWKV_PAIRS, 3 * HEAD_DIM, LANES), BF16),
                        pltpu.VMEM((2, 2, RWKV_PAIRS, HEAD_DIM, LANES), F32)],
        compiler_params=pltpu.CompilerParams(dimension_semantics=("parallel", "parallel", "arbitrary")),
        name="rwkv7_scan",
        interpret=interpret,
    )(a, b, k, r, g, v, a, b, k, r, g, v, _rwkv_selectors())
    return jnp.concatenate([yf, yb], axis=0)


HALO = 8


def _head_sums(x, bd_ref):
    bd = bd_ref[...]
    return sum(jnp.dot(piece.astype(BF16), bd, preferred_element_type=F32) for piece in _split3(x))


def _token_shift(cur, prev_row, next_row, mu):
    tb = cur.shape[0]
    rid = lax.broadcasted_iota(jnp.int32, cur.shape, 0)
    prev = jnp.where(rid == 0, prev_row, pltpu.roll(cur, 1, 0))
    nxt = jnp.where(rid == tb - 1, next_row, pltpu.roll(cur, tb - 1, 0))
    return cur + mu[0:1] * (prev - cur) + mu[1:2] * (nxt - cur)


def _rwkv_prep_body(x_ref, xp_ref, xn_ref, r_ref, rp_ref, rn_ref, k_ref, kp_ref, kn_ref, v_ref, vp_ref, vn_ref,
                    mux_ref, mur_ref, w1_ref, w2_ref, w0_ref, a1_ref, a2_ref, a0_ref, g1_ref, g2_ref,
                    kk_w_ref, ka_w_ref, bd_ref, tri_ref,
                    ro_ref, vo_ref, go_ref, kd_ref, sa_ref, sb_ref, sk_ref, sr_ref, sg_ref):
    i = pl.program_id(1)
    first = i == 0
    last = i == pl.num_programs(1) - 1

    def shifted(cur_ref, p_ref, n_ref, mu):
        prev_row = jnp.where(first, 0.0, p_ref[0, HALO - 1:HALO, :])
        next_row = jnp.where(last, 0.0, n_ref[0, 0:1, :])
        return _token_shift(cur_ref[0], prev_row, next_row, mu)

    r = shifted(r_ref, rp_ref, rn_ref, mur_ref[0])
    k = shifted(k_ref, kp_ref, kn_ref, mur_ref[1])
    v = shifted(v_ref, vp_ref, vn_ref, mur_ref[2])
    xw = shifted(x_ref, xp_ref, xn_ref, mux_ref[0]).astype(BF16)
    xa = shifted(x_ref, xp_ref, xn_ref, mux_ref[1]).astype(BF16)
    xg = shifted(x_ref, xp_ref, xn_ref, mux_ref[2]).astype(BF16)
    hg = jax.nn.sigmoid(jnp.dot(xg, g1_ref[...], preferred_element_type=F32)).astype(BF16)
    go_ref[0] = jnp.dot(hg, g2_ref[...], preferred_element_type=F32)
    kk = k * kk_w_ref[...]
    kk = kk / jnp.maximum(jnp.sqrt(_head_sums(kk * kk, bd_ref)), 1e-12)
    ro_ref[0] = r
    vo_ref[0] = v
    for z in range(2):
        hw = jnp.tanh(jnp.dot(xw, w1_ref[z], preferred_element_type=F32)).astype(BF16)
        ha = jnp.dot(xa, a1_ref[z], preferred_element_type=F32).astype(BF16)
        w_raw = w0_ref[z:z + 1] + jnp.dot(hw, w2_ref[z], preferred_element_type=F32)
        lw = -math.exp(-0.5) * jax.nn.sigmoid(w_raw)
        a = jax.nn.sigmoid(a0_ref[z:z + 1]
                           + jnp.dot(ha, a2_ref[z], preferred_element_type=F32))
        kd = k * (1.0 + (a - 1.0) * ka_w_ref[...])
        kd_ref[z, 0] = kd
        cum = sum(jnp.dot(tri_ref[z], piece.astype(BF16), preferred_element_type=F32) for piece in _split3(lw))
        gamma = jnp.exp(cum)
        inv_gamma = jnp.exp(-cum)
        sa_ref[z, 0] = -kk * jnp.exp(cum - lw)
        sb_ref[z, 0] = kk * a * inv_gamma
        sk_ref[z, 0] = kd * inv_gamma
        sr_ref[z, 0] = r * gamma
        sg_ref[z, 0] = gamma


def _rwkv_post_body(ys_ref, r_ref, v_ref, kd_ref, g_ref, lng_ref, lnb_ref, rk_ref, bd_ref, o_ref):
    y = ys_ref[0, 0] + ys_ref[1, 0]
    inv_n = 1.0 / HEAD_DIM
    yc = y - _head_sums(y, bd_ref) * inv_n
    yn = yc * lax.rsqrt(_head_sums(yc * yc, bd_ref) * inv_n + GN_EPS) * lng_ref[...] + lnb_ref[...]
    r, v = r_ref[0], v_ref[0]
    bonus = (_head_sums(r * kd_ref[0, 0] * rk_ref[...], bd_ref)
             + _head_sums(r * kd_ref[1, 0] * rk_ref[...], bd_ref)) * v
    o_ref[0] = ((yn + bonus) * g_ref[0]).astype(o_ref.dtype)


def _rwkv7_mixer(u, proj, p, *, tb=256):
    B, T, D = u.shape
    C = D_RWKV
    tb = min(tb, T)
    nt = T // tb
    assert T % tb == 0 and tb % HALO == 0
    hb = tb // HALO

    def cur(width, col):
        return pl.BlockSpec((1, tb, width), lambda b, i: (b, i, col))

    def prev(width, col):
        return pl.BlockSpec((1, HALO, width), lambda b, i: (b, jnp.maximum(i * hb - 1, 0), col))

    def nxt(width, col):
        return pl.BlockSpec((1, HALO, width), lambda b, i: (b, jnp.minimum((i + 1) * hb, T // HALO - 1), col))

    def full(a):
        return pl.BlockSpec(a.shape, lambda b, i: (0,) * a.ndim)

    lane = np.arange(C) // HEAD_DIM
    bd = jnp.asarray(lane[:, None] == lane[None, :], BF16)
    t_id = np.arange(tb)
    same = (t_id[:, None] // RWKV_SUB) == (t_id[None, :] // RWKV_SUB)
    tri = jnp.asarray(np.stack([same & (t_id[None, :] <= t_id[:, None]),
                                same & (t_id[None, :] >= t_id[:, None])]), BF16)
    weights = [p['rwkv_mu_x'], p['rwkv_mu_rkv'], p['rwkv_w1'].astype(BF16), p['rwkv_w2'].astype(BF16), p['rwkv_w0'],
               p['rwkv_a1'].astype(BF16), p['rwkv_a2'].astype(BF16), p['rwkv_a0'], p['rwkv_g1'].astype(BF16), p['rwkv_g2'].astype(BF16),
               p['rwkv_k_k'][None], p['rwkv_k_a'][None], bd, tri]
    one = jax.ShapeDtypeStruct((B, T, C), F32)
    two = jax.ShapeDtypeStruct((2, B, T, C), F32)
    out1 = pl.BlockSpec((1, tb, C), lambda b, i: (b, i, 0))
    out2 = pl.BlockSpec((2, 1, tb, C), lambda b, i: (0, b, i, 0))
    r, v, g, kd, sa, sb, sk, sr, sg = pl.pallas_call(
        _rwkv_prep_body,
        out_shape=[one, one, one, two, two, two, two, two, two],
        grid=(B, nt),
        in_specs=[cur(D, 0), prev(D, 0), nxt(D, 0)]
        + [spec(C, col) for col in range(3) for spec in (cur, prev, nxt)]
        + [full(w) for w in weights],
        out_specs=[out1, out1, out1, out2, out2, out2, out2, out2, out2],
        compiler_params=pltpu.CompilerParams(dimension_semantics=("parallel", "parallel")),
        name="rwkv7_prepare",
    )(u, u, u, *([proj] * 9), *weights)
    ys = _rwkv7_scan(sa, sb, sk, sr, sg, v)
    post_w = [p['rwkv_lnx_g'][None], p['rwkv_lnx_b'][None], p['rwkv_r_k'].reshape(1, C), bd]
    return pl.pallas_call(
        _rwkv_post_body,
        out_shape=jax.ShapeDtypeStruct((B, T, C), BF16),
        grid=(B, nt),
        in_specs=[out2, out1, out1, out2, out1] + [full(w) for w in post_w],
        out_specs=out1,
        compiler_params=pltpu.CompilerParams(dimension_semantics=("parallel", "parallel")),
        name="rwkv7_output",
    )(ys, r, v, kd, g, *post_w)


NA_ROWS_PER_STEP = 8


def _na_body(q_ref, k_ref, v_ref, bias_ref, mask_ref, o_ref, *, rows):
    kr = NA_WIN_ROWS
    win = kr * GRID_W
    valid = mask_ref[...] != 0
    lane = lax.broadcasted_iota(jnp.int32, (GRID_W, LANES), 1)
    for ii in range(NA_ROWS_PER_STEP):
        i = pl.program_id(1) * NA_ROWS_PER_STEP + ii
        start = jnp.clip(i - kr // 2, 0, rows - kr)
        d = i - start
        krows = pl.ds(pl.multiple_of(start * GRID_W, GRID_W), win)
        for pair in range(D_NA // LANES):
            lanes = slice(pair * LANES, (pair + 1) * LANES)
            q2 = q_ref[0, ii * GRID_W:(ii + 1) * GRID_W, lanes] * (HEAD_DIM ** -0.5)
            k2 = k_ref[0, krows, lanes]
            v2 = v_ref[0, krows, lanes]
            outs = []
            for hh in range(LANES // HEAD_DIM):
                in_head = (lane >= hh * HEAD_DIM) & (lane < (hh + 1) * HEAD_DIM)
                qh = jnp.where(in_head, q2, 0.0).astype(BF16)
                s = lax.dot_general(qh, k2, (((1,), (1,)), ((), ())), preferred_element_type=F32)
                s = s + bias_ref[d, pair * (LANES // HEAD_DIM) + hh]
                s = jnp.where(valid, s, -1e30)
                m = jnp.max(s, axis=-1, keepdims=True)
                e = jnp.exp(s - m)
                p = e / jnp.sum(e, axis=-1, keepdims=True)
                outs.append(jnp.dot(p.astype(BF16), v2, preferred_element_type=F32))
            o_ref[0, ii * GRID_W:(ii + 1) * GRID_W, lanes] = jnp.where(lane < HEAD_DIM, outs[0], outs[1])


def _neighbourhood_attention_pallas(q, k, v, rpb, *, interpret=False):
    B, T, _ = q.shape
    rows = T // GRID_W
    kr, kc = NA_WIN_ROWS, NA_WIN_COLS
    assert rows >= kr and rows % NA_ROWS_PER_STEP == 0
    ci = jnp.arange(GRID_W)
    col_start = jnp.clip(ci - kc // 2, 0, GRID_W - kc)
    col_valid = (ci[None] >= col_start[:, None]) & (ci[None] < col_start[:, None] + kc)
    mask = jnp.tile(col_valid.astype(jnp.int32), (1, kr))
    dc_idx = jnp.clip(ci[None] - ci[:, None] + kc - 1, 0, 2 * kc - 2)
    dr_idx = jnp.arange(kr)[None, :] - jnp.arange(kr)[:, None] + kr - 1
    onehot = (jnp.arange(2 * kc - 1)[:, None] == dc_idx.reshape(1, -1)).astype(F32)
    bias = jnp.dot(rpb[:, dr_idx].reshape(-1, 2 * kc - 1), onehot, precision=lax.Precision.HIGHEST)
    bias = bias.reshape(NA_HEADS, kr, kr, GRID_W, GRID_W)
    bias = jnp.transpose(bias, (1, 0, 3, 2, 4)).reshape(kr, NA_HEADS, GRID_W, kr * GRID_W)
    tq = NA_ROWS_PER_STEP * GRID_W
    return pl.pallas_call(
        functools.partial(_na_body, rows=rows),
        out_shape=jax.ShapeDtypeStruct((B, T, D_NA), F32),
        grid=(B, rows // NA_ROWS_PER_STEP),
        in_specs=[pl.BlockSpec((1, tq, D_NA), lambda b, i: (b, i, 0)),
                  pl.BlockSpec((1, T, D_NA), lambda b, i: (b, 0, 0)),
                  pl.BlockSpec((1, T, D_NA), lambda b, i: (b, 0, 0)),
                  pl.BlockSpec((kr, NA_HEADS, GRID_W, kr * GRID_W), lambda b, i: (0, 0, 0, 0)),
                  pl.BlockSpec((GRID_W, kr * GRID_W), lambda b, i: (0, 0))],
        out_specs=pl.BlockSpec((1, tq, D_NA), lambda b, i: (b, i, 0)),
        compiler_params=pltpu.CompilerParams(dimension_semantics=("parallel", "arbitrary")),
        name="neighbourhood_attention",
        interpret=interpret,
    )(q, k.astype(BF16), v.astype(BF16), bias, mask)


def _hyena_filter_spectra(L, p):
    t = jnp.linspace(0.0, 1.0, L, dtype=F32)[:, None]
    n_bands = (HY_EMB - 1) // 2
    omega = 2.0 * math.pi * jnp.arange(L, dtype=F32)[:, None] / L
    bands = jnp.linspace(1e-4, n_bands - 1, n_bands, dtype=F32)[None]
    z = jnp.concatenate([t, jnp.cos(bands * omega), -jnp.sin(bands * omega)], -1)
    freq = p['hy_freq']
    h = jnp.sin(freq * (z @ p['hy_w1'] + p['hy_b1']))
    h = jnp.sin(freq * (h @ p['hy_w2'] + p['hy_b2']))
    h = jnp.sin(freq * (h @ p['hy_w3'] + p['hy_b3']))
    h = (h @ p['hy_w4']).reshape(L, HY_ORDER, 2, D_HY)
    deltas = jnp.abs(jnp.linspace(math.log(HY_TOL) / HY_FAST_PCT, math.log(HY_TOL) / HY_SLOW_PCT, D_HY, dtype=F32))
    h = h * jnp.exp(-t * deltas)[:, None, None, :]
    h_f, h_b = h[:, :, 0], h[:, :, 1]
    kern = jnp.concatenate([h_f, jnp.zeros((1, HY_ORDER, D_HY), F32), h_b[1:][::-1]], 0)
    kern = kern / jnp.sum(jnp.abs(kern), 0, keepdims=True)
    return jnp.fft.fft(kern, axis=0)


def _dft_tables(R):
    n = np.arange(R)
    ang = (2.0 * np.pi / R) * ((n[:, None] * n[None, :]) % R)
    tw = (2.0 * np.pi / (R * R)) * (n[:, None] * n[None, :])
    c, s, ct, st = (jnp.asarray(f(a), F32) for a in (ang, tw) for f in (np.cos, np.sin))
    first = jnp.concatenate([c, -s], axis=0)[:, :R // 2]
    last = jnp.concatenate([c, -s], axis=1)[:R // 2]
    cp = c[None] * ct[:, None, :] - s[None] * st[:, None, :]
    sp = s[None] * ct[:, None, :] + c[None] * st[:, None, :]
    fwd = jnp.concatenate([jnp.concatenate([cp, sp], axis=2), jnp.concatenate([-sp, cp], axis=2)], axis=1)
    cpt = c[None] * ct[:, :, None] - s[None] * st[:, :, None]
    spt = s[None] * ct[:, :, None] + c[None] * st[:, :, None]
    inv = jnp.concatenate([jnp.concatenate([cpt, -spt], axis=2), jnp.concatenate([spt, cpt], axis=2)], axis=1)
    return first.astype(BF16), fwd.astype(BF16), inv.astype(BF16), last.astype(BF16)


def _conv_first_body(f_ref, z_ref, o_ref):
    o_ref[0] = jnp.dot(f_ref[...], z_ref[0].astype(BF16), preferred_element_type=F32).astype(o_ref.dtype)


def _conv_mid_body(a_ref, mf_ref, mi_ref, h_ref, o_ref, *, R, kb):
    for j in range(kb):
        a = jnp.concatenate([a_ref[0, 0, j], a_ref[0, 1, j]], axis=0)
        x = jnp.dot(mf_ref[j], a, preferred_element_type=F32)
        xr, xi = x[:R], x[R:]
        hr, hi = h_ref[0, j], h_ref[1, j]
        y = jnp.concatenate([xr * hr - xi * hi, xr * hi + xi * hr], axis=0).astype(BF16)
        b = jnp.dot(mi_ref[j], y, preferred_element_type=F32)
        o_ref[0, 0, j] = b[:R].astype(o_ref.dtype)
        o_ref[0, 1, j] = b[R:].astype(o_ref.dtype)


def _conv_last_body(f_ref, b_ref, z_ref, gate_ref, bias_ref, o_ref):
    y = jnp.dot(f_ref[...], b_ref[0], preferred_element_type=F32)
    o_ref[0] = gate_ref[0] * (y + z_ref[0] * bias_ref[...])


def _gated_long_conv(z, gate, hm, bias, tables, *, tn=4096, kb=4, interpret=False):
    B, L, C = z.shape
    R = math.isqrt(2 * L)
    assert R * R == 2 * L and R % (2 * kb) == 0
    first, fwd, inv, last = tables
    tn = min(tn, R * C)
    nt = (R * C) // tn
    z2 = z.reshape(B, R // 2, R * C)
    a = pl.pallas_call(
        _conv_first_body,
        out_shape=jax.ShapeDtypeStruct((B, 2 * R, R * C), BF16),
        grid=(B, nt),
        in_specs=[pl.BlockSpec((2 * R, R // 2), lambda b, j: (0, 0)),
                  pl.BlockSpec((1, R // 2, tn), lambda b, j: (b, 0, j))],
        out_specs=pl.BlockSpec((1, 2 * R, tn), lambda b, j: (b, 0, j)),
        name="long_conv_first",
        interpret=interpret,
    )(first, z2)
    bmid = pl.pallas_call(
        functools.partial(_conv_mid_body, R=R, kb=kb),
        out_shape=jax.ShapeDtypeStruct((B, 2, R, R, C), BF16),
        grid=(B, R // kb),
        in_specs=[pl.BlockSpec((1, 2, kb, R, C), lambda b, k: (b, 0, k, 0, 0)),
                  pl.BlockSpec((kb, 2 * R, 2 * R), lambda b, k: (k, 0, 0)),
                  pl.BlockSpec((kb, 2 * R, 2 * R), lambda b, k: (k, 0, 0)),
                  pl.BlockSpec((2, kb, R, C), lambda b, k: (0, k, 0, 0))],
        out_specs=pl.BlockSpec((1, 2, kb, R, C), lambda b, k: (b, 0, k, 0, 0)),
        name="long_conv_mid",
        interpret=interpret,
    )(a.reshape(B, 2, R, R, C), fwd, inv, hm)
    out = pl.pallas_call(
        _conv_last_body,
        out_shape=jax.ShapeDtypeStruct((B, R // 2, R * C), F32),
        grid=(B, nt),
        in_specs=[pl.BlockSpec((R // 2, 2 * R), lambda b, j: (0, 0)),
                  pl.BlockSpec((1, 2 * R, tn), lambda b, j: (b, 0, j)),
                  pl.BlockSpec((1, R // 2, tn), lambda b, j: (b, 0, j)),
                  pl.BlockSpec((1, R // 2, tn), lambda b, j: (b, 0, j)),
                  pl.BlockSpec((1, tn), lambda b, j: (0, j))],
        out_specs=pl.BlockSpec((1, R // 2, tn), lambda b, j: (b, 0, j)),
        name="long_conv_last",
        interpret=interpret,
    )(last, bmid.reshape(B, 2 * R, R * C), z2, gate.reshape(B, R // 2, R * C), jnp.tile(bias, R)[None])
    return out.reshape(B, L, C)


def _short_conv_body(u_ref, up_ref, un_ref, sw_ref, sb_ref, x1_ref, x2_ref, v_ref):
    i = pl.program_id(1)
    cur = u_ref[0]
    tb = cur.shape[0]
    prev_row = jnp.where(i == 0, 0.0, up_ref[0, HALO - 1:HALO, :])
    next_row = jnp.where(i == pl.num_programs(1) - 1, 0.0, un_ref[0, 0:1, :])
    rid = lax.broadcasted_iota(jnp.int32, cur.shape, 0)
    prev = jnp.where(rid == 0, prev_row, pltpu.roll(cur, 1, 0))
    nxt = jnp.where(rid == tb - 1, next_row, pltpu.roll(cur, tb - 1, 0))
    u = prev * sw_ref[0:1] + cur * sw_ref[1:2] + nxt * sw_ref[2:3] + sb_ref[...]
    x1_ref[0] = u[:, :D_HY]
    x2_ref[0] = u[:, D_HY:2 * D_HY]
    v_ref[0] = u[:, 2 * D_HY:]


def _hyena_short_conv(proj, sw, sb, *, tb=512):
    B, T, n_in = proj.shape
    width = 3 * D_HY
    col = (3 * D_RWKV + 3 * D_NA) // width
    assert col * width == 3 * D_RWKV + 3 * D_NA
    tb = min(tb, T)
    hb = tb // HALO
    out = jax.ShapeDtypeStruct((B, T, D_HY), F32)
    ospec = pl.BlockSpec((1, tb, D_HY), lambda b, i: (b, i, 0))
    return pl.pallas_call(
        _short_conv_body,
        out_shape=[out, out, out],
        grid=(B, T // tb),
        in_specs=[pl.BlockSpec((1, tb, width), lambda b, i: (b, i, col)),
                  pl.BlockSpec((1, HALO, width), lambda b, i: (b, jnp.maximum(i * hb - 1, 0), col)),
                  pl.BlockSpec((1, HALO, width), lambda b, i: (b, jnp.minimum((i + 1) * hb, T // HALO - 1), col)),
                  pl.BlockSpec(sw.shape, lambda b, i: (0, 0)),
                  pl.BlockSpec((1, width), lambda b, i: (0, 0))],
        out_specs=[ospec, ospec, ospec],
        compiler_params=pltpu.CompilerParams(dimension_semantics=("parallel", "parallel")),
        name="hyena_short_conv",
    )(proj, proj, proj, sw, sb[None])


def _hyena_mixer(proj, p):
    T = proj.shape[1]
    R = math.isqrt(2 * T)
    x1, x2, v = _hyena_short_conv(proj, p['hy_short_w'], p['hy_short_b'])
    spec = _hyena_filter_spectra(T, p) * (1.0 / (2 * T))
    spec = jnp.swapaxes(spec.reshape(R, R, HY_ORDER, D_HY), 0, 1)
    tables = _dft_tables(R)
    z = v
    for o, gate in enumerate((x1, x2)):
        hm = jnp.stack([jnp.real(spec[:, :, o]), jnp.imag(spec[:, :, o])])
        z = _gated_long_conv(z, gate, hm, p['hy_bias'][o], tables)
    return z


def _route(aff):
    n = aff.shape[0]
    cap = (CAPACITY_FACTOR * n) // N_EXPERTS
    return lax.top_k(aff.T, cap)


def _mix_body(x_ref, ya_ref, yb_ref, yc_ref, g_ref, wa_ref, wb_ref, wc_ref, wo_ref, lg_ref, lb_ref, wr_ref,
              o_ref, ob_ref, aff_ref):
    g = jax.nn.sigmoid(g_ref[...])
    m = (g[:, :D_MODEL] * jnp.dot(ya_ref[...].astype(BF16), wa_ref[...], preferred_element_type=F32)
         + g[:, D_MODEL:2 * D_MODEL] * jnp.dot(yb_ref[...].astype(BF16), wb_ref[...], preferred_element_type=F32)
         + g[:, 2 * D_MODEL:] * jnp.dot(yc_ref[...].astype(BF16), wc_ref[...], preferred_element_type=F32))
    h = ALPHA * x_ref[...] + jnp.dot(m.astype(BF16), wo_ref[...], preferred_element_type=F32)
    y = _layer_norm(h, lg_ref[...], lb_ref[...])
    o_ref[...] = y
    ob_ref[...] = y.astype(BF16)
    logits = jnp.dot(y, wr_ref[...], precision=lax.Precision.HIGHEST, preferred_element_type=F32)
    e = jnp.exp(logits - jnp.max(logits, -1, keepdims=True))
    aff_ref[...] = e / jnp.sum(e, -1, keepdims=True)


def _mix_and_norm(x, y_a, y_b, y_c, proj, p, *, tm=512):
    n, d = x.shape
    assert n % tm == 0 and proj.shape[1] == 2 * N_BRANCH * d
    row = lambda w: pl.BlockSpec((tm, w), lambda i: (i, 0))
    full = lambda a: pl.BlockSpec(a.shape, lambda i: (0,) * a.ndim)
    wa, wb, wc, wo = (p[k].astype(BF16) for k in ('w_branch_a', 'w_branch_b', 'w_branch_c', 'w_out'))
    lg, lb = p['ln1_g'][None], p['ln1_b'][None]
    return pl.pallas_call(
        _mix_body,
        out_shape=[jax.ShapeDtypeStruct((n, d), F32), jax.ShapeDtypeStruct((n, d), BF16),
                   jax.ShapeDtypeStruct((n, N_EXPERTS), F32)],
        grid=(n // tm,),
        in_specs=[row(d), row(D_RWKV), row(D_NA), row(D_HY),
                  pl.BlockSpec((tm, N_BRANCH * d), lambda i: (i, 1)),
                  full(wa), full(wb), full(wc), full(wo), full(lg), full(lb), full(p['w_router'])],
        out_specs=[row(d), row(d), row(N_EXPERTS)],
        compiler_params=pltpu.CompilerParams(dimension_semantics=("parallel",)),
        name="mix_norm_route",
    )(x, y_a, y_b, y_c, proj, wa, wb, wc, wo, lg, lb, p['w_router'])


def _mixers(x, p):
    B, T, D = x.shape
    n = B * T
    proj2 = _matmul(x.reshape(n, D), p['w_in'])
    proj = proj2.reshape(B, T, -1)
    qb, kb, vb = (proj[..., 3 * D_RWKV + j * D_NA:3 * D_RWKV + (j + 1) * D_NA] for j in range(3))
    y_a = _rwkv7_mixer(x, proj, p)
    y_b = _neighbourhood_attention_pallas(qb, kb, vb, p['na_rpb'])
    y_c = _hyena_mixer(proj, p)
    return _mix_and_norm(x.reshape(n, D), y_a.reshape(n, -1), y_b.reshape(n, -1), y_c.reshape(n, -1), proj2, p)


def _encoder_layer(xs, p):
    mixed = [_mixers(x, p) for x in xs]
    routes = [_route(aff) for _, _, aff in mixed]
    xe = jnp.concatenate([jnp.take(xb, idx, axis=0) for (_, xb, _), (_, idx) in zip(mixed, routes)], axis=1)
    gate = jnp.concatenate([g for g, _ in routes], axis=1)[..., None]
    ye = _expert_ffn(xe, gate, p['w_exp_gate'], p['w_exp_up'], p['w_exp_down'])
    outs, off = [], 0
    for x, (x1, _, _), (_, idx) in zip(xs, mixed, routes):
        cap = idx.shape[1]
        ffn = jnp.zeros_like(x1).at[idx.reshape(-1)].add(ye[:, off:off + cap].reshape(-1, x1.shape[-1]))
        off += cap
        outs.append(_ln(x1, p['ln2_g'], p['ln2_b'], residual=ffn).reshape(x.shape))
    return outs


def kernel(x_prompt, x_sample, ln_in_g, ln_in_b, w_in, rwkv_mu_rkv, rwkv_mu_x, rwkv_w0, rwkv_w1, rwkv_w2,
           rwkv_a0, rwkv_a1, rwkv_a2, rwkv_g1, rwkv_g2, rwkv_k_k, rwkv_k_a, rwkv_r_k, rwkv_lnx_g, rwkv_lnx_b,
           na_rpb, hy_short_w, hy_short_b, hy_w1, hy_b1, hy_w2, hy_b2, hy_w3, hy_b3, hy_w4, hy_freq, hy_bias,
           w_branch_a, w_branch_b, w_branch_c, w_out, ln1_g, ln1_b, w_router, w_exp_gate, w_exp_up,
           w_exp_down, ln2_g, ln2_b):
    stacked = {
        'w_in': w_in, 'rwkv_mu_rkv': rwkv_mu_rkv, 'rwkv_mu_x': rwkv_mu_x, 'rwkv_w0': rwkv_w0,
        'rwkv_w1': rwkv_w1, 'rwkv_w2': rwkv_w2, 'rwkv_a0': rwkv_a0, 'rwkv_a1': rwkv_a1, 'rwkv_a2': rwkv_a2,
        'rwkv_g1': rwkv_g1, 'rwkv_g2': rwkv_g2, 'rwkv_k_k': rwkv_k_k, 'rwkv_k_a': rwkv_k_a,
        'rwkv_r_k': rwkv_r_k, 'rwkv_lnx_g': rwkv_lnx_g, 'rwkv_lnx_b': rwkv_lnx_b, 'na_rpb': na_rpb,
        'hy_short_w': hy_short_w, 'hy_short_b': hy_short_b, 'hy_w1': hy_w1, 'hy_b1': hy_b1,
        'hy_w2': hy_w2, 'hy_b2': hy_b2, 'hy_w3': hy_w3, 'hy_b3': hy_b3, 'hy_w4': hy_w4,
        'hy_freq': hy_freq, 'hy_bias': hy_bias, 'w_branch_a': w_branch_a, 'w_branch_b': w_branch_b,
        'w_branch_c': w_branch_c, 'w_out': w_out, 'ln1_g': ln1_g, 'ln1_b': ln1_b, 'w_router': w_router,
        'w_exp_gate': w_exp_gate, 'w_exp_up': w_exp_up, 'w_exp_down': w_exp_down,
        'ln2_g': ln2_g, 'ln2_b': ln2_b,
    }
    xs = [_ln(x.reshape(-1, D_MODEL), ln_in_g, ln_in_b).reshape(x.shape) for x in (x_prompt, x_sample)]
    for l in range(DEPTH):
        xs = _encoder_layer(xs, {name: arr[l] for name, arr in stacked.items()})
    return tuple(xs)
```

```python
import functools
import math

import jax
import jax.numpy as jnp
import numpy as np
from jax import lax
from jax.experimental import pallas as pl
from jax.experimental.pallas import tpu as pltpu

D_MODEL = 1024
DEPTH = 2
GRID_W = 64
HEAD_DIM = 64
D_RWKV = D_MODEL // 2
RWKV_HEADS = D_RWKV // HEAD_DIM
D_NA = D_MODEL // 4
NA_HEADS = D_NA // HEAD_DIM
D_HY = D_MODEL // 4
N_BRANCH = 3
GN_EPS = 64e-5
NA_WIN_ROWS = 8
NA_WIN_COLS = 16
HY_ORDER = 2
HY_EMB = 33
HY_TOL = 1e-2
HY_FAST_PCT = 0.3
HY_SLOW_PCT = 1.5
N_EXPERTS = 16
CAPACITY_FACTOR = 2
ALPHA = (2 * DEPTH) ** 0.25
LN_EPS = 1e-5

F32 = jnp.float32
BF16 = jnp.bfloat16
EXPERT_FFN_VMEM_BYTES = 52 * 1024 * 1024


def _mm_body(x_ref, w_ref, o_ref):
    o_ref[...] = jnp.dot(x_ref[...].astype(BF16), w_ref[...].astype(BF16),
                         preferred_element_type=F32).astype(o_ref.dtype)


def _matmul(x, w, *, tm=1024, tn=512, out_dtype=F32):
    m, k = x.shape
    _, n = w.shape
    tm = min(tm, m)
    tn = min(tn, n)
    assert m % tm == 0 and n % tn == 0
    return pl.pallas_call(
        _mm_body,
        out_shape=jax.ShapeDtypeStruct((m, n), out_dtype),
        grid=(m // tm, n // tn),
        in_specs=[pl.BlockSpec((tm, k), lambda i, j: (i, 0)),
                  pl.BlockSpec((k, tn), lambda i, j: (0, j))],
        out_specs=pl.BlockSpec((tm, tn), lambda i, j: (i, j)),
        name="dense_matmul",
    )(x, w)


def _expert_ffn_body(x_ref, wg_ref, wu_ref, wd_ref, gate_ref, o_ref, acc_ref):
    f = pl.program_id(2)

    @pl.when(f == 0)
    def _():
        acc_ref[...] = jnp.zeros_like(acc_ref)

    xb = x_ref[0]
    hg = jnp.dot(xb, wg_ref[0, 0].astype(BF16), preferred_element_type=F32)
    hu = jnp.dot(xb, wu_ref[0, 0].astype(BF16), preferred_element_type=F32)
    h = (hg * jax.nn.sigmoid(hg) * hu).astype(BF16)
    acc_ref[...] += jnp.dot(h, wd_ref[0, 0].astype(BF16), preferred_element_type=F32)

    @pl.when(f == pl.num_programs(2) - 1)
    def _():
        o_ref[0] = acc_ref[...] * gate_ref[0]


def _expert_ffn(xe, gate, w_gate, w_up, w_down, layer, *, tm=2048, tf=256):
    e, m, d = xe.shape
    ff = w_gate.shape[-1]
    tm = min(tm, m)
    assert m % tm == 0 and ff % tf == 0
    return pl.pallas_call(
        _expert_ffn_body,
        out_shape=jax.ShapeDtypeStruct((e, m, d), F32),
        grid=(e, m // tm, ff // tf),
        in_specs=[pl.BlockSpec((1, tm, d), lambda g, i, f: (g, i, 0)),
                  pl.BlockSpec((1, 1, d, tf), lambda g, i, f: (layer, g, 0, f)),
                  pl.BlockSpec((1, 1, d, tf), lambda g, i, f: (layer, g, 0, f)),
                  pl.BlockSpec((1, 1, tf, d), lambda g, i, f: (layer, g, f, 0)),
                  pl.BlockSpec((1, tm, 1), lambda g, i, f: (g, i, 0))],
        out_specs=pl.BlockSpec((1, tm, d), lambda g, i, f: (g, i, 0)),
        scratch_shapes=[pltpu.VMEM((tm, d), F32)],
        compiler_params=pltpu.CompilerParams(
            dimension_semantics=("parallel", "parallel", "arbitrary"), vmem_limit_bytes=EXPERT_FFN_VMEM_BYTES),
        name="expert_ffn",
    )(xe, w_gate, w_up, w_down, gate)


def _layer_norm(h, g, b):
    hc = h - jnp.mean(h, -1, keepdims=True)
    var = jnp.mean(hc * hc, -1, keepdims=True)
    return hc * lax.rsqrt(var + LN_EPS) * g + b


def _ln_body(x_ref, g_ref, b_ref, o_ref):
    o_ref[...] = _layer_norm(x_ref[...], g_ref[...], b_ref[...])


def _ln_residual_body(x_ref, r_ref, g_ref, b_ref, o_ref):
    o_ref[...] = _layer_norm(ALPHA * x_ref[...] + r_ref[...], g_ref[...], b_ref[...])


def _ln(x, g, b, residual=None, *, tm=1024):
    n, d = x.shape
    tm = min(tm, n)
    assert n % tm == 0
    row = pl.BlockSpec((tm, d), lambda i: (i, 0))
    vec = pl.BlockSpec((1, d), lambda i: (0, 0))
    args = (x,) if residual is None else (x, residual)
    return pl.pallas_call(
        _ln_body if residual is None else _ln_residual_body,
        out_shape=jax.ShapeDtypeStruct((n, d), F32),
        grid=(n // tm,),
        in_specs=[row] * len(args) + [vec, vec],
        out_specs=row,
        compiler_params=pltpu.CompilerParams(dimension_semantics=("parallel",)),
        name="layer_norm",
    )(*args, g[None], b[None])


RWKV_SUB = 16
RWKV_PAIRS = 2
LANES = 128
RWKV_PIECES = 3
RWKV_GAMMA_SLOT = 2 * RWKV_PIECES * RWKV_SUB


def _rwkv_selectors():
    z = np.zeros((RWKV_SUB + 1, LANES, LANES), np.float32)
    for h in range(LANES // HEAD_DIM):
        lanes = slice(h * HEAD_DIM, (h + 1) * HEAD_DIM)
        for p in range(RWKV_PIECES):
            for t in range(RWKV_SUB):
                z[t, (h * RWKV_PIECES + p) * RWKV_SUB + t, lanes] = 1.0
            z[RWKV_SUB, RWKV_GAMMA_SLOT + 8 * h + p, lanes] = 1.0
    return jnp.asarray(z, BF16)


def _split3(x):
    hi = x.astype(BF16).astype(F32)
    rem = x - hi
    mid = rem.astype(BF16).astype(F32)
    lo = (rem - mid).astype(BF16).astype(F32)
    return [hi, mid, lo]


def _operand_tile(xa, xb, extra):
    pa = jnp.concatenate(_split3(xa), axis=0)
    pb = pa if xb is None else jnp.concatenate(_split3(xb), axis=0)
    lane_lo = lax.broadcasted_iota(jnp.int32, pa.shape, 1) < HEAD_DIM
    head0 = jnp.where(lane_lo, pa, pltpu.roll(pb, HEAD_DIM, 1))
    head1 = jnp.where(lane_lo, pltpu.roll(pa, HEAD_DIM, 1), pb)
    tt = jnp.concatenate([head0, head1, extra], axis=0).T
    return tt[:HEAD_DIM].astype(BF16), tt[HEAD_DIM:].astype(BF16)


def _rwkv_prepare(refs, base, lanes, reverse):
    a_ref, b_ref, k_ref, r_ref, g_ref = refs
    rows = pl.ds(base, RWKV_SUB)
    last = 0 if reverse else RWKV_SUB - 1
    g_last = g_ref[0, 0, rows, lanes][last:last + 1]
    gam = jnp.concatenate(_split3(g_last) + [jnp.zeros((8 - RWKV_PIECES, LANES), F32)], axis=0)
    extra = jnp.concatenate([gam, pltpu.roll(gam, HEAD_DIM, 1), jnp.zeros((16, LANES), F32)], axis=0)
    g_a, g_b = _operand_tile(a_ref[0, 0, rows, lanes], b_ref[0, 0, rows, lanes], extra)
    g_k, _ = _operand_tile(k_ref[0, 0, rows, lanes], None, jnp.zeros((32, LANES), F32))
    rg = r_ref[0, 0, rows, lanes]
    pad = jnp.zeros((HEAD_DIM - RWKV_SUB, LANES), F32)
    r_t = jnp.concatenate([rg, pad, pltpu.roll(rg, HEAD_DIM, 1), pad], axis=0).T[:HEAD_DIM]
    return jnp.concatenate([g_a, g_b, g_k], axis=0), r_t


def _rwkv_steps(streams, z_ref, s_ref):
    head_lane = (lax.broadcasted_iota(jnp.int32, (HEAD_DIM, LANES), 1) // HEAD_DIM) * HEAD_DIM
    lanes = [slice(q * LANES, (q + 1) * LANES) for _, _, _, _, _, q, _, _ in streams]
    vs = [s[2][0, pl.ds(s[6], RWKV_SUB), ln] for s, ln in zip(streams, lanes)]
    sts = [s_ref[d, q] for _, _, _, _, d, q, _, _ in streams]
    ys = [[None] * RWKV_SUB for _ in streams]
    for p in range(RWKV_SUB // 2):
        cols = []
        for g_all, _, _, _, _, _, _, reverse in streams:
            ts = (RWKV_SUB - 1 - 2 * p, RWKV_SUB - 2 - 2 * p) if reverse else (2 * p, 2 * p + 1)
            sel = jnp.concatenate([z_ref[ts[0]], z_ref[ts[1]]], axis=1)
            cols.append((ts, jnp.dot(g_all, sel, preferred_element_type=F32)))
        for n in range(2):
            for i, (_, r_t, _, _, _, _, _, _) in enumerate(streams):
                ts, c = cols[i]
                t = ts[n]
                a_c, b_c, k_c = (c[o * HEAD_DIM:(o + 1) * HEAD_DIM, n * LANES:(n + 1) * LANES] for o in range(3))
                r_c = jnp.take_along_axis(r_t, head_lane + t, axis=1)
                sa = jnp.sum(sts[i] * a_c, axis=0, keepdims=True)
                sts[i] = sts[i] + b_c * sa + k_c * vs[i][t:t + 1, :]
                ys[i][t] = jnp.sum(sts[i] * r_c, axis=0, keepdims=True)
    for i, (g_all, _, _, y_ref, d, q, base, _) in enumerate(streams):
        s_ref[d, q] = sts[i] * jnp.dot(g_all[:HEAD_DIM], z_ref[RWKV_SUB], preferred_element_type=F32)
        y_ref[0, 0, pl.ds(base, RWKV_SUB), lanes[i]] = jnp.concatenate(ys[i], axis=0)


def _rwkv_body(af, bf, kf, rf, gf, vf, ab, bb, kb, rb, gb, vb, z_ref, yf_ref, yb_ref, s_ref, g_ref,
               rt_ref, *, tb):
    @pl.when(pl.program_id(2) == 0)
    def _():
        s_ref[...] = jnp.zeros_like(s_ref)

    nsub = tb // RWKV_SUB
    refs_f = (af, bf, kf, rf, gf)
    refs_b = (ab, bb, kb, rb, gb)

    def base_f(c):
        return pl.multiple_of(c * RWKV_SUB, RWKV_SUB)

    def base_b(c):
        return pl.multiple_of((nsub - 1 - c) * RWKV_SUB, RWKV_SUB)

    def prepare(slot, c):
        for q in range(RWKV_PAIRS):
            lanes = slice(q * LANES, (q + 1) * LANES)
            g_ref[slot, 0, q], rt_ref[slot, 0, q] = _rwkv_prepare(refs_f, base_f(c), lanes, False)
            g_ref[slot, 1, q], rt_ref[slot, 1, q] = _rwkv_prepare(refs_b, base_b(c), lanes, True)

    prepare(0, 0)

    def sub(c, carry):
        slot = c % 2
        nxt = jnp.minimum(c + 1, nsub - 1)
        streams = []
        for q in range(RWKV_PAIRS):
            streams.append((g_ref[slot, 0, q], rt_ref[slot, 0, q], vf, yf_ref, 0, q, base_f(c), False))
            streams.append((g_ref[slot, 1, q], rt_ref[slot, 1, q], vb, yb_ref, 1, q, base_b(c), True))
        prepare(1 - slot, nxt)
        _rwkv_steps(streams, z_ref, s_ref)
        return carry

    lax.fori_loop(0, nsub, sub, 0)


def _rwkv7_scan(a, b, k, r, g, v, *, tb=512, interpret=False):
    B, T, C = v.shape
    tb = min(tb, T)
    nt = T // tb
    width = RWKV_PAIRS * LANES
    assert T % tb == 0 and tb % RWKV_SUB == 0 and C % width == 0
    fwd3 = pl.BlockSpec((1, tb, width), lambda b, j, i: (b, i, j))
    bwd3 = pl.BlockSpec((1, tb, width), lambda b, j, i: (b, nt - 1 - i, j))
    fwd4 = pl.BlockSpec((1, 1, tb, width), lambda b, j, i: (0, b, i, j))
    bwd4 = pl.BlockSpec((1, 1, tb, width), lambda b, j, i: (1, b, nt - 1 - i, j))
    yf, yb = pl.pallas_call(
        functools.partial(_rwkv_body, tb=tb),
        out_shape=[jax.ShapeDtypeStruct((1, B, T, C), F32)] * 2,
        grid=(B, C // width, nt),
        in_specs=[fwd4] * 5 + [fwd3] + [bwd4] * 5 + [bwd3]
        + [pl.BlockSpec((RWKV_SUB + 1, LANES, LANES), lambda b, j, i: (0, 0, 0))],
        out_specs=[pl.BlockSpec((1, 1, tb, width), lambda b, j, i: (0, b, i, j)),
                   pl.BlockSpec((1, 1, tb, width), lambda b, j, i: (0, b, nt - 1 - i, j))],
        scratch_shapes=[pltpu.VMEM((2, RWKV_PAIRS, HEAD_DIM, LANES), F32),
                        pltpu.VMEM((2, 2, RWKV_PAIRS, 3 * HEAD_DIM, LANES), BF16),
                        pltpu.VMEM((2, 2, RWKV_PAIRS, HEAD_DIM, LANES), F32)],
        compiler_params=pltpu.CompilerParams(dimension_semantics=("parallel", "parallel", "arbitrary")),
        name="rwkv7_scan",
        interpret=interpret,
    )(a, b, k, r, g, v, a, b, k, r, g, v, _rwkv_selectors())
    return yf, yb


HALO = 8


def _head_sums(x, bd_ref):
    bd = bd_ref[...]
    return sum(jnp.dot(piece.astype(BF16), bd, preferred_element_type=F32) for piece in _split3(x))


def _token_shift(cur, prev_row, next_row, mu):
    tb = cur.shape[0]
    rid = lax.broadcasted_iota(jnp.int32, cur.shape, 0)
    prev = jnp.where(rid == 0, prev_row, pltpu.roll(cur, 1, 0))
    nxt = jnp.where(rid == tb - 1, next_row, pltpu.roll(cur, tb - 1, 0))
    return cur + mu[0:1] * (prev - cur) + mu[1:2] * (nxt - cur)


def _rwkv_prep_body(x_ref, xp_ref, xn_ref, r_ref, rp_ref, rn_ref, k_ref, kp_ref, kn_ref, v_ref, vp_ref, vn_ref,
                    mux_ref, mur_ref, w1_ref, w2_ref, w0_ref, a1_ref, a2_ref, a0_ref, g1_ref, g2_ref,
                    kk_w_ref, ka_w_ref, bd_ref, tri_ref,
                    ro_ref, vo_ref, go_ref, kd_ref, sa_ref, sb_ref, sk_ref, sr_ref, sg_ref):
    i = pl.program_id(1)
    first = i == 0
    last = i == pl.num_programs(1) - 1

    def shifted(cur_ref, p_ref, n_ref, mu):
        prev_row = jnp.where(first, 0.0, p_ref[0, HALO - 1:HALO, :])
        next_row = jnp.where(last, 0.0, n_ref[0, 0:1, :])
        return _token_shift(cur_ref[0], prev_row, next_row, mu)

    r = shifted(r_ref, rp_ref, rn_ref, mur_ref[0])
    k = shifted(k_ref, kp_ref, kn_ref, mur_ref[1])
    v = shifted(v_ref, vp_ref, vn_ref, mur_ref[2])
    xw = shifted(x_ref, xp_ref, xn_ref, mux_ref[0]).astype(BF16)
    xa = shifted(x_ref, xp_ref, xn_ref, mux_ref[1]).astype(BF16)
    xg = shifted(x_ref, xp_ref, xn_ref, mux_ref[2]).astype(BF16)
    hg = jax.nn.sigmoid(jnp.dot(xg, g1_ref[...], preferred_element_type=F32)).astype(BF16)
    go_ref[0] = jnp.dot(hg, g2_ref[...], preferred_element_type=F32)
    kk = k * kk_w_ref[...]
    kk = kk / jnp.maximum(jnp.sqrt(_head_sums(kk * kk, bd_ref)), 1e-12)
    ro_ref[0] = r
    vo_ref[0] = v
    for z in range(2):
        hw = jnp.tanh(jnp.dot(xw, w1_ref[z], preferred_element_type=F32)).astype(BF16)
        ha = jnp.dot(xa, a1_ref[z], preferred_element_type=F32).astype(BF16)
        w_raw = w0_ref[z:z + 1] + jnp.dot(hw, w2_ref[z], preferred_element_type=F32)
        lw = -math.exp(-0.5) * jax.nn.sigmoid(w_raw)
        a = jax.nn.sigmoid(a0_ref[z:z + 1]
                           + jnp.dot(ha, a2_ref[z], preferred_element_type=F32))
        kd = k * (1.0 + (a - 1.0) * ka_w_ref[...])
        kd_ref[z, 0] = kd
        cum = sum(jnp.dot(tri_ref[z], piece.astype(BF16), preferred_element_type=F32) for piece in _split3(lw))
        gamma = jnp.exp(cum)
        inv_gamma = jnp.exp(-cum)
        sa_ref[z, 0] = -kk * jnp.exp(cum - lw)
        sb_ref[z, 0] = kk * a * inv_gamma
        sk_ref[z, 0] = kd * inv_gamma
        sr_ref[z, 0] = r * gamma
        sg_ref[z, 0] = gamma


def _rwkv_post_body(yf_ref, yb_ref, r_ref, v_ref, kd_ref, g_ref, lng_ref, lnb_ref, rk_ref, bd_ref, o_ref):
    y = yf_ref[0, 0] + yb_ref[0, 0]
    inv_n = 1.0 / HEAD_DIM
    yc = y - _head_sums(y, bd_ref) * inv_n
    yn = yc * lax.rsqrt(_head_sums(yc * yc, bd_ref) * inv_n + GN_EPS) * lng_ref[...] + lnb_ref[...]
    r, v = r_ref[0], v_ref[0]
    bonus = (_head_sums(r * kd_ref[0, 0] * rk_ref[...], bd_ref)
             + _head_sums(r * kd_ref[1, 0] * rk_ref[...], bd_ref)) * v
    o_ref[0] = ((yn + bonus) * g_ref[0]).astype(o_ref.dtype)


def _rwkv7_mixer(u, proj, p, *, tb=256):
    B, T, D = u.shape
    C = D_RWKV
    tb = min(tb, T)
    nt = T // tb
    assert T % tb == 0 and tb % HALO == 0
    hb = tb // HALO

    def cur(width, col):
        return pl.BlockSpec((1, tb, width), lambda b, i: (b, i, col))

    def prev(width, col):
        return pl.BlockSpec((1, HALO, width), lambda b, i: (b, jnp.maximum(i * hb - 1, 0), col))

    def nxt(width, col):
        return pl.BlockSpec((1, HALO, width), lambda b, i: (b, jnp.minimum((i + 1) * hb, T // HALO - 1), col))

    def full(a):
        return pl.BlockSpec(a.shape, lambda b, i: (0,) * a.ndim)

    lane = np.arange(C) // HEAD_DIM
    bd = jnp.asarray(lane[:, None] == lane[None, :], BF16)
    t_id = np.arange(tb)
    same = (t_id[:, None] // RWKV_SUB) == (t_id[None, :] // RWKV_SUB)
    tri = jnp.asarray(np.stack([same & (t_id[None, :] <= t_id[:, None]),
                                same & (t_id[None, :] >= t_id[:, None])]), BF16)
    weights = [p['rwkv_mu_x'], p['rwkv_mu_rkv'], p['rwkv_w1'].astype(BF16), p['rwkv_w2'].astype(BF16), p['rwkv_w0'],
               p['rwkv_a1'].astype(BF16), p['rwkv_a2'].astype(BF16), p['rwkv_a0'], p['rwkv_g1'].astype(BF16), p['rwkv_g2'].astype(BF16),
               p['rwkv_k_k'][None], p['rwkv_k_a'][None], bd, tri]
    one = jax.ShapeDtypeStruct((B, T, C), F32)
    two = jax.ShapeDtypeStruct((2, B, T, C), F32)
    out1 = pl.BlockSpec((1, tb, C), lambda b, i: (b, i, 0))
    out2 = pl.BlockSpec((2, 1, tb, C), lambda b, i: (0, b, i, 0))
    r, v, g, kd, sa, sb, sk, sr, sg = pl.pallas_call(
        _rwkv_prep_body,
        out_shape=[one, one, one, two, two, two, two, two, two],
        grid=(B, nt),
        in_specs=[cur(D, 0), prev(D, 0), nxt(D, 0)]
        + [spec(C, col) for col in range(3) for spec in (cur, prev, nxt)]
        + [full(w) for w in weights],
        out_specs=[out1, out1, out1, out2, out2, out2, out2, out2, out2],
        compiler_params=pltpu.CompilerParams(dimension_semantics=("parallel", "parallel")),
        name="rwkv7_prepare",
    )(u, u, u, *([proj] * 9), *weights)
    yf, yb = _rwkv7_scan(sa, sb, sk, sr, sg, v)
    ydir = pl.BlockSpec((1, 1, tb, C), lambda b, i: (0, b, i, 0))
    post_w = [p['rwkv_lnx_g'][None], p['rwkv_lnx_b'][None], p['rwkv_r_k'].reshape(1, C), bd]
    return pl.pallas_call(
        _rwkv_post_body,
        out_shape=jax.ShapeDtypeStruct((B, T, C), BF16),
        grid=(B, nt),
        in_specs=[ydir, ydir, out1, out1, out2, out1] + [full(w) for w in post_w],
        out_specs=out1,
        compiler_params=pltpu.CompilerParams(dimension_semantics=("parallel", "parallel")),
        name="rwkv7_output",
    )(yf, yb, r, v, kd, g, *post_w)


NA_ROWS_PER_STEP = 8


def _na_body(q_ref, k_ref, v_ref, bias_ref, mask_ref, o_ref, *, rows):
    kr = NA_WIN_ROWS
    win = kr * GRID_W
    valid = mask_ref[...] != 0
    lane = lax.broadcasted_iota(jnp.int32, (GRID_W, LANES), 1)
    for ii in range(NA_ROWS_PER_STEP):
        i = pl.program_id(1) * NA_ROWS_PER_STEP + ii
        start = jnp.clip(i - kr // 2, 0, rows - kr)
        d = i - start
        krows = pl.ds(pl.multiple_of(start * GRID_W, GRID_W), win)
        for pair in range(D_NA // LANES):
            lanes = slice(pair * LANES, (pair + 1) * LANES)
            q2 = q_ref[0, ii * GRID_W:(ii + 1) * GRID_W, lanes] * (HEAD_DIM ** -0.5)
            k2 = k_ref[0, krows, lanes]
            v2 = v_ref[0, krows, lanes]
            outs = []
            for hh in range(LANES // HEAD_DIM):
                in_head = (lane >= hh * HEAD_DIM) & (lane < (hh + 1) * HEAD_DIM)
                qh = jnp.where(in_head, q2, 0.0).astype(BF16)
                s = lax.dot_general(qh, k2, (((1,), (1,)), ((), ())), preferred_element_type=F32)
                s = s + bias_ref[d, pair * (LANES // HEAD_DIM) + hh]
                s = jnp.where(valid, s, -1e30)
                m = jnp.max(s, axis=-1, keepdims=True)
                e = jnp.exp(s - m)
                p = e / jnp.sum(e, axis=-1, keepdims=True)
                outs.append(jnp.dot(p.astype(BF16), v2, preferred_element_type=F32))
            o_ref[0, ii * GRID_W:(ii + 1) * GRID_W, lanes] = jnp.where(lane < HEAD_DIM, outs[0], outs[1])


def _neighbourhood_attention_pallas(proj, k, v, rpb, *, interpret=False):
    B, T, _ = proj.shape
    q_col = 3 * D_RWKV // D_NA
    rows = T // GRID_W
    kr, kc = NA_WIN_ROWS, NA_WIN_COLS
    assert rows >= kr and rows % NA_ROWS_PER_STEP == 0
    ci = jnp.arange(GRID_W)
    col_start = jnp.clip(ci - kc // 2, 0, GRID_W - kc)
    col_valid = (ci[None] >= col_start[:, None]) & (ci[None] < col_start[:, None] + kc)
    mask = jnp.tile(col_valid.astype(jnp.int32), (1, kr))
    dc_idx = jnp.clip(ci[None] - ci[:, None] + kc - 1, 0, 2 * kc - 2)
    dr_idx = jnp.arange(kr)[None, :] - jnp.arange(kr)[:, None] + kr - 1
    onehot = (jnp.arange(2 * kc - 1)[:, None] == dc_idx.reshape(1, -1)).astype(F32)
    bias = jnp.dot(rpb[:, dr_idx].reshape(-1, 2 * kc - 1), onehot, precision=lax.Precision.HIGHEST)
    bias = bias.reshape(NA_HEADS, kr, kr, GRID_W, GRID_W)
    bias = jnp.transpose(bias, (1, 0, 3, 2, 4)).reshape(kr, NA_HEADS, GRID_W, kr * GRID_W)
    tq = NA_ROWS_PER_STEP * GRID_W
    return pl.pallas_call(
        functools.partial(_na_body, rows=rows),
        out_shape=jax.ShapeDtypeStruct((B, T, D_NA), F32),
        grid=(B, rows // NA_ROWS_PER_STEP),
        in_specs=[pl.BlockSpec((1, tq, D_NA), lambda b, i: (b, i, q_col)),
                  pl.BlockSpec((1, T, D_NA), lambda b, i: (b, 0, 0)),
                  pl.BlockSpec((1, T, D_NA), lambda b, i: (b, 0, 0)),
                  pl.BlockSpec((kr, NA_HEADS, GRID_W, kr * GRID_W), lambda b, i: (0, 0, 0, 0)),
                  pl.BlockSpec((GRID_W, kr * GRID_W), lambda b, i: (0, 0))],
        out_specs=pl.BlockSpec((1, tq, D_NA), lambda b, i: (b, i, 0)),
        compiler_params=pltpu.CompilerParams(dimension_semantics=("parallel", "arbitrary")),
        name="neighbourhood_attention",
        interpret=interpret,
    )(proj, k, v, bias, mask)


def _hyena_filter_spectra(L, p):
    t = jnp.linspace(0.0, 1.0, L, dtype=F32)[:, None]
    n_bands = (HY_EMB - 1) // 2
    omega = 2.0 * math.pi * jnp.arange(L, dtype=F32)[:, None] / L
    bands = jnp.linspace(1e-4, n_bands - 1, n_bands, dtype=F32)[None]
    z = jnp.concatenate([t, jnp.cos(bands * omega), -jnp.sin(bands * omega)], -1)
    freq = p['hy_freq']
    h = jnp.sin(freq * (z @ p['hy_w1'] + p['hy_b1']))
    h = jnp.sin(freq * (h @ p['hy_w2'] + p['hy_b2']))
    h = jnp.sin(freq * (h @ p['hy_w3'] + p['hy_b3']))
    h = (h @ p['hy_w4']).reshape(L, HY_ORDER, 2, D_HY)
    deltas = jnp.abs(jnp.linspace(math.log(HY_TOL) / HY_FAST_PCT, math.log(HY_TOL) / HY_SLOW_PCT, D_HY, dtype=F32))
    h = h * jnp.exp(-t * deltas)[:, None, None, :]
    h_f, h_b = h[:, :, 0], h[:, :, 1]
    kern = jnp.concatenate([h_f, jnp.zeros((1, HY_ORDER, D_HY), F32), h_b[1:][::-1]], 0)
    kern = kern / jnp.sum(jnp.abs(kern), 0, keepdims=True)
    return jnp.fft.fft(kern, axis=0)


def _dft_tables(R):
    n = np.arange(R)
    ang = (2.0 * np.pi / R) * ((n[:, None] * n[None, :]) % R)
    tw = (2.0 * np.pi / (R * R)) * (n[:, None] * n[None, :])
    c, s, ct, st = (jnp.asarray(f(a), F32) for a in (ang, tw) for f in (np.cos, np.sin))
    first = jnp.concatenate([c, -s], axis=0)[:, :R // 2]
    last = jnp.concatenate([c, -s], axis=1)[:R // 2]
    cp = c[None] * ct[:, None, :] - s[None] * st[:, None, :]
    sp = s[None] * ct[:, None, :] + c[None] * st[:, None, :]
    fwd = jnp.concatenate([jnp.concatenate([cp, sp], axis=2), jnp.concatenate([-sp, cp], axis=2)], axis=1)
    cpt = c[None] * ct[:, :, None] - s[None] * st[:, :, None]
    spt = s[None] * ct[:, :, None] + c[None] * st[:, :, None]
    inv = jnp.concatenate([jnp.concatenate([cpt, -spt], axis=2), jnp.concatenate([spt, cpt], axis=2)], axis=1)
    return first.astype(BF16), fwd.astype(BF16), inv.astype(BF16), last.astype(BF16)


def _conv_first_body(f_ref, z_ref, o_ref):
    o_ref[0] = jnp.dot(f_ref[...], z_ref[0].astype(BF16), preferred_element_type=F32).astype(o_ref.dtype)


def _conv_mid_body(a_ref, mf_ref, mi_ref, h_ref, o_ref, *, R, kb):
    for j in range(kb):
        a = jnp.concatenate([a_ref[0, 0, j], a_ref[0, 1, j]], axis=0)
        x = jnp.dot(mf_ref[j], a, preferred_element_type=F32)
        xr, xi = x[:R], x[R:]
        hr, hi = h_ref[0, j], h_ref[1, j]
        y = jnp.concatenate([xr * hr - xi * hi, xr * hi + xi * hr], axis=0).astype(BF16)
        b = jnp.dot(mi_ref[j], y, preferred_element_type=F32)
        o_ref[0, 0, j] = b[:R].astype(o_ref.dtype)
        o_ref[0, 1, j] = b[R:].astype(o_ref.dtype)


def _conv_last_body(f_ref, b_ref, z_ref, gate_ref, bias_ref, o_ref):
    y = jnp.dot(f_ref[...], b_ref[0], preferred_element_type=F32)
    o_ref[0] = gate_ref[0] * (y + z_ref[0] * bias_ref[...])


def _gated_long_conv(z, gate, hm, bias, tables, *, tn=4096, kb=4, interpret=False):
    B, L, C = z.shape
    R = math.isqrt(2 * L)
    assert R * R == 2 * L and R % (2 * kb) == 0
    first, fwd, inv, last = tables
    tn = min(tn, R * C)
    nt = (R * C) // tn
    z2 = z.reshape(B, R // 2, R * C)
    a = pl.pallas_call(
        _conv_first_body,
        out_shape=jax.ShapeDtypeStruct((B, 2 * R, R * C), BF16),
        grid=(B, nt),
        in_specs=[pl.BlockSpec((2 * R, R // 2), lambda b, j: (0, 0)),
                  pl.BlockSpec((1, R // 2, tn), lambda b, j: (b, 0, j))],
        out_specs=pl.BlockSpec((1, 2 * R, tn), lambda b, j: (b, 0, j)),
        name="long_conv_first",
        interpret=interpret,
    )(first, z2)
    bmid = pl.pallas_call(
        functools.partial(_conv_mid_body, R=R, kb=kb),
        out_shape=jax.ShapeDtypeStruct((B, 2, R, R, C), BF16),
        grid=(B, R // kb),
        in_specs=[pl.BlockSpec((1, 2, kb, R, C), lambda b, k: (b, 0, k, 0, 0)),
                  pl.BlockSpec((kb, 2 * R, 2 * R), lambda b, k: (k, 0, 0)),
                  pl.BlockSpec((kb, 2 * R, 2 * R), lambda b, k: (k, 0, 0)),
                  pl.BlockSpec((2, kb, R, C), lambda b, k: (0, k, 0, 0))],
        out_specs=pl.BlockSpec((1, 2, kb, R, C), lambda b, k: (b, 0, k, 0, 0)),
        name="long_conv_mid",
        interpret=interpret,
    )(a.reshape(B, 2, R, R, C), fwd, inv, hm)
    out = pl.pallas_call(
        _conv_last_body,
        out_shape=jax.ShapeDtypeStruct((B, R // 2, R * C), F32),
        grid=(B, nt),
        in_specs=[pl.BlockSpec((R // 2, 2 * R), lambda b, j: (0, 0)),
                  pl.BlockSpec((1, 2 * R, tn), lambda b, j: (b, 0, j)),
                  pl.BlockSpec((1, R // 2, tn), lambda b, j: (b, 0, j)),
                  pl.BlockSpec((1, R // 2, tn), lambda b, j: (b, 0, j)),
                  pl.BlockSpec((1, tn), lambda b, j: (0, j))],
        out_specs=pl.BlockSpec((1, R // 2, tn), lambda b, j: (b, 0, j)),
        name="long_conv_last",
        interpret=interpret,
    )(last, bmid.reshape(B, 2 * R, R * C), z2, gate.reshape(B, R // 2, R * C), jnp.tile(bias, R)[None])
    return out.reshape(B, L, C)


def _short_conv_body(u_ref, up_ref, un_ref, nk_ref, nv_ref, sw_ref, sb_ref, x1_ref, x2_ref, v_ref, ko_ref, vo_ref):
    i = pl.program_id(1)
    cur = u_ref[0]
    tb = cur.shape[0]
    prev_row = jnp.where(i == 0, 0.0, up_ref[0, HALO - 1:HALO, :])
    next_row = jnp.where(i == pl.num_programs(1) - 1, 0.0, un_ref[0, 0:1, :])
    rid = lax.broadcasted_iota(jnp.int32, cur.shape, 0)
    prev = jnp.where(rid == 0, prev_row, pltpu.roll(cur, 1, 0))
    nxt = jnp.where(rid == tb - 1, next_row, pltpu.roll(cur, tb - 1, 0))
    u = prev * sw_ref[0:1] + cur * sw_ref[1:2] + nxt * sw_ref[2:3] + sb_ref[...]
    x1_ref[0] = u[:, :D_HY]
    x2_ref[0] = u[:, D_HY:2 * D_HY]
    v_ref[0] = u[:, 2 * D_HY:]
    ko_ref[0] = nk_ref[0].astype(BF16)
    vo_ref[0] = nv_ref[0].astype(BF16)


def _split_projection(proj, sw, sb, *, tb=512):
    B, T, n_in = proj.shape
    width = 3 * D_HY
    col = (3 * D_RWKV + 3 * D_NA) // width
    assert col * width == 3 * D_RWKV + 3 * D_NA
    tb = min(tb, T)
    hb = tb // HALO
    out = jax.ShapeDtypeStruct((B, T, D_HY), F32)
    out_na = jax.ShapeDtypeStruct((B, T, D_NA), BF16)
    ospec = pl.BlockSpec((1, tb, D_HY), lambda b, i: (b, i, 0))
    na_spec = pl.BlockSpec((1, tb, D_NA), lambda b, i: (b, i, 0))
    na_col = 3 * D_RWKV // D_NA
    return pl.pallas_call(
        _short_conv_body,
        out_shape=[out, out, out, out_na, out_na],
        grid=(B, T // tb),
        in_specs=[pl.BlockSpec((1, tb, width), lambda b, i: (b, i, col)),
                  pl.BlockSpec((1, HALO, width), lambda b, i: (b, jnp.maximum(i * hb - 1, 0), col)),
                  pl.BlockSpec((1, HALO, width), lambda b, i: (b, jnp.minimum((i + 1) * hb, T // HALO - 1), col)),
                  pl.BlockSpec((1, tb, D_NA), lambda b, i: (b, i, na_col + 1)),
                  pl.BlockSpec((1, tb, D_NA), lambda b, i: (b, i, na_col + 2)),
                  pl.BlockSpec(sw.shape, lambda b, i: (0, 0)),
                  pl.BlockSpec((1, width), lambda b, i: (0, 0))],
        out_specs=[ospec, ospec, ospec, na_spec, na_spec],
        compiler_params=pltpu.CompilerParams(dimension_semantics=("parallel", "parallel")),
        name="split_projection",
    )(proj, proj, proj, proj, proj, sw, sb[None])


def _hyena_mixer(x1, x2, v, p):
    T = v.shape[1]
    R = math.isqrt(2 * T)
    spec = _hyena_filter_spectra(T, p) * (1.0 / (2 * T))
    spec = jnp.swapaxes(spec.reshape(R, R, HY_ORDER, D_HY), 0, 1)
    tables = _dft_tables(R)
    z = v
    for o, gate in enumerate((x1, x2)):
        hm = jnp.stack([jnp.real(spec[:, :, o]), jnp.imag(spec[:, :, o])])
        z = _gated_long_conv(z, gate, hm, p['hy_bias'][o], tables)
    return z


def _route(aff):
    n = aff.shape[0]
    cap = (CAPACITY_FACTOR * n) // N_EXPERTS
    return lax.top_k(aff.T, cap)


def _mix_body(x_ref, ya_ref, yb_ref, yc_ref, g_ref, wa_ref, wb_ref, wc_ref, wo_ref, lg_ref, lb_ref, wr_ref,
              o_ref, ob_ref, aff_ref):
    g = jax.nn.sigmoid(g_ref[...])
    m = (g[:, :D_MODEL] * jnp.dot(ya_ref[...].astype(BF16), wa_ref[...], preferred_element_type=F32)
         + g[:, D_MODEL:2 * D_MODEL] * jnp.dot(yb_ref[...].astype(BF16), wb_ref[...], preferred_element_type=F32)
         + g[:, 2 * D_MODEL:] * jnp.dot(yc_ref[...].astype(BF16), wc_ref[...], preferred_element_type=F32))
    h = ALPHA * x_ref[...] + jnp.dot(m.astype(BF16), wo_ref[...], preferred_element_type=F32)
    y = _layer_norm(h, lg_ref[...], lb_ref[...])
    o_ref[...] = y
    ob_ref[...] = y.astype(BF16)
    logits = jnp.dot(y, wr_ref[...], precision=lax.Precision.HIGHEST, preferred_element_type=F32)
    e = jnp.exp(logits - jnp.max(logits, -1, keepdims=True))
    aff_ref[...] = e / jnp.sum(e, -1, keepdims=True)


def _mix_and_norm(x, y_a, y_b, y_c, proj, p, *, tm=512):
    n, d = x.shape
    assert n % tm == 0 and proj.shape[1] == 2 * N_BRANCH * d
    row = lambda w: pl.BlockSpec((tm, w), lambda i: (i, 0))
    full = lambda a: pl.BlockSpec(a.shape, lambda i: (0,) * a.ndim)
    wa, wb, wc, wo = (p[k].astype(BF16) for k in ('w_branch_a', 'w_branch_b', 'w_branch_c', 'w_out'))
    lg, lb = p['ln1_g'][None], p['ln1_b'][None]
    return pl.pallas_call(
        _mix_body,
        out_shape=[jax.ShapeDtypeStruct((n, d), F32), jax.ShapeDtypeStruct((n, d), BF16),
                   jax.ShapeDtypeStruct((n, N_EXPERTS), F32)],
        grid=(n // tm,),
        in_specs=[row(d), row(D_RWKV), row(D_NA), row(D_HY),
                  pl.BlockSpec((tm, N_BRANCH * d), lambda i: (i, 1)),
                  full(wa), full(wb), full(wc), full(wo), full(lg), full(lb), full(p['w_router'])],
        out_specs=[row(d), row(d), row(N_EXPERTS)],
        compiler_params=pltpu.CompilerParams(dimension_semantics=("parallel",)),
        name="mix_norm_route",
    )(x, y_a, y_b, y_c, proj, wa, wb, wc, wo, lg, lb, p['w_router'])


def _mixers(x, p):
    B, T, D = x.shape
    n = B * T
    proj2 = _matmul(x.reshape(n, D), p['w_in'])
    proj = proj2.reshape(B, T, -1)
    x1c, x2c, vc, kb, vb = _split_projection(proj, p['hy_short_w'], p['hy_short_b'])
    y_a = _rwkv7_mixer(x, proj, p)
    y_b = _neighbourhood_attention_pallas(proj, kb, vb, p['na_rpb'])
    y_c = _hyena_mixer(x1c, x2c, vc, p)
    return _mix_and_norm(x.reshape(n, D), y_a.reshape(n, -1), y_b.reshape(n, -1), y_c.reshape(n, -1), proj2, p)


def _encoder_layer(xs, p, expert_weights, layer):
    mixed = [_mixers(x, p) for x in xs]
    routes = [_route(aff) for _, _, aff in mixed]
    xe = jnp.concatenate([jnp.take(xb, idx, axis=0) for (_, xb, _), (_, idx) in zip(mixed, routes)], axis=1)
    gate = jnp.concatenate([g for g, _ in routes], axis=1)[..., None]
    ye = _expert_ffn(xe, gate, *expert_weights, layer)
    outs, off = [], 0
    for x, (x1, _, _), (_, idx) in zip(xs, mixed, routes):
        cap = idx.shape[1]
        ffn = jnp.zeros_like(x1).at[idx.reshape(-1)].add(ye[:, off:off + cap].reshape(-1, x1.shape[-1]))
        off += cap
        outs.append(_ln(x1, p['ln2_g'], p['ln2_b'], residual=ffn).reshape(x.shape))
    return outs


def kernel(x_prompt, x_sample, ln_in_g, ln_in_b, w_in, rwkv_mu_rkv, rwkv_mu_x, rwkv_w0, rwkv_w1, rwkv_w2,
           rwkv_a0, rwkv_a1, rwkv_a2, rwkv_g1, rwkv_g2, rwkv_k_k, rwkv_k_a, rwkv_r_k, rwkv_lnx_g, rwkv_lnx_b,
           na_rpb, hy_short_w, hy_short_b, hy_w1, hy_b1, hy_w2, hy_b2, hy_w3, hy_b3, hy_w4, hy_freq, hy_bias,
           w_branch_a, w_branch_b, w_branch_c, w_out, ln1_g, ln1_b, w_router, w_exp_gate, w_exp_up,
           w_exp_down, ln2_g, ln2_b):
    stacked = {
        'w_in': w_in, 'rwkv_mu_rkv': rwkv_mu_rkv, 'rwkv_mu_x': rwkv_mu_x, 'rwkv_w0': rwkv_w0,
        'rwkv_w1': rwkv_w1, 'rwkv_w2': rwkv_w2, 'rwkv_a0': rwkv_a0, 'rwkv_a1': rwkv_a1, 'rwkv_a2': rwkv_a2,
        'rwkv_g1': rwkv_g1, 'rwkv_g2': rwkv_g2, 'rwkv_k_k': rwkv_k_k, 'rwkv_k_a': rwkv_k_a,
        'rwkv_r_k': rwkv_r_k, 'rwkv_lnx_g': rwkv_lnx_g, 'rwkv_lnx_b': rwkv_lnx_b, 'na_rpb': na_rpb,
        'hy_short_w': hy_short_w, 'hy_short_b': hy_short_b, 'hy_w1': hy_w1, 'hy_b1': hy_b1,
        'hy_w2': hy_w2, 'hy_b2': hy_b2, 'hy_w3': hy_w3, 'hy_b3': hy_b3, 'hy_w4': hy_w4,
        'hy_freq': hy_freq, 'hy_bias': hy_bias, 'w_branch_a': w_branch_a, 'w_branch_b': w_branch_b,
        'w_branch_c': w_branch_c, 'w_out': w_out, 'ln1_g': ln1_g, 'ln1_b': ln1_b, 'w_router': w_router,
        'w_exp_gate': w_exp_gate, 'w_exp_up': w_exp_up, 'w_exp_down': w_exp_down,
        'ln2_g': ln2_g, 'ln2_b': ln2_b,
    }
    xs = [_ln(x.reshape(-1, D_MODEL), ln_in_g, ln_in_b).reshape(x.shape) for x in (x_prompt, x_sample)]
    expert_weights = [stacked.pop(name) for name in ('w_exp_gate', 'w_exp_up', 'w_exp_down')]
    for l in range(DEPTH):
        xs = _encoder_layer(xs, {name: arr[l] for name, arr in stacked.items()}, expert_weights, l)
    return tuple(xs)
```

```python
import functools
import math

import jax
import jax.numpy as jnp
import numpy as np
from jax import lax
from jax.experimental import pallas as pl
from jax.experimental.pallas import tpu as pltpu

D_MODEL = 1024
DEPTH = 2
GRID_W = 64
HEAD_DIM = 64
D_RWKV = D_MODEL // 2
RWKV_HEADS = D_RWKV // HEAD_DIM
D_NA = D_MODEL // 4
NA_HEADS = D_NA // HEAD_DIM
D_HY = D_MODEL // 4
N_BRANCH = 3
GN_EPS = 64e-5
NA_WIN_ROWS = 8
NA_WIN_COLS = 16
HY_ORDER = 2
HY_EMB = 33
HY_TOL = 1e-2
HY_FAST_PCT = 0.3
HY_SLOW_PCT = 1.5
N_EXPERTS = 16
CAPACITY_FACTOR = 2
ALPHA = (2 * DEPTH) ** 0.25
LN_EPS = 1e-5

F32 = jnp.float32
BF16 = jnp.bfloat16
EXPERT_FFN_VMEM_BYTES = 52 * 1024 * 1024


def _mm_body(x_ref, w_ref, o_ref):
    o_ref[...] = jnp.dot(x_ref[...].astype(BF16), w_ref[...].astype(BF16),
                         preferred_element_type=F32).astype(o_ref.dtype)


def _matmul(x, w, *, tm=1024, tn=512, out_dtype=F32):
    m, k = x.shape
    _, n = w.shape
    tm = min(tm, m)
    tn = min(tn, n)
    assert m % tm == 0 and n % tn == 0
    return pl.pallas_call(
        _mm_body,
        out_shape=jax.ShapeDtypeStruct((m, n), out_dtype),
        grid=(m // tm, n // tn),
        in_specs=[pl.BlockSpec((tm, k), lambda i, j: (i, 0)),
                  pl.BlockSpec((k, tn), lambda i, j: (0, j))],
        out_specs=pl.BlockSpec((tm, tn), lambda i, j: (i, j)),
        name="dense_matmul",
    )(x, w)


def _expert_ffn_body(x_ref, wg_ref, wu_ref, wd_ref, gate_ref, o_ref, acc_ref):
    f = pl.program_id(2)

    @pl.when(f == 0)
    def _():
        acc_ref[...] = jnp.zeros_like(acc_ref)

    xb = x_ref[0]
    hg = jnp.dot(xb, wg_ref[0, 0].astype(BF16), preferred_element_type=F32)
    hu = jnp.dot(xb, wu_ref[0, 0].astype(BF16), preferred_element_type=F32)
    h = (hg * jax.nn.sigmoid(hg) * hu).astype(BF16)
    acc_ref[...] += jnp.dot(h, wd_ref[0, 0].astype(BF16), preferred_element_type=F32)

    @pl.when(f == pl.num_programs(2) - 1)
    def _():
        o_ref[0] = acc_ref[...] * gate_ref[0]


def _expert_ffn(xe, gate, w_gate, w_up, w_down, layer, *, tm=2048, tf=256):
    e, m, d = xe.shape
    ff = w_gate.shape[-1]
    tm = min(tm, m)
    assert m % tm == 0 and ff % tf == 0
    return pl.pallas_call(
        _expert_ffn_body,
        out_shape=jax.ShapeDtypeStruct((e, m, d), F32),
        grid=(e, m // tm, ff // tf),
        in_specs=[pl.BlockSpec((1, tm, d), lambda g, i, f: (g, i, 0)),
                  pl.BlockSpec((1, 1, d, tf), lambda g, i, f: (layer, g, 0, f)),
                  pl.BlockSpec((1, 1, d, tf), lambda g, i, f: (layer, g, 0, f)),
                  pl.BlockSpec((1, 1, tf, d), lambda g, i, f: (layer, g, f, 0)),
                  pl.BlockSpec((1, tm, 1), lambda g, i, f: (g, i, 0))],
        out_specs=pl.BlockSpec((1, tm, d), lambda g, i, f: (g, i, 0)),
        scratch_shapes=[pltpu.VMEM((tm, d), F32)],
        compiler_params=pltpu.CompilerParams(
            dimension_semantics=("parallel", "parallel", "arbitrary"), vmem_limit_bytes=EXPERT_FFN_VMEM_BYTES),
        name="expert_ffn",
    )(xe, w_gate, w_up, w_down, gate)


def _layer_norm(h, g, b):
    hc = h - jnp.mean(h, -1, keepdims=True)
    var = jnp.mean(hc * hc, -1, keepdims=True)
    return hc * lax.rsqrt(var + LN_EPS) * g + b


def _ln_body(x_ref, g_ref, b_ref, o_ref):
    o_ref[...] = _layer_norm(x_ref[...], g_ref[...], b_ref[...])


def _ln_residual_body(x_ref, r_ref, g_ref, b_ref, o_ref):
    o_ref[...] = _layer_norm(ALPHA * x_ref[...] + r_ref[...], g_ref[...], b_ref[...])


def _ln(x, g, b, residual=None, *, tm=1024):
    n, d = x.shape
    tm = min(tm, n)
    assert n % tm == 0
    row = pl.BlockSpec((tm, d), lambda i: (i, 0))
    vec = pl.BlockSpec((1, d), lambda i: (0, 0))
    args = (x,) if residual is None else (x, residual)
    return pl.pallas_call(
        _ln_body if residual is None else _ln_residual_body,
        out_shape=jax.ShapeDtypeStruct((n, d), F32),
        grid=(n // tm,),
        in_specs=[row] * len(args) + [vec, vec],
        out_specs=row,
        compiler_params=pltpu.CompilerParams(dimension_semantics=("parallel",)),
        name="layer_norm",
    )(*args, g[None], b[None])


RWKV_SUB = 16
RWKV_PAIRS = 2
LANES = 128
RWKV_PIECES = 3
RWKV_GAMMA_SLOT = 2 * RWKV_PIECES * RWKV_SUB


def _rwkv_selectors():
    z = np.zeros((RWKV_SUB + 1, LANES, LANES), np.float32)
    for h in range(LANES // HEAD_DIM):
        lanes = slice(h * HEAD_DIM, (h + 1) * HEAD_DIM)
        for p in range(RWKV_PIECES):
            for t in range(RWKV_SUB):
                z[t, (h * RWKV_PIECES + p) * RWKV_SUB + t, lanes] = 1.0
            z[RWKV_SUB, RWKV_GAMMA_SLOT + 8 * h + p, lanes] = 1.0
    return jnp.asarray(z, BF16)


def _split3(x):
    hi = x.astype(BF16).astype(F32)
    rem = x - hi
    mid = rem.astype(BF16).astype(F32)
    lo = (rem - mid).astype(BF16).astype(F32)
    return [hi, mid, lo]


def _operand_tile(xa, xb, extra):
    pa = jnp.concatenate(_split3(xa), axis=0)
    pb = pa if xb is None else jnp.concatenate(_split3(xb), axis=0)
    lane_lo = lax.broadcasted_iota(jnp.int32, pa.shape, 1) < HEAD_DIM
    head0 = jnp.where(lane_lo, pa, pltpu.roll(pb, HEAD_DIM, 1))
    head1 = jnp.where(lane_lo, pltpu.roll(pa, HEAD_DIM, 1), pb)
    tt = jnp.concatenate([head0, head1, extra], axis=0).T
    return tt[:HEAD_DIM].astype(BF16), tt[HEAD_DIM:].astype(BF16)


def _rwkv_prepare(refs, base, lanes, reverse):
    a_ref, b_ref, k_ref, r_ref, g_ref = refs
    rows = pl.ds(base, RWKV_SUB)
    last = 0 if reverse else RWKV_SUB - 1
    g_last = g_ref[0, 0, rows, lanes][last:last + 1]
    gam = jnp.concatenate(_split3(g_last) + [jnp.zeros((8 - RWKV_PIECES, LANES), F32)], axis=0)
    extra = jnp.concatenate([gam, pltpu.roll(gam, HEAD_DIM, 1), jnp.zeros((16, LANES), F32)], axis=0)
    g_a, g_b = _operand_tile(a_ref[0, 0, rows, lanes], b_ref[0, 0, rows, lanes], extra)
    g_k, _ = _operand_tile(k_ref[0, 0, rows, lanes], None, jnp.zeros((32, LANES), F32))
    rg = r_ref[0, 0, rows, lanes]
    pad = jnp.zeros((HEAD_DIM - RWKV_SUB, LANES), F32)
    r_t = jnp.concatenate([rg, pad, pltpu.roll(rg, HEAD_DIM, 1), pad], axis=0).T[:HEAD_DIM]
    return jnp.concatenate([g_a, g_b, g_k], axis=0), r_t


def _rwkv_steps(streams, z_ref, s_ref):
    head_lane = (lax.broadcasted_iota(jnp.int32, (HEAD_DIM, LANES), 1) // HEAD_DIM) * HEAD_DIM
    lanes = [slice(q * LANES, (q + 1) * LANES) for _, _, _, _, _, q, _, _ in streams]
    vs = [s[2][0, pl.ds(s[6], RWKV_SUB), ln] for s, ln in zip(streams, lanes)]
    sts = [s_ref[d, q] for _, _, _, _, d, q, _, _ in streams]
    ys = [[None] * RWKV_SUB for _ in streams]
    gammas = [jnp.dot(s[0][:HEAD_DIM], z_ref[RWKV_SUB], preferred_element_type=F32) for s in streams]
    for p in range(RWKV_SUB // 2):
        cols = []
        for g_all, _, _, _, _, _, _, reverse in streams:
            ts = (RWKV_SUB - 1 - 2 * p, RWKV_SUB - 2 - 2 * p) if reverse else (2 * p, 2 * p + 1)
            sel = jnp.concatenate([z_ref[ts[0]], z_ref[ts[1]]], axis=1)
            cols.append((ts, jnp.dot(g_all, sel, preferred_element_type=F32)))
        for n in range(2):
            for i, (_, r_t, _, _, _, _, _, _) in enumerate(streams):
                ts, c = cols[i]
                t = ts[n]
                a_c, b_c, k_c = (c[o * HEAD_DIM:(o + 1) * HEAD_DIM, n * LANES:(n + 1) * LANES] for o in range(3))
                r_c = jnp.take_along_axis(r_t, head_lane + t, axis=1)
                sa = jnp.sum(sts[i] * a_c, axis=0, keepdims=True)
                sts[i] = sts[i] + b_c * sa + k_c * vs[i][t:t + 1, :]
                ys[i][t] = jnp.sum(sts[i] * r_c, axis=0, keepdims=True)
    for i, (_, _, _, y_ref, d, q, base, _) in enumerate(streams):
        s_ref[d, q] = sts[i] * gammas[i]
        y_ref[0, 0, pl.ds(base, RWKV_SUB), lanes[i]] = jnp.concatenate(ys[i], axis=0)


def _rwkv_body(af, bf, kf, rf, gf, vf, ab, bb, kb, rb, gb, vb, z_ref, yf_ref, yb_ref, s_ref, g_ref,
               rt_ref, *, tb):
    @pl.when(pl.program_id(2) == 0)
    def _():
        s_ref[...] = jnp.zeros_like(s_ref)

    nsub = tb // RWKV_SUB
    refs_f = (af, bf, kf, rf, gf)
    refs_b = (ab, bb, kb, rb, gb)

    def base_f(c):
        return pl.multiple_of(c * RWKV_SUB, RWKV_SUB)

    def base_b(c):
        return pl.multiple_of((nsub - 1 - c) * RWKV_SUB, RWKV_SUB)

    def prepare(slot, c):
        for q in range(RWKV_PAIRS):
            lanes = slice(q * LANES, (q + 1) * LANES)
            g_ref[slot, 0, q], rt_ref[slot, 0, q] = _rwkv_prepare(refs_f, base_f(c), lanes, False)
            g_ref[slot, 1, q], rt_ref[slot, 1, q] = _rwkv_prepare(refs_b, base_b(c), lanes, True)

    prepare(0, 0)

    def sub(c, carry):
        slot = c % 2
        nxt = jnp.minimum(c + 1, nsub - 1)
        streams = []
        for q in range(RWKV_PAIRS):
            streams.append((g_ref[slot, 0, q], rt_ref[slot, 0, q], vf, yf_ref, 0, q, base_f(c), False))
            streams.append((g_ref[slot, 1, q], rt_ref[slot, 1, q], vb, yb_ref, 1, q, base_b(c), True))
        prepare(1 - slot, nxt)
        _rwkv_steps(streams, z_ref, s_ref)
        return carry

    lax.fori_loop(0, nsub, sub, 0)


def _rwkv7_scan(a, b, k, r, g, v, *, tb=512, interpret=False):
    B, T, C = v.shape
    tb = min(tb, T)
    nt = T // tb
    width = RWKV_PAIRS * LANES
    assert T % tb == 0 and tb % RWKV_SUB == 0 and C % width == 0
    fwd3 = pl.BlockSpec((1, tb, width), lambda b, j, i: (b, i, j))
    bwd3 = pl.BlockSpec((1, tb, width), lambda b, j, i: (b, nt - 1 - i, j))
    fwd4 = pl.BlockSpec((1, 1, tb, width), lambda b, j, i: (0, b, i, j))
    bwd4 = pl.BlockSpec((1, 1, tb, width), lambda b, j, i: (1, b, nt - 1 - i, j))
    yf, yb = pl.pallas_call(
        functools.partial(_rwkv_body, tb=tb),
        out_shape=[jax.ShapeDtypeStruct((1, B, T, C), F32)] * 2,
        grid=(B, C // width, nt),
        in_specs=[fwd4] * 5 + [fwd3] + [bwd4] * 5 + [bwd3]
        + [pl.BlockSpec((RWKV_SUB + 1, LANES, LANES), lambda b, j, i: (0, 0, 0))],
        out_specs=[pl.BlockSpec((1, 1, tb, width), lambda b, j, i: (0, b, i, j)),
                   pl.BlockSpec((1, 1, tb, width), lambda b, j, i: (0, b, nt - 1 - i, j))],
        scratch_shapes=[pltpu.VMEM((2, RWKV_PAIRS, HEAD_DIM, LANES), F32),
                        pltpu.VMEM((2, 2, RWKV_PAIRS, 3 * HEAD_DIM, LANES), BF16),
                        pltpu.VMEM((2, 2, RWKV_PAIRS, HEAD_DIM, LANES), F32)],
        compiler_params=pltpu.CompilerParams(dimension_semantics=("parallel", "parallel", "arbitrary")),
        name="rwkv7_scan",
        interpret=interpret,
    )(a, b, k, r, g, v, a, b, k, r, g, v, _rwkv_selectors())
    return yf, yb


HALO = 8


def _head_sums(x, bd_ref):
    bd = bd_ref[...]
    return sum(jnp.dot(piece.astype(BF16), bd, preferred_element_type=F32) for piece in _split3(x))


def _token_shift(cur, prev_row, next_row, mu):
    tb = cur.shape[0]
    rid = lax.broadcasted_iota(jnp.int32, cur.shape, 0)
    prev = jnp.where(rid == 0, prev_row, pltpu.roll(cur, 1, 0))
    nxt = jnp.where(rid == tb - 1, next_row, pltpu.roll(cur, tb - 1, 0))
    return cur + mu[0:1] * (prev - cur) + mu[1:2] * (nxt - cur)


def _rwkv_prep_body(x_ref, xp_ref, xn_ref, r_ref, rp_ref, rn_ref, k_ref, kp_ref, kn_ref, v_ref, vp_ref, vn_ref,
                    mux_ref, mur_ref, w1_ref, w2_ref, w0_ref, a1_ref, a2_ref, a0_ref, g1_ref, g2_ref,
                    kk_w_ref, ka_w_ref, bd_ref, tri_ref,
                    ro_ref, vo_ref, go_ref, kd_ref, sa_ref, sb_ref, sk_ref, sr_ref, sg_ref):
    i = pl.program_id(1)
    first = i == 0
    last = i == pl.num_programs(1) - 1

    def shifted(cur_ref, p_ref, n_ref, mu):
        prev_row = jnp.where(first, 0.0, p_ref[0, HALO - 1:HALO, :])
        next_row = jnp.where(last, 0.0, n_ref[0, 0:1, :])
        return _token_shift(cur_ref[0], prev_row, next_row, mu)

    r = shifted(r_ref, rp_ref, rn_ref, mur_ref[0])
    k = shifted(k_ref, kp_ref, kn_ref, mur_ref[1])
    v = shifted(v_ref, vp_ref, vn_ref, mur_ref[2])
    xw = shifted(x_ref, xp_ref, xn_ref, mux_ref[0]).astype(BF16)
    xa = shifted(x_ref, xp_ref, xn_ref, mux_ref[1]).astype(BF16)
    xg = shifted(x_ref, xp_ref, xn_ref, mux_ref[2]).astype(BF16)
    hg = jax.nn.sigmoid(jnp.dot(xg, g1_ref[...], preferred_element_type=F32)).astype(BF16)
    go_ref[0] = jnp.dot(hg, g2_ref[...], preferred_element_type=F32)
    kk = k * kk_w_ref[...]
    kk = kk / jnp.maximum(jnp.sqrt(_head_sums(kk * kk, bd_ref)), 1e-12)
    ro_ref[0] = r
    vo_ref[0] = v
    for z in range(2):
        hw = jnp.tanh(jnp.dot(xw, w1_ref[z], preferred_element_type=F32)).astype(BF16)
        ha = jnp.dot(xa, a1_ref[z], preferred_element_type=F32).astype(BF16)
        w_raw = w0_ref[z:z + 1] + jnp.dot(hw, w2_ref[z], preferred_element_type=F32)
        lw = -math.exp(-0.5) * jax.nn.sigmoid(w_raw)
        a = jax.nn.sigmoid(a0_ref[z:z + 1]
                           + jnp.dot(ha, a2_ref[z], preferred_element_type=F32))
        kd = k * (1.0 + (a - 1.0) * ka_w_ref[...])
        kd_ref[z, 0] = kd
        cum = sum(jnp.dot(tri_ref[z], piece.astype(BF16), preferred_element_type=F32) for piece in _split3(lw))
        gamma = jnp.exp(cum)
        inv_gamma = jnp.exp(-cum)
        sa_ref[z, 0] = -kk * jnp.exp(cum - lw)
        sb_ref[z, 0] = kk * a * inv_gamma
        sk_ref[z, 0] = kd * inv_gamma
        sr_ref[z, 0] = r * gamma
        sg_ref[z, 0] = gamma


def _rwkv_post_body(yf_ref, yb_ref, r_ref, v_ref, kd_ref, g_ref, lng_ref, lnb_ref, rk_ref, bd_ref, o_ref):
    y = yf_ref[0, 0] + yb_ref[0, 0]
    inv_n = 1.0 / HEAD_DIM
    yc = y - _head_sums(y, bd_ref) * inv_n
    yn = yc * lax.rsqrt(_head_sums(yc * yc, bd_ref) * inv_n + GN_EPS) * lng_ref[...] + lnb_ref[...]
    r, v = r_ref[0], v_ref[0]
    bonus = (_head_sums(r * kd_ref[0, 0] * rk_ref[...], bd_ref)
             + _head_sums(r * kd_ref[1, 0] * rk_ref[...], bd_ref)) * v
    o_ref[0] = ((yn + bonus) * g_ref[0]).astype(o_ref.dtype)


def _rwkv7_mixer(u, proj, p, *, tb=256):
    B, T, D = u.shape
    C = D_RWKV
    tb = min(tb, T)
    nt = T // tb
    assert T % tb == 0 and tb % HALO == 0
    hb = tb // HALO

    def cur(width, col):
        return pl.BlockSpec((1, tb, width), lambda b, i: (b, i, col))

    def prev(width, col):
        return pl.BlockSpec((1, HALO, width), lambda b, i: (b, jnp.maximum(i * hb - 1, 0), col))

    def nxt(width, col):
        return pl.BlockSpec((1, HALO, width), lambda b, i: (b, jnp.minimum((i + 1) * hb, T // HALO - 1), col))

    def full(a):
        return pl.BlockSpec(a.shape, lambda b, i: (0,) * a.ndim)

    lane = np.arange(C) // HEAD_DIM
    bd = jnp.asarray(lane[:, None] == lane[None, :], BF16)
    t_id = np.arange(tb)
    same = (t_id[:, None] // RWKV_SUB) == (t_id[None, :] // RWKV_SUB)
    tri = jnp.asarray(np.stack([same & (t_id[None, :] <= t_id[:, None]),
                                same & (t_id[None, :] >= t_id[:, None])]), BF16)
    weights = [p['rwkv_mu_x'], p['rwkv_mu_rkv'], p['rwkv_w1'].astype(BF16), p['rwkv_w2'].astype(BF16), p['rwkv_w0'],
               p['rwkv_a1'].astype(BF16), p['rwkv_a2'].astype(BF16), p['rwkv_a0'], p['rwkv_g1'].astype(BF16), p['rwkv_g2'].astype(BF16),
               p['rwkv_k_k'][None], p['rwkv_k_a'][None], bd, tri]
    one = jax.ShapeDtypeStruct((B, T, C), F32)
    two = jax.ShapeDtypeStruct((2, B, T, C), F32)
    out1 = pl.BlockSpec((1, tb, C), lambda b, i: (b, i, 0))
    out2 = pl.BlockSpec((2, 1, tb, C), lambda b, i: (0, b, i, 0))
    r, v, g, kd, sa, sb, sk, sr, sg = pl.pallas_call(
        _rwkv_prep_body,
        out_shape=[one, one, one, two, two, two, two, two, two],
        grid=(B, nt),
        in_specs=[cur(D, 0), prev(D, 0), nxt(D, 0)]
        + [spec(C, col) for col in range(3) for spec in (cur, prev, nxt)]
        + [full(w) for w in weights],
        out_specs=[out1, out1, out1, out2, out2, out2, out2, out2, out2],
        compiler_params=pltpu.CompilerParams(dimension_semantics=("parallel", "parallel")),
        name="rwkv7_prepare",
    )(u, u, u, *([proj] * 9), *weights)
    yf, yb = _rwkv7_scan(sa, sb, sk, sr, sg, v)
    ydir = pl.BlockSpec((1, 1, tb, C), lambda b, i: (0, b, i, 0))
    post_w = [p['rwkv_lnx_g'][None], p['rwkv_lnx_b'][None], p['rwkv_r_k'].reshape(1, C), bd]
    return pl.pallas_call(
        _rwkv_post_body,
        out_shape=jax.ShapeDtypeStruct((B, T, C), BF16),
        grid=(B, nt),
        in_specs=[ydir, ydir, out1, out1, out2, out1] + [full(w) for w in post_w],
        out_specs=out1,
        compiler_params=pltpu.CompilerParams(dimension_semantics=("parallel", "parallel")),
        name="rwkv7_output",
    )(yf, yb, r, v, kd, g, *post_w)


NA_ROWS_PER_STEP = 8


def _na_body(q_ref, k_ref, v_ref, bias_ref, mask_ref, o_ref, *, rows):
    kr = NA_WIN_ROWS
    win = kr * GRID_W
    valid = mask_ref[...] != 0
    lane = lax.broadcasted_iota(jnp.int32, (GRID_W, LANES), 1)
    for ii in range(NA_ROWS_PER_STEP):
        i = pl.program_id(1) * NA_ROWS_PER_STEP + ii
        start = jnp.clip(i - kr // 2, 0, rows - kr)
        d = i - start
        krows = pl.ds(pl.multiple_of(start * GRID_W, GRID_W), win)
        for pair in range(D_NA // LANES):
            lanes = slice(pair * LANES, (pair + 1) * LANES)
            q2 = q_ref[0, ii * GRID_W:(ii + 1) * GRID_W, lanes] * (HEAD_DIM ** -0.5)
            k2 = k_ref[0, krows, lanes]
            v2 = v_ref[0, krows, lanes]
            outs = []
            for hh in range(LANES // HEAD_DIM):
                in_head = (lane >= hh * HEAD_DIM) & (lane < (hh + 1) * HEAD_DIM)
                qh = jnp.where(in_head, q2, 0.0).astype(BF16)
                s = lax.dot_general(qh, k2, (((1,), (1,)), ((), ())), preferred_element_type=F32)
                s = s + bias_ref[d, pair * (LANES // HEAD_DIM) + hh]
                s = jnp.where(valid, s, -1e30)
                m = jnp.max(s, axis=-1, keepdims=True)
                e = jnp.exp(s - m)
                p = e / jnp.sum(e, axis=-1, keepdims=True)
                outs.append(jnp.dot(p.astype(BF16), v2, preferred_element_type=F32))
            o_ref[0, ii * GRID_W:(ii + 1) * GRID_W, lanes] = jnp.where(lane < HEAD_DIM, outs[0], outs[1])


def _neighbourhood_attention_pallas(proj, k, v, rpb, *, interpret=False):
    B, T, _ = proj.shape
    q_col = 3 * D_RWKV // D_NA
    rows = T // GRID_W
    kr, kc = NA_WIN_ROWS, NA_WIN_COLS
    assert rows >= kr and rows % NA_ROWS_PER_STEP == 0
    ci = jnp.arange(GRID_W)
    col_start = jnp.clip(ci - kc // 2, 0, GRID_W - kc)
    col_valid = (ci[None] >= col_start[:, None]) & (ci[None] < col_start[:, None] + kc)
    mask = jnp.tile(col_valid.astype(jnp.int32), (1, kr))
    dc_idx = jnp.clip(ci[None] - ci[:, None] + kc - 1, 0, 2 * kc - 2)
    dr_idx = jnp.arange(kr)[None, :] - jnp.arange(kr)[:, None] + kr - 1
    onehot = (jnp.arange(2 * kc - 1)[:, None] == dc_idx.reshape(1, -1)).astype(F32)
    bias = jnp.dot(rpb[:, dr_idx].reshape(-1, 2 * kc - 1), onehot, precision=lax.Precision.HIGHEST)
    bias = bias.reshape(NA_HEADS, kr, kr, GRID_W, GRID_W)
    bias = jnp.transpose(bias, (1, 0, 3, 2, 4)).reshape(kr, NA_HEADS, GRID_W, kr * GRID_W)
    tq = NA_ROWS_PER_STEP * GRID_W
    return pl.pallas_call(
        functools.partial(_na_body, rows=rows),
        out_shape=jax.ShapeDtypeStruct((B, T, D_NA), F32),
        grid=(B, rows // NA_ROWS_PER_STEP),
        in_specs=[pl.BlockSpec((1, tq, D_NA), lambda b, i: (b, i, q_col)),
                  pl.BlockSpec((1, T, D_NA), lambda b, i: (b, 0, 0)),
                  pl.BlockSpec((1, T, D_NA), lambda b, i: (b, 0, 0)),
                  pl.BlockSpec((kr, NA_HEADS, GRID_W, kr * GRID_W), lambda b, i: (0, 0, 0, 0)),
                  pl.BlockSpec((GRID_W, kr * GRID_W), lambda b, i: (0, 0))],
        out_specs=pl.BlockSpec((1, tq, D_NA), lambda b, i: (b, i, 0)),
        compiler_params=pltpu.CompilerParams(dimension_semantics=("parallel", "arbitrary")),
        name="neighbourhood_attention",
        interpret=interpret,
    )(proj, k, v, bias, mask)


def _hyena_filter_spectra(L, p):
    t = jnp.linspace(0.0, 1.0, L, dtype=F32)[:, None]
    n_bands = (HY_EMB - 1) // 2
    omega = 2.0 * math.pi * jnp.arange(L, dtype=F32)[:, None] / L
    bands = jnp.linspace(1e-4, n_bands - 1, n_bands, dtype=F32)[None]
    z = jnp.concatenate([t, jnp.cos(bands * omega), -jnp.sin(bands * omega)], -1)
    freq = p['hy_freq']
    h = jnp.sin(freq * (z @ p['hy_w1'] + p['hy_b1']))
    h = jnp.sin(freq * (h @ p['hy_w2'] + p['hy_b2']))
    h = jnp.sin(freq * (h @ p['hy_w3'] + p['hy_b3']))
    h = (h @ p['hy_w4']).reshape(L, HY_ORDER, 2, D_HY)
    deltas = jnp.abs(jnp.linspace(math.log(HY_TOL) / HY_FAST_PCT, math.log(HY_TOL) / HY_SLOW_PCT, D_HY, dtype=F32))
    h = h * jnp.exp(-t * deltas)[:, None, None, :]
    h_f, h_b = h[:, :, 0], h[:, :, 1]
    kern = jnp.concatenate([h_f, jnp.zeros((1, HY_ORDER, D_HY), F32), h_b[1:][::-1]], 0)
    kern = kern / jnp.sum(jnp.abs(kern), 0, keepdims=True)
    return jnp.fft.fft(kern, axis=0)


def _dft_tables(R):
    n = np.arange(R)
    ang = (2.0 * np.pi / R) * ((n[:, None] * n[None, :]) % R)
    tw = (2.0 * np.pi / (R * R)) * (n[:, None] * n[None, :])
    c, s, ct, st = (jnp.asarray(f(a), F32) for a in (ang, tw) for f in (np.cos, np.sin))
    first = jnp.concatenate([c, -s], axis=0)[:, :R // 2]
    last = jnp.concatenate([c, -s], axis=1)[:R // 2]
    cp = c[None] * ct[:, None, :] - s[None] * st[:, None, :]
    sp = s[None] * ct[:, None, :] + c[None] * st[:, None, :]
    fwd = jnp.concatenate([jnp.concatenate([cp, sp], axis=2), jnp.concatenate([-sp, cp], axis=2)], axis=1)
    cpt = c[None] * ct[:, :, None] - s[None] * st[:, :, None]
    spt = s[None] * ct[:, :, None] + c[None] * st[:, :, None]
    inv = jnp.concatenate([jnp.concatenate([cpt, -spt], axis=2), jnp.concatenate([spt, cpt], axis=2)], axis=1)
    return first.astype(BF16), fwd.astype(BF16), inv.astype(BF16), last.astype(BF16)


def _conv_first_body(f_ref, z_ref, o_ref):
    o_ref[0] = jnp.dot(f_ref[...], z_ref[0].astype(BF16), preferred_element_type=F32).astype(o_ref.dtype)


def _conv_mid_body(a_ref, mf_ref, mi_ref, h_ref, o_ref, *, R, kb):
    for j in range(kb):
        a = jnp.concatenate([a_ref[0, 0, j], a_ref[0, 1, j]], axis=0)
        x = jnp.dot(mf_ref[j], a, preferred_element_type=F32)
        xr, xi = x[:R], x[R:]
        hr, hi = h_ref[0, j], h_ref[1, j]
        y = jnp.concatenate([xr * hr - xi * hi, xr * hi + xi * hr], axis=0).astype(BF16)
        b = jnp.dot(mi_ref[j], y, preferred_element_type=F32)
        o_ref[0, 0, j] = b[:R].astype(o_ref.dtype)
        o_ref[0, 1, j] = b[R:].astype(o_ref.dtype)


def _conv_last_body(f_ref, b_ref, z_ref, gate_ref, bias_ref, o_ref):
    y = jnp.dot(f_ref[...], b_ref[0], preferred_element_type=F32)
    o_ref[0] = gate_ref[0] * (y + z_ref[0] * bias_ref[...])


def _gated_long_conv(z, gate, hm, bias, tables, *, tn=4096, kb=4, interpret=False):
    B, L, C = z.shape
    R = math.isqrt(2 * L)
    assert R * R == 2 * L and R % (2 * kb) == 0
    first, fwd, inv, last = tables
    tn = min(tn, R * C)
    nt = (R * C) // tn
    z2 = z.reshape(B, R // 2, R * C)
    a = pl.pallas_call(
        _conv_first_body,
        out_shape=jax.ShapeDtypeStruct((B, 2 * R, R * C), BF16),
        grid=(B, nt),
        in_specs=[pl.BlockSpec((2 * R, R // 2), lambda b, j: (0, 0)),
                  pl.BlockSpec((1, R // 2, tn), lambda b, j: (b, 0, j))],
        out_specs=pl.BlockSpec((1, 2 * R, tn), lambda b, j: (b, 0, j)),
        name="long_conv_first",
        interpret=interpret,
    )(first, z2)
    bmid = pl.pallas_call(
        functools.partial(_conv_mid_body, R=R, kb=kb),
        out_shape=jax.ShapeDtypeStruct((B, 2, R, R, C), BF16),
        grid=(B, R // kb),
        in_specs=[pl.BlockSpec((1, 2, kb, R, C), lambda b, k: (b, 0, k, 0, 0)),
                  pl.BlockSpec((kb, 2 * R, 2 * R), lambda b, k: (k, 0, 0)),
                  pl.BlockSpec((kb, 2 * R, 2 * R), lambda b, k: (k, 0, 0)),
                  pl.BlockSpec((2, kb, R, C), lambda b, k: (0, k, 0, 0))],
        out_specs=pl.BlockSpec((1, 2, kb, R, C), lambda b, k: (b, 0, k, 0, 0)),
        name="long_conv_mid",
        interpret=interpret,
    )(a.reshape(B, 2, R, R, C), fwd, inv, hm)
    out = pl.pallas_call(
        _conv_last_body,
        out_shape=jax.ShapeDtypeStruct((B, R // 2, R * C), F32),
        grid=(B, nt),
        in_specs=[pl.BlockSpec((R // 2, 2 * R), lambda b, j: (0, 0)),
                  pl.BlockSpec((1, 2 * R, tn), lambda b, j: (b, 0, j)),
                  pl.BlockSpec((1, R // 2, tn), lambda b, j: (b, 0, j)),
                  pl.BlockSpec((1, R // 2, tn), lambda b, j: (b, 0, j)),
                  pl.BlockSpec((1, tn), lambda b, j: (0, j))],
        out_specs=pl.BlockSpec((1, R // 2, tn), lambda b, j: (b, 0, j)),
        name="long_conv_last",
        interpret=interpret,
    )(last, bmid.reshape(B, 2 * R, R * C), z2, gate.reshape(B, R // 2, R * C), jnp.tile(bias, R)[None])
    return out.reshape(B, L, C)


def _short_conv_body(u_ref, up_ref, un_ref, nk_ref, nv_ref, sw_ref, sb_ref, x1_ref, x2_ref, v_ref, ko_ref, vo_ref):
    i = pl.program_id(1)
    cur = u_ref[0]
    tb = cur.shape[0]
    prev_row = jnp.where(i == 0, 0.0, up_ref[0, HALO - 1:HALO, :])
    next_row = jnp.where(i == pl.num_programs(1) - 1, 0.0, un_ref[0, 0:1, :])
    rid = lax.broadcasted_iota(jnp.int32, cur.shape, 0)
    prev = jnp.where(rid == 0, prev_row, pltpu.roll(cur, 1, 0))
    nxt = jnp.where(rid == tb - 1, next_row, pltpu.roll(cur, tb - 1, 0))
    u = prev * sw_ref[0:1] + cur * sw_ref[1:2] + nxt * sw_ref[2:3] + sb_ref[...]
    x1_ref[0] = u[:, :D_HY]
    x2_ref[0] = u[:, D_HY:2 * D_HY]
    v_ref[0] = u[:, 2 * D_HY:]
    ko_ref[0] = nk_ref[0].astype(BF16)
    vo_ref[0] = nv_ref[0].astype(BF16)


def _split_projection(proj, sw, sb, *, tb=512):
    B, T, n_in = proj.shape
    width = 3 * D_HY
    col = (3 * D_RWKV + 3 * D_NA) // width
    assert col * width == 3 * D_RWKV + 3 * D_NA
    tb = min(tb, T)
    hb = tb // HALO
    out = jax.ShapeDtypeStruct((B, T, D_HY), F32)
    out_na = jax.ShapeDtypeStruct((B, T, D_NA), BF16)
    ospec = pl.BlockSpec((1, tb, D_HY), lambda b, i: (b, i, 0))
    na_spec = pl.BlockSpec((1, tb, D_NA), lambda b, i: (b, i, 0))
    na_col = 3 * D_RWKV // D_NA
    return pl.pallas_call(
        _short_conv_body,
        out_shape=[out, out, out, out_na, out_na],
        grid=(B, T // tb),
        in_specs=[pl.BlockSpec((1, tb, width), lambda b, i: (b, i, col)),
                  pl.BlockSpec((1, HALO, width), lambda b, i: (b, jnp.maximum(i * hb - 1, 0), col)),
                  pl.BlockSpec((1, HALO, width), lambda b, i: (b, jnp.minimum((i + 1) * hb, T // HALO - 1), col)),
                  pl.BlockSpec((1, tb, D_NA), lambda b, i: (b, i, na_col + 1)),
                  pl.BlockSpec((1, tb, D_NA), lambda b, i: (b, i, na_col + 2)),
                  pl.BlockSpec(sw.shape, lambda b, i: (0, 0)),
                  pl.BlockSpec((1, width), lambda b, i: (0, 0))],
        out_specs=[ospec, ospec, ospec, na_spec, na_spec],
        compiler_params=pltpu.CompilerParams(dimension_semantics=("parallel", "parallel")),
        name="split_projection",
    )(proj, proj, proj, proj, proj, sw, sb[None])


def _hyena_mixer(x1, x2, v, p):
    T = v.shape[1]
    R = math.isqrt(2 * T)
    spec = _hyena_filter_spectra(T, p) * (1.0 / (2 * T))
    spec = jnp.swapaxes(spec.reshape(R, R, HY_ORDER, D_HY), 0, 1)
    tables = _dft_tables(R)
    z = v
    for o, gate in enumerate((x1, x2)):
        hm = jnp.stack([jnp.real(spec[:, :, o]), jnp.imag(spec[:, :, o])])
        z = _gated_long_conv(z, gate, hm, p['hy_bias'][o], tables)
    return z


def _route(aff):
    n = aff.shape[0]
    cap = (CAPACITY_FACTOR * n) // N_EXPERTS
    return lax.top_k(aff.T, cap)


def _mix_body(x_ref, ya_ref, yb_ref, yc_ref, g_ref, wa_ref, wb_ref, wc_ref, wo_ref, lg_ref, lb_ref, wr_ref,
              o_ref, ob_ref, aff_ref):
    g = jax.nn.sigmoid(g_ref[...])
    m = (g[:, :D_MODEL] * jnp.dot(ya_ref[...].astype(BF16), wa_ref[...], preferred_element_type=F32)
         + g[:, D_MODEL:2 * D_MODEL] * jnp.dot(yb_ref[...].astype(BF16), wb_ref[...], preferred_element_type=F32)
         + g[:, 2 * D_MODEL:] * jnp.dot(yc_ref[...].astype(BF16), wc_ref[...], preferred_element_type=F32))
    h = ALPHA * x_ref[...] + jnp.dot(m.astype(BF16), wo_ref[...], preferred_element_type=F32)
    y = _layer_norm(h, lg_ref[...], lb_ref[...])
    o_ref[...] = y
    ob_ref[...] = y.astype(BF16)
    logits = jnp.dot(y, wr_ref[...], precision=lax.Precision.HIGHEST, preferred_element_type=F32)
    e = jnp.exp(logits - jnp.max(logits, -1, keepdims=True))
    aff_ref[...] = e / jnp.sum(e, -1, keepdims=True)


def _mix_and_norm(x, y_a, y_b, y_c, proj, p, *, tm=512):
    n, d = x.shape
    assert n % tm == 0 and proj.shape[1] == 2 * N_BRANCH * d
    row = lambda w: pl.BlockSpec((tm, w), lambda i: (i, 0))
    full = lambda a: pl.BlockSpec(a.shape, lambda i: (0,) * a.ndim)
    wa, wb, wc, wo = (p[k].astype(BF16) for k in ('w_branch_a', 'w_branch_b', 'w_branch_c', 'w_out'))
    lg, lb = p['ln1_g'][None], p['ln1_b'][None]
    return pl.pallas_call(
        _mix_body,
        out_shape=[jax.ShapeDtypeStruct((n, d), F32), jax.ShapeDtypeStruct((n, d), BF16),
                   jax.ShapeDtypeStruct((n, N_EXPERTS), F32)],
        grid=(n // tm,),
        in_specs=[row(d), row(D_RWKV), row(D_NA), row(D_HY),
                  pl.BlockSpec((tm, N_BRANCH * d), lambda i: (i, 1)),
                  full(wa), full(wb), full(wc), full(wo), full(lg), full(lb), full(p['w_router'])],
        out_specs=[row(d), row(d), row(N_EXPERTS)],
        compiler_params=pltpu.CompilerParams(dimension_semantics=("parallel",)),
        name="mix_norm_route",
    )(x, y_a, y_b, y_c, proj, wa, wb, wc, wo, lg, lb, p['w_router'])


def _mixers(x, p):
    B, T, D = x.shape
    n = B * T
    proj2 = _matmul(x.reshape(n, D), p['w_in'])
    proj = proj2.reshape(B, T, -1)
    x1c, x2c, vc, kb, vb = _split_projection(proj, p['hy_short_w'], p['hy_short_b'])
    y_a = _rwkv7_mixer(x, proj, p)
    y_b = _neighbourhood_attention_pallas(proj, kb, vb, p['na_rpb'])
    y_c = _hyena_mixer(x1c, x2c, vc, p)
    return _mix_and_norm(x.reshape(n, D), y_a.reshape(n, -1), y_b.reshape(n, -1), y_c.reshape(n, -1), proj2, p)


def _encoder_layer(xs, p, expert_weights, layer):
    mixed = [_mixers(x, p) for x in xs]
    routes = [_route(aff) for _, _, aff in mixed]
    xe = jnp.concatenate([jnp.take(xb, idx, axis=0) for (_, xb, _), (_, idx) in zip(mixed, routes)], axis=1)
    gate = jnp.concatenate([g for g, _ in routes], axis=1)[..., None]
    ye = _expert_ffn(xe, gate, *expert_weights, layer)
    outs, off = [], 0
    for x, (x1, _, _), (_, idx) in zip(xs, mixed, routes):
        cap = idx.shape[1]
        ffn = jnp.zeros_like(x1).at[idx.reshape(-1)].add(ye[:, off:off + cap].reshape(-1, x1.shape[-1]))
        off += cap
        outs.append(_ln(x1, p['ln2_g'], p['ln2_b'], residual=ffn).reshape(x.shape))
    return outs


def kernel(x_prompt, x_sample, ln_in_g, ln_in_b, w_in, rwkv_mu_rkv, rwkv_mu_x, rwkv_w0, rwkv_w1, rwkv_w2,
           rwkv_a0, rwkv_a1, rwkv_a2, rwkv_g1, rwkv_g2, rwkv_k_k, rwkv_k_a, rwkv_r_k, rwkv_lnx_g, rwkv_lnx_b,
           na_rpb, hy_short_w, hy_short_b, hy_w1, hy_b1, hy_w2, hy_b2, hy_w3, hy_b3, hy_w4, hy_freq, hy_bias,
           w_branch_a, w_branch_b, w_branch_c, w_out, ln1_g, ln1_b, w_router, w_exp_gate, w_exp_up,
           w_exp_down, ln2_g, ln2_b):
    stacked = {
        'w_in': w_in, 'rwkv_mu_rkv': rwkv_mu_rkv, 'rwkv_mu_x': rwkv_mu_x, 'rwkv_w0': rwkv_w0,
        'rwkv_w1': rwkv_w1, 'rwkv_w2': rwkv_w2, 'rwkv_a0': rwkv_a0, 'rwkv_a1': rwkv_a1, 'rwkv_a2': rwkv_a2,
        'rwkv_g1': rwkv_g1, 'rwkv_g2': rwkv_g2, 'rwkv_k_k': rwkv_k_k, 'rwkv_k_a': rwkv_k_a,
        'rwkv_r_k': rwkv_r_k, 'rwkv_lnx_g': rwkv_lnx_g, 'rwkv_lnx_b': rwkv_lnx_b, 'na_rpb': na_rpb,
        'hy_short_w': hy_short_w, 'hy_short_b': hy_short_b, 'hy_w1': hy_w1, 'hy_b1': hy_b1,
        'hy_w2': hy_w2, 'hy_b2': hy_b2, 'hy_w3': hy_w3, 'hy_b3': hy_b3, 'hy_w4': hy_w4,
        'hy_freq': hy_freq, 'hy_bias': hy_bias, 'w_branch_a': w_branch_a, 'w_branch_b': w_branch_b,
        'w_branch_c': w_branch_c, 'w_out': w_out, 'ln1_g': ln1_g, 'ln1_b': ln1_b, 'w_router': w_router,
        'w_exp_gate': w_exp_gate, 'w_exp_up': w_exp_up, 'w_exp_down': w_exp_down,
        'ln2_g': ln2_g, 'ln2_b': ln2_b,
    }
    xs = [_ln(x.reshape(-1, D_MODEL), ln_in_g, ln_in_b).reshape(x.shape) for x in (x_prompt, x_sample)]
    expert_weights = [stacked.pop(name) for name in ('w_exp_gate', 'w_exp_up', 'w_exp_down')]
    for l in range(DEPTH):
        xs = _encoder_layer(xs, {name: arr[l] for name, arr in stacked.items()}, expert_weights, l)
    return tuple(xs)
```

```python
import functools
import math

import jax
import jax.numpy as jnp
import numpy as np
from jax import lax
from jax.experimental import pallas as pl
from jax.experimental.pallas import tpu as pltpu

D_MODEL = 1024
DEPTH = 2
GRID_W = 64
HEAD_DIM = 64
D_RWKV = D_MODEL // 2
RWKV_HEADS = D_RWKV // HEAD_DIM
D_NA = D_MODEL // 4
NA_HEADS = D_NA // HEAD_DIM
D_HY = D_MODEL // 4
N_BRANCH = 3
GN_EPS = 64e-5
NA_WIN_ROWS = 8
NA_WIN_COLS = 16
HY_ORDER = 2
HY_EMB = 33
HY_TOL = 1e-2
HY_FAST_PCT = 0.3
HY_SLOW_PCT = 1.5
N_EXPERTS = 16
CAPACITY_FACTOR = 2
ALPHA = (2 * DEPTH) ** 0.25
LN_EPS = 1e-5

F32 = jnp.float32
BF16 = jnp.bfloat16
EXPERT_FFN_VMEM_BYTES = 52 * 1024 * 1024


def _mm_body(x_ref, w_ref, o_ref):
    o_ref[...] = jnp.dot(x_ref[...].astype(BF16), w_ref[...], preferred_element_type=F32).astype(o_ref.dtype)


def _matmul(x, w, *, tm=2048, tn=512, out_dtype=F32):
    m, k = x.shape
    _, n = w.shape
    tm = min(tm, m)
    tn = min(tn, n)
    assert m % tm == 0 and n % tn == 0
    return pl.pallas_call(
        _mm_body,
        out_shape=jax.ShapeDtypeStruct((m, n), out_dtype),
        grid=(m // tm, n // tn),
        in_specs=[pl.BlockSpec((tm, k), lambda i, j: (i, 0)),
                  pl.BlockSpec((k, tn), lambda i, j: (0, j))],
        out_specs=pl.BlockSpec((tm, tn), lambda i, j: (i, j)),
        name="dense_matmul",
    )(x, w)


def _expert_ffn_body(x_ref, wg_ref, wu_ref, wd_ref, gate_ref, o_ref, acc_ref):
    f = pl.program_id(2)

    @pl.when(f == 0)
    def _():
        acc_ref[...] = jnp.zeros_like(acc_ref)

    xb = x_ref[0]
    hg = jnp.dot(xb, wg_ref[0, 0].astype(BF16), preferred_element_type=F32)
    hu = jnp.dot(xb, wu_ref[0, 0].astype(BF16), preferred_element_type=F32)
    h = (hg * jax.nn.sigmoid(hg) * hu).astype(BF16)
    acc_ref[...] += jnp.dot(h, wd_ref[0, 0].astype(BF16), preferred_element_type=F32)

    @pl.when(f == pl.num_programs(2) - 1)
    def _():
        o_ref[0] = acc_ref[...] * gate_ref[0]


def _expert_ffn(xe, gate, w_gate, w_up, w_down, layer, *, tm=2048, tf=256):
    e, m, d = xe.shape
    ff = w_gate.shape[-1]
    tm = min(tm, m)
    assert m % tm == 0 and ff % tf == 0
    return pl.pallas_call(
        _expert_ffn_body,
        out_shape=jax.ShapeDtypeStruct((e, m, d), F32),
        grid=(e, m // tm, ff // tf),
        in_specs=[pl.BlockSpec((1, tm, d), lambda g, i, f: (g, i, 0)),
                  pl.BlockSpec((1, 1, d, tf), lambda g, i, f: (layer, g, 0, f)),
                  pl.BlockSpec((1, 1, d, tf), lambda g, i, f: (layer, g, 0, f)),
                  pl.BlockSpec((1, 1, tf, d), lambda g, i, f: (layer, g, f, 0)),
                  pl.BlockSpec((1, tm, 1), lambda g, i, f: (g, i, 0))],
        out_specs=pl.BlockSpec((1, tm, d), lambda g, i, f: (g, i, 0)),
        scratch_shapes=[pltpu.VMEM((tm, d), F32)],
        compiler_params=pltpu.CompilerParams(
            dimension_semantics=("parallel", "parallel", "arbitrary"), vmem_limit_bytes=EXPERT_FFN_VMEM_BYTES),
        name="expert_ffn",
    )(xe, w_gate, w_up, w_down, gate)


def _layer_norm(h, g, b):
    hc = h - jnp.mean(h, -1, keepdims=True)
    var = jnp.mean(hc * hc, -1, keepdims=True)
    return hc * lax.rsqrt(var + LN_EPS) * g + b


def _ln_body(x_ref, g_ref, b_ref, o_ref):
    o_ref[...] = _layer_norm(x_ref[...], g_ref[...], b_ref[...])


def _ln_residual_body(x_ref, r_ref, g_ref, b_ref, o_ref):
    o_ref[...] = _layer_norm(ALPHA * x_ref[...] + r_ref[...], g_ref[...], b_ref[...])


def _ln(x, g, b, residual=None, *, tm=1024):
    n, d = x.shape
    tm = min(tm, n)
    assert n % tm == 0
    row = pl.BlockSpec((tm, d), lambda i: (i, 0))
    vec = pl.BlockSpec((1, d), lambda i: (0, 0))
    args = (x,) if residual is None else (x, residual)
    return pl.pallas_call(
        _ln_body if residual is None else _ln_residual_body,
        out_shape=jax.ShapeDtypeStruct((n, d), F32),
        grid=(n // tm,),
        in_specs=[row] * len(args) + [vec, vec],
        out_specs=row,
        compiler_params=pltpu.CompilerParams(dimension_semantics=("parallel",)),
        name="layer_norm",
    )(*args, g[None], b[None])


RWKV_SUB = 16
RWKV_PAIRS = 2
LANES = 128
RWKV_PIECES = 3
RWKV_GAMMA_SLOT = 2 * RWKV_PIECES * RWKV_SUB


def _rwkv_selectors():
    z = np.zeros((RWKV_SUB + 1, LANES, LANES), np.float32)
    for h in range(LANES // HEAD_DIM):
        lanes = slice(h * HEAD_DIM, (h + 1) * HEAD_DIM)
        for p in range(RWKV_PIECES):
            for t in range(RWKV_SUB):
                z[t, (h * RWKV_PIECES + p) * RWKV_SUB + t, lanes] = 1.0
            z[RWKV_SUB, RWKV_GAMMA_SLOT + 8 * h + p, lanes] = 1.0
    return jnp.asarray(z, BF16)


def _split3(x):
    hi = x.astype(BF16).astype(F32)
    rem = x - hi
    mid = rem.astype(BF16).astype(F32)
    lo = (rem - mid).astype(BF16).astype(F32)
    return [hi, mid, lo]


def _operand_tile(xa, xb, extra):
    pa = jnp.concatenate(_split3(xa), axis=0)
    pb = pa if xb is None else jnp.concatenate(_split3(xb), axis=0)
    lane_lo = lax.broadcasted_iota(jnp.int32, pa.shape, 1) < HEAD_DIM
    head0 = jnp.where(lane_lo, pa, pltpu.roll(pb, HEAD_DIM, 1))
    head1 = jnp.where(lane_lo, pltpu.roll(pa, HEAD_DIM, 1), pb)
    tt = jnp.concatenate([head0, head1, extra], axis=0).T
    return tt[:HEAD_DIM].astype(BF16), tt[HEAD_DIM:].astype(BF16)


def _rwkv_prepare(refs, base, lanes, reverse):
    a_ref, b_ref, k_ref, r_ref, g_ref = refs
    rows = pl.ds(base, RWKV_SUB)
    last = 0 if reverse else RWKV_SUB - 1
    g_last = g_ref[0, 0, rows, lanes][last:last + 1]
    gam = jnp.concatenate(_split3(g_last) + [jnp.zeros((8 - RWKV_PIECES, LANES), F32)], axis=0)
    extra = jnp.concatenate([gam, pltpu.roll(gam, HEAD_DIM, 1), jnp.zeros((16, LANES), F32)], axis=0)
    g_a, g_b = _operand_tile(a_ref[0, 0, rows, lanes], b_ref[0, 0, rows, lanes], extra)
    g_k, _ = _operand_tile(k_ref[0, 0, rows, lanes], None, jnp.zeros((32, LANES), F32))
    rg = r_ref[0, 0, rows, lanes]
    pad = jnp.zeros((HEAD_DIM - RWKV_SUB, LANES), F32)
    r_t = jnp.concatenate([rg, pad, pltpu.roll(rg, HEAD_DIM, 1), pad], axis=0).T[:HEAD_DIM]
    return jnp.concatenate([g_a, g_b, g_k], axis=0), r_t


def _rwkv_steps(streams, z_ref, s_ref):
    head_lane = (lax.broadcasted_iota(jnp.int32, (HEAD_DIM, LANES), 1) // HEAD_DIM) * HEAD_DIM
    lanes = [slice(q * LANES, (q + 1) * LANES) for _, _, _, _, _, q, _, _ in streams]
    vs = [s[2][0, pl.ds(s[6], RWKV_SUB), ln] for s, ln in zip(streams, lanes)]
    sts = [s_ref[d, q] for _, _, _, _, d, q, _, _ in streams]
    ys = [[None] * RWKV_SUB for _ in streams]
    gammas = [jnp.dot(s[0][:HEAD_DIM], z_ref[RWKV_SUB], preferred_element_type=F32) for s in streams]
    for p in range(RWKV_SUB // 2):
        cols = []
        for g_all, _, _, _, _, _, _, reverse in streams:
            ts = (RWKV_SUB - 1 - 2 * p, RWKV_SUB - 2 - 2 * p) if reverse else (2 * p, 2 * p + 1)
            sel = jnp.concatenate([z_ref[ts[0]], z_ref[ts[1]]], axis=1)
            cols.append((ts, jnp.dot(g_all, sel, preferred_element_type=F32)))
        for n in range(2):
            for i, (_, r_t, _, _, _, _, _, _) in enumerate(streams):
                ts, c = cols[i]
                t = ts[n]
                a_c, b_c, k_c = (c[o * HEAD_DIM:(o + 1) * HEAD_DIM, n * LANES:(n + 1) * LANES] for o in range(3))
                r_c = jnp.take_along_axis(r_t, head_lane + t, axis=1)
                sa = jnp.sum(sts[i] * a_c, axis=0, keepdims=True)
                sts[i] = sts[i] + b_c * sa + k_c * vs[i][t:t + 1, :]
                ys[i][t] = jnp.sum(sts[i] * r_c, axis=0, keepdims=True)
    for i, (_, _, _, y_ref, d, q, base, _) in enumerate(streams):
        s_ref[d, q] = sts[i] * gammas[i]
        y_ref[0, 0, pl.ds(base, RWKV_SUB), lanes[i]] = jnp.concatenate(ys[i], axis=0)


def _rwkv_body(af, bf, kf, rf, gf, vf, ab, bb, kb, rb, gb, vb, z_ref, yf_ref, yb_ref, s_ref, g_ref,
               rt_ref, *, tb):
    @pl.when(pl.program_id(2) == 0)
    def _():
        s_ref[...] = jnp.zeros_like(s_ref)

    nsub = tb // RWKV_SUB
    refs_f = (af, bf, kf, rf, gf)
    refs_b = (ab, bb, kb, rb, gb)

    def base_f(c):
        return pl.multiple_of(c * RWKV_SUB, RWKV_SUB)

    def base_b(c):
        return pl.multiple_of((nsub - 1 - c) * RWKV_SUB, RWKV_SUB)

    def prepare(slot, c):
        for q in range(RWKV_PAIRS):
            lanes = slice(q * LANES, (q + 1) * LANES)
            g_ref[slot, 0, q], rt_ref[slot, 0, q] = _rwkv_prepare(refs_f, base_f(c), lanes, False)
            g_ref[slot, 1, q], rt_ref[slot, 1, q] = _rwkv_prepare(refs_b, base_b(c), lanes, True)

    prepare(0, 0)

    def sub(c, carry):
        slot = c % 2
        nxt = jnp.minimum(c + 1, nsub - 1)
        streams = []
        for q in range(RWKV_PAIRS):
            streams.append((g_ref[slot, 0, q], rt_ref[slot, 0, q], vf, yf_ref, 0, q, base_f(c), False))
            streams.append((g_ref[slot, 1, q], rt_ref[slot, 1, q], vb, yb_ref, 1, q, base_b(c), True))
        prepare(1 - slot, nxt)
        _rwkv_steps(streams, z_ref, s_ref)
        return carry

    lax.fori_loop(0, nsub, sub, 0)


def _rwkv7_scan(a, b, k, r, g, v, *, tb=512, interpret=False):
    B, T, C = v.shape
    tb = min(tb, T)
    nt = T // tb
    width = RWKV_PAIRS * LANES
    assert T % tb == 0 and tb % RWKV_SUB == 0 and C % width == 0
    fwd3 = pl.BlockSpec((1, tb, width), lambda b, j, i: (b, i, j))
    bwd3 = pl.BlockSpec((1, tb, width), lambda b, j, i: (b, nt - 1 - i, j))
    fwd4 = pl.BlockSpec((1, 1, tb, width), lambda b, j, i: (0, b, i, j))
    bwd4 = pl.BlockSpec((1, 1, tb, width), lambda b, j, i: (1, b, nt - 1 - i, j))
    yf, yb = pl.pallas_call(
        functools.partial(_rwkv_body, tb=tb),
        out_shape=[jax.ShapeDtypeStruct((1, B, T, C), F32)] * 2,
        grid=(B, C // width, nt),
        in_specs=[fwd4] * 5 + [fwd3] + [bwd4] * 5 + [bwd3]
        + [pl.BlockSpec((RWKV_SUB + 1, LANES, LANES), lambda b, j, i: (0, 0, 0))],
        out_specs=[pl.BlockSpec((1, 1, tb, width), lambda b, j, i: (0, b, i, j)),
                   pl.BlockSpec((1, 1, tb, width), lambda b, j, i: (0, b, nt - 1 - i, j))],
        scratch_shapes=[pltpu.VMEM((2, RWKV_PAIRS, HEAD_DIM, LANES), F32),
                        pltpu.VMEM((2, 2, RWKV_PAIRS, 3 * HEAD_DIM, LANES), BF16),
                        pltpu.VMEM((2, 2, RWKV_PAIRS, HEAD_DIM, LANES), F32)],
        compiler_params=pltpu.CompilerParams(dimension_semantics=("parallel", "parallel", "arbitrary")),
        name="rwkv7_scan",
        interpret=interpret,
    )(a, b, k, r, g, v, a, b, k, r, g, v, _rwkv_selectors())
    return yf, yb


HALO = 8


def _head_sums(x, bd_ref):
    bd = bd_ref[...]
    return sum(jnp.dot(piece.astype(BF16), bd, preferred_element_type=F32) for piece in _split3(x))


def _token_shift(cur, prev_row, next_row, mu):
    tb = cur.shape[0]
    rid = lax.broadcasted_iota(jnp.int32, cur.shape, 0)
    prev = jnp.where(rid == 0, prev_row, pltpu.roll(cur, 1, 0))
    nxt = jnp.where(rid == tb - 1, next_row, pltpu.roll(cur, tb - 1, 0))
    return cur + mu[0:1] * (prev - cur) + mu[1:2] * (nxt - cur)


def _rwkv_prep_body(x_ref, xp_ref, xn_ref, r_ref, rp_ref, rn_ref, k_ref, kp_ref, kn_ref, v_ref, vp_ref, vn_ref,
                    mux_ref, mur_ref, w1_ref, w2_ref, w0_ref, a1_ref, a2_ref, a0_ref, g1_ref, g2_ref,
                    kk_w_ref, ka_w_ref, bd_ref, tri_ref,
                    ro_ref, vo_ref, go_ref, kd_ref, sa_ref, sb_ref, sk_ref, sr_ref, sg_ref):
    i = pl.program_id(1)
    first = i == 0
    last = i == pl.num_programs(1) - 1

    def shifted(cur_ref, p_ref, n_ref, mu):
        prev_row = jnp.where(first, 0.0, p_ref[0, HALO - 1:HALO, :])
        next_row = jnp.where(last, 0.0, n_ref[0, 0:1, :])
        return _token_shift(cur_ref[0], prev_row, next_row, mu)

    r = shifted(r_ref, rp_ref, rn_ref, mur_ref[0])
    k = shifted(k_ref, kp_ref, kn_ref, mur_ref[1])
    v = shifted(v_ref, vp_ref, vn_ref, mur_ref[2])
    xw = shifted(x_ref, xp_ref, xn_ref, mux_ref[0]).astype(BF16)
    xa = shifted(x_ref, xp_ref, xn_ref, mux_ref[1]).astype(BF16)
    xg = shifted(x_ref, xp_ref, xn_ref, mux_ref[2]).astype(BF16)
    hg = jax.nn.sigmoid(jnp.dot(xg, g1_ref[...], preferred_element_type=F32)).astype(BF16)
    go_ref[0] = jnp.dot(hg, g2_ref[...], preferred_element_type=F32)
    kk = k * kk_w_ref[...]
    kk = kk / jnp.maximum(jnp.sqrt(_head_sums(kk * kk, bd_ref)), 1e-12)
    ro_ref[0] = r
    vo_ref[0] = v
    for z in range(2):
        hw = jnp.tanh(jnp.dot(xw, w1_ref[z], preferred_element_type=F32)).astype(BF16)
        ha = jnp.dot(xa, a1_ref[z], preferred_element_type=F32).astype(BF16)
        w_raw = w0_ref[z:z + 1] + jnp.dot(hw, w2_ref[z], preferred_element_type=F32)
        lw = -math.exp(-0.5) * jax.nn.sigmoid(w_raw)
        a = jax.nn.sigmoid(a0_ref[z:z + 1]
                           + jnp.dot(ha, a2_ref[z], preferred_element_type=F32))
        kd = k * (1.0 + (a - 1.0) * ka_w_ref[...])
        kd_ref[z, 0] = kd
        cum = sum(jnp.dot(tri_ref[z], piece.astype(BF16), preferred_element_type=F32) for piece in _split3(lw))
        gamma = jnp.exp(cum)
        inv_gamma = jnp.exp(-cum)
        sa_ref[z, 0] = -kk * jnp.exp(cum - lw)
        sb_ref[z, 0] = kk * a * inv_gamma
        sk_ref[z, 0] = kd * inv_gamma
        sr_ref[z, 0] = r * gamma
        sg_ref[z, 0] = gamma


def _rwkv_post_body(yf_ref, yb_ref, r_ref, v_ref, kd_ref, g_ref, lng_ref, lnb_ref, rk_ref, bd_ref, o_ref):
    y = yf_ref[0, 0] + yb_ref[0, 0]
    inv_n = 1.0 / HEAD_DIM
    yc = y - _head_sums(y, bd_ref) * inv_n
    yn = yc * lax.rsqrt(_head_sums(yc * yc, bd_ref) * inv_n + GN_EPS) * lng_ref[...] + lnb_ref[...]
    r, v = r_ref[0], v_ref[0]
    bonus = (_head_sums(r * kd_ref[0, 0] * rk_ref[...], bd_ref)
             + _head_sums(r * kd_ref[1, 0] * rk_ref[...], bd_ref)) * v
    o_ref[0] = ((yn + bonus) * g_ref[0]).astype(o_ref.dtype)


def _rwkv7_mixer(u, proj, p, *, tb=256):
    B, T, D = u.shape
    C = D_RWKV
    tb = min(tb, T)
    nt = T // tb
    assert T % tb == 0 and tb % HALO == 0
    hb = tb // HALO

    def cur(width, col):
        return pl.BlockSpec((1, tb, width), lambda b, i: (b, i, col))

    def prev(width, col):
        return pl.BlockSpec((1, HALO, width), lambda b, i: (b, jnp.maximum(i * hb - 1, 0), col))

    def nxt(width, col):
        return pl.BlockSpec((1, HALO, width), lambda b, i: (b, jnp.minimum((i + 1) * hb, T // HALO - 1), col))

    def full(a):
        return pl.BlockSpec(a.shape, lambda b, i: (0,) * a.ndim)

    lane = np.arange(C) // HEAD_DIM
    bd = jnp.asarray(lane[:, None] == lane[None, :], BF16)
    t_id = np.arange(tb)
    same = (t_id[:, None] // RWKV_SUB) == (t_id[None, :] // RWKV_SUB)
    tri = jnp.asarray(np.stack([same & (t_id[None, :] <= t_id[:, None]),
                                same & (t_id[None, :] >= t_id[:, None])]), BF16)
    weights = [p['rwkv_mu_x'], p['rwkv_mu_rkv'], p['rwkv_w1'].astype(BF16), p['rwkv_w2'].astype(BF16), p['rwkv_w0'],
               p['rwkv_a1'].astype(BF16), p['rwkv_a2'].astype(BF16), p['rwkv_a0'], p['rwkv_g1'].astype(BF16), p['rwkv_g2'].astype(BF16),
               p['rwkv_k_k'][None], p['rwkv_k_a'][None], bd, tri]
    one = jax.ShapeDtypeStruct((B, T, C), F32)
    two = jax.ShapeDtypeStruct((2, B, T, C), F32)
    out1 = pl.BlockSpec((1, tb, C), lambda b, i: (b, i, 0))
    out2 = pl.BlockSpec((2, 1, tb, C), lambda b, i: (0, b, i, 0))
    r, v, g, kd, sa, sb, sk, sr, sg = pl.pallas_call(
        _rwkv_prep_body,
        out_shape=[one, one, one, two, two, two, two, two, two],
        grid=(B, nt),
        in_specs=[cur(D, 0), prev(D, 0), nxt(D, 0)]
        + [spec(C, col) for col in range(3) for spec in (cur, prev, nxt)]
        + [full(w) for w in weights],
        out_specs=[out1, out1, out1, out2, out2, out2, out2, out2, out2],
        compiler_params=pltpu.CompilerParams(dimension_semantics=("parallel", "parallel")),
        name="rwkv7_prepare",
    )(u, u, u, *([proj] * 9), *weights)
    yf, yb = _rwkv7_scan(sa, sb, sk, sr, sg, v)
    ydir = pl.BlockSpec((1, 1, tb, C), lambda b, i: (0, b, i, 0))
    post_w = [p['rwkv_lnx_g'][None], p['rwkv_lnx_b'][None], p['rwkv_r_k'].reshape(1, C), bd]
    return pl.pallas_call(
        _rwkv_post_body,
        out_shape=jax.ShapeDtypeStruct((B, T, C), BF16),
        grid=(B, nt),
        in_specs=[ydir, ydir, out1, out1, out2, out1] + [full(w) for w in post_w],
        out_specs=out1,
        compiler_params=pltpu.CompilerParams(dimension_semantics=("parallel", "parallel")),
        name="rwkv7_output",
    )(yf, yb, r, v, kd, g, *post_w)


NA_ROWS_PER_STEP = 8


def _na_body(q_ref, k_ref, v_ref, bias_ref, mask_ref, o_ref, *, rows):
    kr = NA_WIN_ROWS
    win = kr * GRID_W
    valid = mask_ref[...] != 0
    lane = lax.broadcasted_iota(jnp.int32, (GRID_W, LANES), 1)
    for ii in range(NA_ROWS_PER_STEP):
        i = pl.program_id(1) * NA_ROWS_PER_STEP + ii
        start = jnp.clip(i - kr // 2, 0, rows - kr)
        d = i - start
        krows = pl.ds(pl.multiple_of(start * GRID_W, GRID_W), win)
        for pair in range(D_NA // LANES):
            lanes = slice(pair * LANES, (pair + 1) * LANES)
            q2 = q_ref[0, ii * GRID_W:(ii + 1) * GRID_W, lanes] * (HEAD_DIM ** -0.5)
            k2 = k_ref[0, krows, lanes]
            v2 = v_ref[0, krows, lanes]
            outs = []
            for hh in range(LANES // HEAD_DIM):
                in_head = (lane >= hh * HEAD_DIM) & (lane < (hh + 1) * HEAD_DIM)
                qh = jnp.where(in_head, q2, 0.0).astype(BF16)
                s = lax.dot_general(qh, k2, (((1,), (1,)), ((), ())), preferred_element_type=F32)
                s = s + bias_ref[d, pair * (LANES // HEAD_DIM) + hh]
                s = jnp.where(valid, s, -1e30)
                m = jnp.max(s, axis=-1, keepdims=True)
                e = jnp.exp(s - m)
                p = e / jnp.sum(e, axis=-1, keepdims=True)
                outs.append(jnp.dot(p.astype(BF16), v2, preferred_element_type=F32))
            o_ref[0, ii * GRID_W:(ii + 1) * GRID_W, lanes] = jnp.where(lane < HEAD_DIM, outs[0], outs[1])


def _neighbourhood_attention_pallas(proj, k, v, rpb, *, interpret=False):
    B, T, _ = proj.shape
    q_col = 3 * D_RWKV // D_NA
    rows = T // GRID_W
    kr, kc = NA_WIN_ROWS, NA_WIN_COLS
    assert rows >= kr and rows % NA_ROWS_PER_STEP == 0
    ci = jnp.arange(GRID_W)
    col_start = jnp.clip(ci - kc // 2, 0, GRID_W - kc)
    col_valid = (ci[None] >= col_start[:, None]) & (ci[None] < col_start[:, None] + kc)
    mask = jnp.tile(col_valid.astype(jnp.int32), (1, kr))
    dc_idx = jnp.clip(ci[None] - ci[:, None] + kc - 1, 0, 2 * kc - 2)
    dr_idx = jnp.arange(kr)[None, :] - jnp.arange(kr)[:, None] + kr - 1
    onehot = (jnp.arange(2 * kc - 1)[:, None] == dc_idx.reshape(1, -1)).astype(F32)
    bias = jnp.dot(rpb[:, dr_idx].reshape(-1, 2 * kc - 1), onehot, precision=lax.Precision.HIGHEST)
    bias = bias.reshape(NA_HEADS, kr, kr, GRID_W, GRID_W)
    bias = jnp.transpose(bias, (1, 0, 3, 2, 4)).reshape(kr, NA_HEADS, GRID_W, kr * GRID_W)
    tq = NA_ROWS_PER_STEP * GRID_W
    return pl.pallas_call(
        functools.partial(_na_body, rows=rows),
        out_shape=jax.ShapeDtypeStruct((B, T, D_NA), F32),
        grid=(B, rows // NA_ROWS_PER_STEP),
        in_specs=[pl.BlockSpec((1, tq, D_NA), lambda b, i: (b, i, q_col)),
                  pl.BlockSpec((1, T, D_NA), lambda b, i: (b, 0, 0)),
                  pl.BlockSpec((1, T, D_NA), lambda b, i: (b, 0, 0)),
                  pl.BlockSpec((kr, NA_HEADS, GRID_W, kr * GRID_W), lambda b, i: (0, 0, 0, 0)),
                  pl.BlockSpec((GRID_W, kr * GRID_W), lambda b, i: (0, 0))],
        out_specs=pl.BlockSpec((1, tq, D_NA), lambda b, i: (b, i, 0)),
        compiler_params=pltpu.CompilerParams(dimension_semantics=("parallel", "arbitrary")),
        name="neighbourhood_attention",
        interpret=interpret,
    )(proj, k, v, bias, mask)


def _hyena_filters(L, p):
    t = jnp.linspace(0.0, 1.0, L, dtype=F32)[:, None]
    n_bands = (HY_EMB - 1) // 2
    omega = 2.0 * math.pi * jnp.arange(L, dtype=F32)[:, None] / L
    bands = jnp.linspace(1e-4, n_bands - 1, n_bands, dtype=F32)[None]
    z = jnp.concatenate([t, jnp.cos(bands * omega), -jnp.sin(bands * omega)], -1)
    freq = p['hy_freq']
    h = jnp.sin(freq * (z @ p['hy_w1'] + p['hy_b1']))
    h = jnp.sin(freq * (h @ p['hy_w2'] + p['hy_b2']))
    h = jnp.sin(freq * (h @ p['hy_w3'] + p['hy_b3']))
    h = (h @ p['hy_w4']).reshape(L, HY_ORDER, 2, D_HY)
    deltas = jnp.abs(jnp.linspace(math.log(HY_TOL) / HY_FAST_PCT, math.log(HY_TOL) / HY_SLOW_PCT, D_HY, dtype=F32))
    h = h * jnp.exp(-t * deltas)[:, None, None, :]
    h_f, h_b = h[:, :, 0], h[:, :, 1]
    kern = jnp.concatenate([h_f, jnp.zeros((1, HY_ORDER, D_HY), F32), h_b[1:][::-1]], 0)
    return kern / jnp.sum(jnp.abs(kern), 0, keepdims=True)


def _dft_tables(R):
    n = np.arange(R)
    ang = (2.0 * np.pi / R) * ((n[:, None] * n[None, :]) % R)
    tw = (2.0 * np.pi / (R * R)) * (n[:, None] * n[None, :])
    c, s, ct, st = (jnp.asarray(f(a), F32) for a in (ang, tw) for f in (np.cos, np.sin))
    first = jnp.concatenate([c, -s], axis=0)
    last = jnp.concatenate([c, -s], axis=1)[:R // 2]
    cp = c[None] * ct[:, None, :] - s[None] * st[:, None, :]
    sp = s[None] * ct[:, None, :] + c[None] * st[:, None, :]
    fwd = jnp.concatenate([jnp.concatenate([cp, sp], axis=2), jnp.concatenate([-sp, cp], axis=2)], axis=1)
    cpt = c[None] * ct[:, :, None] - s[None] * st[:, :, None]
    spt = s[None] * ct[:, :, None] + c[None] * st[:, :, None]
    inv = jnp.concatenate([jnp.concatenate([cpt, -spt], axis=2), jnp.concatenate([spt, cpt], axis=2)], axis=1)
    return first.astype(BF16), fwd.astype(BF16), inv.astype(BF16), last.astype(BF16)


def _conv_first_body(f_ref, z_ref, o_ref):
    o_ref[0] = jnp.dot(f_ref[...], z_ref[0].astype(BF16), preferred_element_type=F32).astype(o_ref.dtype)


def _spectrum_mid_body(a_ref, mf_ref, o_ref, *, R, kb):
    for j in range(kb):
        a = jnp.concatenate([a_ref[0, 0, j], a_ref[0, 1, j]], axis=0)
        x = jnp.dot(mf_ref[j], a, preferred_element_type=F32)
        o_ref[0, j] = x[:R]
        o_ref[1, j] = x[R:]


def _filter_spectrum(kern, tables, *, tn=4096, kb=4):
    n, C = kern.shape
    R = math.isqrt(n)
    first, fwd, _, _ = tables
    tn = min(tn, R * C)
    a = pl.pallas_call(
        _conv_first_body,
        out_shape=jax.ShapeDtypeStruct((1, 2 * R, R * C), BF16),
        grid=(1, (R * C) // tn),
        in_specs=[pl.BlockSpec((2 * R, R), lambda b, j: (0, 0)),
                  pl.BlockSpec((1, R, tn), lambda b, j: (b, 0, j))],
        out_specs=pl.BlockSpec((1, 2 * R, tn), lambda b, j: (b, 0, j)),
        name="filter_dft_first",
    )(first, kern.reshape(1, R, R * C))
    return pl.pallas_call(
        functools.partial(_spectrum_mid_body, R=R, kb=kb),
        out_shape=jax.ShapeDtypeStruct((2, R, R, C), F32),
        grid=(R // kb,),
        in_specs=[pl.BlockSpec((1, 2, kb, R, C), lambda k: (0, 0, k, 0, 0)),
                  pl.BlockSpec((kb, 2 * R, 2 * R), lambda k: (k, 0, 0))],
        out_specs=pl.BlockSpec((2, kb, R, C), lambda k: (0, k, 0, 0)),
        name="filter_dft_mid",
    )(a.reshape(1, 2, R, R, C), fwd)


def _conv_mid_body(a_ref, mf_ref, mi_ref, h_ref, o_ref, *, R, kb):
    for j in range(kb):
        a = jnp.concatenate([a_ref[0, 0, j], a_ref[0, 1, j]], axis=0)
        x = jnp.dot(mf_ref[j], a, preferred_element_type=F32)
        xr, xi = x[:R], x[R:]
        hr, hi = h_ref[0, j], h_ref[1, j]
        y = jnp.concatenate([xr * hr - xi * hi, xr * hi + xi * hr], axis=0).astype(BF16)
        b = jnp.dot(mi_ref[j], y, preferred_element_type=F32)
        o_ref[0, 0, j] = b[:R].astype(o_ref.dtype)
        o_ref[0, 1, j] = b[R:].astype(o_ref.dtype)


def _conv_last_body(f_ref, b_ref, z_ref, gate_ref, bias_ref, o_ref):
    y = jnp.dot(f_ref[...], b_ref[0], preferred_element_type=F32)
    o_ref[0] = gate_ref[0] * (y + z_ref[0] * bias_ref[...])


def _gated_long_conv(z, gate, hm, order, bias, tables, *, tn=4096, kb=4, interpret=False):
    B, L, C = z.shape
    R = math.isqrt(2 * L)
    assert R * R == 2 * L and R % (2 * kb) == 0
    first, fwd, inv, last = tables
    tn = min(tn, R * C)
    nt = (R * C) // tn
    z2 = z.reshape(B, R // 2, R * C)
    a = pl.pallas_call(
        _conv_first_body,
        out_shape=jax.ShapeDtypeStruct((B, 2 * R, R * C), BF16),
        grid=(B, nt),
        in_specs=[pl.BlockSpec((2 * R, R // 2), lambda b, j: (0, 0)),
                  pl.BlockSpec((1, R // 2, tn), lambda b, j: (b, 0, j))],
        out_specs=pl.BlockSpec((1, 2 * R, tn), lambda b, j: (b, 0, j)),
        name="long_conv_first",
        interpret=interpret,
    )(first[:, :R // 2], z2)
    bmid = pl.pallas_call(
        functools.partial(_conv_mid_body, R=R, kb=kb),
        out_shape=jax.ShapeDtypeStruct((B, 2, R, R, C), BF16),
        grid=(B, R // kb),
        in_specs=[pl.BlockSpec((1, 2, kb, R, C), lambda b, k: (b, 0, k, 0, 0)),
                  pl.BlockSpec((kb, 2 * R, 2 * R), lambda b, k: (k, 0, 0)),
                  pl.BlockSpec((kb, 2 * R, 2 * R), lambda b, k: (k, 0, 0)),
                  pl.BlockSpec((2, kb, R, C), lambda b, k: (0, k, 0, order))],
        out_specs=pl.BlockSpec((1, 2, kb, R, C), lambda b, k: (b, 0, k, 0, 0)),
        name="long_conv_mid",
        interpret=interpret,
    )(a.reshape(B, 2, R, R, C), fwd, inv, hm)
    out = pl.pallas_call(
        _conv_last_body,
        out_shape=jax.ShapeDtypeStruct((B, R // 2, R * C), F32),
        grid=(B, nt),
        in_specs=[pl.BlockSpec((R // 2, 2 * R), lambda b, j: (0, 0)),
                  pl.BlockSpec((1, 2 * R, tn), lambda b, j: (b, 0, j)),
                  pl.BlockSpec((1, R // 2, tn), lambda b, j: (b, 0, j)),
                  pl.BlockSpec((1, R // 2, tn), lambda b, j: (b, 0, j)),
                  pl.BlockSpec((1, tn), lambda b, j: (0, j))],
        out_specs=pl.BlockSpec((1, R // 2, tn), lambda b, j: (b, 0, j)),
        name="long_conv_last",
        interpret=interpret,
    )(last, bmid.reshape(B, 2 * R, R * C), z2, gate.reshape(B, R // 2, R * C), jnp.tile(bias, R)[None])
    return out.reshape(B, L, C)


def _short_conv_body(u_ref, up_ref, un_ref, nk_ref, nv_ref, sw_ref, sb_ref, x1_ref, x2_ref, v_ref, ko_ref, vo_ref):
    i = pl.program_id(1)
    cur = u_ref[0]
    tb = cur.shape[0]
    prev_row = jnp.where(i == 0, 0.0, up_ref[0, HALO - 1:HALO, :])
    next_row = jnp.where(i == pl.num_programs(1) - 1, 0.0, un_ref[0, 0:1, :])
    rid = lax.broadcasted_iota(jnp.int32, cur.shape, 0)
    prev = jnp.where(rid == 0, prev_row, pltpu.roll(cur, 1, 0))
    nxt = jnp.where(rid == tb - 1, next_row, pltpu.roll(cur, tb - 1, 0))
    u = prev * sw_ref[0:1] + cur * sw_ref[1:2] + nxt * sw_ref[2:3] + sb_ref[...]
    x1_ref[0] = u[:, :D_HY]
    x2_ref[0] = u[:, D_HY:2 * D_HY]
    v_ref[0] = u[:, 2 * D_HY:]
    ko_ref[0] = nk_ref[0].astype(BF16)
    vo_ref[0] = nv_ref[0].astype(BF16)


def _split_projection(proj, sw, sb, *, tb=512):
    B, T, n_in = proj.shape
    width = 3 * D_HY
    col = (3 * D_RWKV + 3 * D_NA) // width
    assert col * width == 3 * D_RWKV + 3 * D_NA
    tb = min(tb, T)
    hb = tb // HALO
    out = jax.ShapeDtypeStruct((B, T, D_HY), F32)
    out_na = jax.ShapeDtypeStruct((B, T, D_NA), BF16)
    ospec = pl.BlockSpec((1, tb, D_HY), lambda b, i: (b, i, 0))
    na_spec = pl.BlockSpec((1, tb, D_NA), lambda b, i: (b, i, 0))
    na_col = 3 * D_RWKV // D_NA
    return pl.pallas_call(
        _short_conv_body,
        out_shape=[out, out, out, out_na, out_na],
        grid=(B, T // tb),
        in_specs=[pl.BlockSpec((1, tb, width), lambda b, i: (b, i, col)),
                  pl.BlockSpec((1, HALO, width), lambda b, i: (b, jnp.maximum(i * hb - 1, 0), col)),
                  pl.BlockSpec((1, HALO, width), lambda b, i: (b, jnp.minimum((i + 1) * hb, T // HALO - 1), col)),
                  pl.BlockSpec((1, tb, D_NA), lambda b, i: (b, i, na_col + 1)),
                  pl.BlockSpec((1, tb, D_NA), lambda b, i: (b, i, na_col + 2)),
                  pl.BlockSpec(sw.shape, lambda b, i: (0, 0)),
                  pl.BlockSpec((1, width), lambda b, i: (0, 0))],
        out_specs=[ospec, ospec, ospec, na_spec, na_spec],
        compiler_params=pltpu.CompilerParams(dimension_semantics=("parallel", "parallel")),
        name="split_projection",
    )(proj, proj, proj, proj, proj, sw, sb[None])


def _hyena_mixer(x1, x2, v, p):
    T = v.shape[1]
    R = math.isqrt(2 * T)
    tables = _dft_tables(R)
    kern = _hyena_filters(T, p).reshape(2 * T, HY_ORDER * D_HY) * (1.0 / (2 * T))
    hm = _filter_spectrum(kern, tables)
    z = v
    for o, gate in enumerate((x1, x2)):
        z = _gated_long_conv(z, gate, hm, o, p['hy_bias'][o], tables)
    return z


def _route(aff):
    n = aff.shape[0]
    cap = (CAPACITY_FACTOR * n) // N_EXPERTS
    return lax.top_k(aff.T, cap)


def _mix_body(x_ref, ya_ref, yb_ref, yc_ref, g_ref, wa_ref, wb_ref, wc_ref, wo_ref, lg_ref, lb_ref, wr_ref,
              o_ref, ob_ref, aff_ref):
    g = jax.nn.sigmoid(g_ref[...])
    m = (g[:, :D_MODEL] * jnp.dot(ya_ref[...], wa_ref[...], preferred_element_type=F32)
         + g[:, D_MODEL:2 * D_MODEL] * jnp.dot(yb_ref[...].astype(BF16), wb_ref[...], preferred_element_type=F32)
         + g[:, 2 * D_MODEL:] * jnp.dot(yc_ref[...].astype(BF16), wc_ref[...], preferred_element_type=F32))
    h = ALPHA * x_ref[...] + jnp.dot(m.astype(BF16), wo_ref[...], preferred_element_type=F32)
    y = _layer_norm(h, lg_ref[...], lb_ref[...])
    o_ref[...] = y
    ob_ref[...] = y.astype(BF16)
    logits = jnp.dot(y, wr_ref[...], precision=lax.Precision.HIGHEST, preferred_element_type=F32)
    e = jnp.exp(logits - jnp.max(logits, -1, keepdims=True))
    aff_ref[...] = e / jnp.sum(e, -1, keepdims=True)


def _mix_and_norm(x, y_a, y_b, y_c, proj, p, *, tm=512):
    n, d = x.shape
    assert n % tm == 0 and proj.shape[1] == 2 * N_BRANCH * d
    row = lambda w: pl.BlockSpec((tm, w), lambda i: (i, 0))
    full = lambda a: pl.BlockSpec(a.shape, lambda i: (0,) * a.ndim)
    wa, wb, wc, wo = (p[k].astype(BF16) for k in ('w_branch_a', 'w_branch_b', 'w_branch_c', 'w_out'))
    lg, lb = p['ln1_g'][None], p['ln1_b'][None]
    return pl.pallas_call(
        _mix_body,
        out_shape=[jax.ShapeDtypeStruct((n, d), F32), jax.ShapeDtypeStruct((n, d), BF16),
                   jax.ShapeDtypeStruct((n, N_EXPERTS), F32)],
        grid=(n // tm,),
        in_specs=[row(d), row(D_RWKV), row(D_NA), row(D_HY),
                  pl.BlockSpec((tm, N_BRANCH * d), lambda i: (i, 1)),
                  full(wa), full(wb), full(wc), full(wo), full(lg), full(lb), full(p['w_router'])],
        out_specs=[row(d), row(d), row(N_EXPERTS)],
        compiler_params=pltpu.CompilerParams(dimension_semantics=("parallel",)),
        name="mix_norm_route",
    )(x, y_a, y_b, y_c, proj, wa, wb, wc, wo, lg, lb, p['w_router'])


def _mixers(x, p):
    B, T, D = x.shape
    n = B * T
    proj2 = _matmul(x.reshape(n, D), p['w_in'].astype(BF16))
    proj = proj2.reshape(B, T, -1)
    x1c, x2c, vc, kb, vb = _split_projection(proj, p['hy_short_w'], p['hy_short_b'])
    y_a = _rwkv7_mixer(x, proj, p)
    y_b = _neighbourhood_attention_pallas(proj, kb, vb, p['na_rpb'])
    y_c = _hyena_mixer(x1c, x2c, vc, p)
    return _mix_and_norm(x.reshape(n, D), y_a.reshape(n, -1), y_b.reshape(n, -1), y_c.reshape(n, -1), proj2, p)


def _encoder_layer(xs, p, expert_weights, layer):
    mixed = [_mixers(x, p) for x in xs]
    routes = [_route(aff) for _, _, aff in mixed]
    xe = jnp.concatenate([jnp.take(xb, idx, axis=0) for (_, xb, _), (_, idx) in zip(mixed, routes)], axis=1)
    gate = jnp.concatenate([g for g, _ in routes], axis=1)[..., None]
    ye = _expert_ffn(xe, gate, *expert_weights, layer)
    outs, off = [], 0
    for x, (x1, _, _), (_, idx) in zip(xs, mixed, routes):
        cap = idx.shape[1]
        ffn = jnp.zeros_like(x1).at[idx.reshape(-1)].add(ye[:, off:off + cap].reshape(-1, x1.shape[-1]))
        off += cap
        outs.append(_ln(x1, p['ln2_g'], p['ln2_b'], residual=ffn).reshape(x.shape))
    return outs


def kernel(x_prompt, x_sample, ln_in_g, ln_in_b, w_in, rwkv_mu_rkv, rwkv_mu_x, rwkv_w0, rwkv_w1, rwkv_w2,
           rwkv_a0, rwkv_a1, rwkv_a2, rwkv_g1, rwkv_g2, rwkv_k_k, rwkv_k_a, rwkv_r_k, rwkv_lnx_g, rwkv_lnx_b,
           na_rpb, hy_short_w, hy_short_b, hy_w1, hy_b1, hy_w2, hy_b2, hy_w3, hy_b3, hy_w4, hy_freq, hy_bias,
           w_branch_a, w_branch_b, w_branch_c, w_out, ln1_g, ln1_b, w_router, w_exp_gate, w_exp_up,
           w_exp_down, ln2_g, ln2_b):
    stacked = {
        'w_in': w_in, 'rwkv_mu_rkv': rwkv_mu_rkv, 'rwkv_mu_x': rwkv_mu_x, 'rwkv_w0': rwkv_w0,
        'rwkv_w1': rwkv_w1, 'rwkv_w2': rwkv_w2, 'rwkv_a0': rwkv_a0, 'rwkv_a1': rwkv_a1, 'rwkv_a2': rwkv_a2,
        'rwkv_g1': rwkv_g1, 'rwkv_g2': rwkv_g2, 'rwkv_k_k': rwkv_k_k, 'rwkv_k_a': rwkv_k_a,
        'rwkv_r_k': rwkv_r_k, 'rwkv_lnx_g': rwkv_lnx_g, 'rwkv_lnx_b': rwkv_lnx_b, 'na_rpb': na_rpb,
        'hy_short_w': hy_short_w, 'hy_short_b': hy_short_b, 'hy_w1': hy_w1, 'hy_b1': hy_b1,
        'hy_w2': hy_w2, 'hy_b2': hy_b2, 'hy_w3': hy_w3, 'hy_b3': hy_b3, 'hy_w4': hy_w4,
        'hy_freq': hy_freq, 'hy_bias': hy_bias, 'w_branch_a': w_branch_a, 'w_branch_b': w_branch_b,
        'w_branch_c': w_branch_c, 'w_out': w_out, 'ln1_g': ln1_g, 'ln1_b': ln1_b, 'w_router': w_router,
        'w_exp_gate': w_exp_gate, 'w_exp_up': w_exp_up, 'w_exp_down': w_exp_down,
        'ln2_g': ln2_g, 'ln2_b': ln2_b,
    }
    xs = [_ln(x.reshape(-1, D_MODEL), ln_in_g, ln_in_b).reshape(x.shape) for x in (x_prompt, x_sample)]
    expert_weights = [stacked.pop(name) for name in ('w_exp_gate', 'w_exp_up', 'w_exp_down')]
    for l in range(DEPTH):
        xs = _encoder_layer(xs, {name: arr[l] for name, arr in stacked.items()}, expert_weights, l)
    return tuple(xs)
```

```python
import functools
import math

import jax
import jax.numpy as jnp
import numpy as np
from jax import lax
from jax.experimental import pallas as pl
from jax.experimental.pallas import tpu as pltpu

D_MODEL = 1024
DEPTH = 2
GRID_W = 64
HEAD_DIM = 64
D_RWKV = D_MODEL // 2
RWKV_HEADS = D_RWKV // HEAD_DIM
D_NA = D_MODEL // 4
NA_HEADS = D_NA // HEAD_DIM
D_HY = D_MODEL // 4
N_BRANCH = 3
GN_EPS = 64e-5
NA_WIN_ROWS = 8
NA_WIN_COLS = 16
HY_ORDER = 2
HY_EMB = 33
HY_TOL = 1e-2
HY_FAST_PCT = 0.3
HY_SLOW_PCT = 1.5
N_EXPERTS = 16
CAPACITY_FACTOR = 2
ALPHA = (2 * DEPTH) ** 0.25
LN_EPS = 1e-5

F32 = jnp.float32
BF16 = jnp.bfloat16
EXPERT_FFN_VMEM_BYTES = 52 * 1024 * 1024


def _mm_body(x_ref, w_ref, o_ref):
    o_ref[...] = jnp.dot(x_ref[...].astype(BF16), w_ref[...], preferred_element_type=F32).astype(o_ref.dtype)


def _matmul(x, w, *, tm=2048, tn=512, out_dtype=F32):
    m, k = x.shape
    _, n = w.shape
    tm = min(tm, m)
    tn = min(tn, n)
    assert m % tm == 0 and n % tn == 0
    return pl.pallas_call(
        _mm_body,
        out_shape=jax.ShapeDtypeStruct((m, n), out_dtype),
        grid=(m // tm, n // tn),
        in_specs=[pl.BlockSpec((tm, k), lambda i, j: (i, 0)),
                  pl.BlockSpec((k, tn), lambda i, j: (0, j))],
        out_specs=pl.BlockSpec((tm, tn), lambda i, j: (i, j)),
        name="dense_matmul",
    )(x, w)


def _expert_ffn_body(x_ref, wg_ref, wu_ref, wd_ref, gate_ref, o_ref, acc_ref):
    f = pl.program_id(2)

    @pl.when(f == 0)
    def _():
        acc_ref[...] = jnp.zeros_like(acc_ref)

    xb = x_ref[0]
    hg = jnp.dot(xb, wg_ref[0, 0].astype(BF16), preferred_element_type=F32)
    hu = jnp.dot(xb, wu_ref[0, 0].astype(BF16), preferred_element_type=F32)
    h = (hg * jax.nn.sigmoid(hg) * hu).astype(BF16)
    acc_ref[...] += jnp.dot(h, wd_ref[0, 0].astype(BF16), preferred_element_type=F32)

    @pl.when(f == pl.num_programs(2) - 1)
    def _():
        o_ref[0] = acc_ref[...] * gate_ref[0]


def _expert_ffn(xe, gate, w_gate, w_up, w_down, layer, *, tm=2048, tf=256):
    e, m, d = xe.shape
    ff = w_gate.shape[-1]
    tm = min(tm, m)
    assert m % tm == 0 and ff % tf == 0
    return pl.pallas_call(
        _expert_ffn_body,
        out_shape=jax.ShapeDtypeStruct((e, m, d), F32),
        grid=(e, m // tm, ff // tf),
        in_specs=[pl.BlockSpec((1, tm, d), lambda g, i, f: (g, i, 0)),
                  pl.BlockSpec((1, 1, d, tf), lambda g, i, f: (layer, g, 0, f)),
                  pl.BlockSpec((1, 1, d, tf), lambda g, i, f: (layer, g, 0, f)),
                  pl.BlockSpec((1, 1, tf, d), lambda g, i, f: (layer, g, f, 0)),
                  pl.BlockSpec((1, tm, 1), lambda g, i, f: (g, i, 0))],
        out_specs=pl.BlockSpec((1, tm, d), lambda g, i, f: (g, i, 0)),
        scratch_shapes=[pltpu.VMEM((tm, d), F32)],
        compiler_params=pltpu.CompilerParams(
            dimension_semantics=("parallel", "parallel", "arbitrary"), vmem_limit_bytes=EXPERT_FFN_VMEM_BYTES),
        name="expert_ffn",
    )(xe, w_gate, w_up, w_down, gate)


def _layer_norm(h, g, b):
    hc = h - jnp.mean(h, -1, keepdims=True)
    var = jnp.mean(hc * hc, -1, keepdims=True)
    return hc * lax.rsqrt(var + LN_EPS) * g + b


def _ln_body(x_ref, g_ref, b_ref, o_ref):
    o_ref[...] = _layer_norm(x_ref[...], g_ref[...], b_ref[...])


def _ln_residual_body(x_ref, r_ref, g_ref, b_ref, o_ref):
    o_ref[...] = _layer_norm(ALPHA * x_ref[...] + r_ref[...], g_ref[...], b_ref[...])


def _ln(x, g, b, residual=None, *, tm=1024):
    n, d = x.shape
    tm = min(tm, n)
    assert n % tm == 0
    row = pl.BlockSpec((tm, d), lambda i: (i, 0))
    vec = pl.BlockSpec((1, d), lambda i: (0, 0))
    args = (x,) if residual is None else (x, residual)
    return pl.pallas_call(
        _ln_body if residual is None else _ln_residual_body,
        out_shape=jax.ShapeDtypeStruct((n, d), F32),
        grid=(n // tm,),
        in_specs=[row] * len(args) + [vec, vec],
        out_specs=row,
        compiler_params=pltpu.CompilerParams(dimension_semantics=("parallel",)),
        name="layer_norm",
    )(*args, g[None], b[None])


RWKV_SUB = 16
RWKV_PAIRS = 2
LANES = 128
RWKV_PIECES = 3
RWKV_GAMMA_SLOT = 2 * RWKV_PIECES * RWKV_SUB


def _rwkv_selectors():
    z = np.zeros((RWKV_SUB + 1, LANES, LANES), np.float32)
    for h in range(LANES // HEAD_DIM):
        lanes = slice(h * HEAD_DIM, (h + 1) * HEAD_DIM)
        for p in range(RWKV_PIECES):
            for t in range(RWKV_SUB):
                z[t, (h * RWKV_PIECES + p) * RWKV_SUB + t, lanes] = 1.0
            z[RWKV_SUB, RWKV_GAMMA_SLOT + 8 * h + p, lanes] = 1.0
    return jnp.asarray(z, BF16)


def _split3(x):
    hi = x.astype(BF16).astype(F32)
    rem = x - hi
    mid = rem.astype(BF16).astype(F32)
    lo = (rem - mid).astype(BF16).astype(F32)
    return [hi, mid, lo]


def _operand_tile(xa, xb, extra):
    pa = jnp.concatenate(_split3(xa), axis=0)
    pb = pa if xb is None else jnp.concatenate(_split3(xb), axis=0)
    lane_lo = lax.broadcasted_iota(jnp.int32, pa.shape, 1) < HEAD_DIM
    head0 = jnp.where(lane_lo, pa, pltpu.roll(pb, HEAD_DIM, 1))
    head1 = jnp.where(lane_lo, pltpu.roll(pa, HEAD_DIM, 1), pb)
    tt = jnp.concatenate([head0, head1, extra], axis=0).T
    return tt[:HEAD_DIM].astype(BF16), tt[HEAD_DIM:].astype(BF16)


def _rwkv_prepare(refs, base, lanes, reverse):
    a_ref, b_ref, k_ref, r_ref, g_ref = refs
    rows = pl.ds(base, RWKV_SUB)
    last = 0 if reverse else RWKV_SUB - 1
    g_last = g_ref[0, 0, rows, lanes][last:last + 1]
    gam = jnp.concatenate(_split3(g_last) + [jnp.zeros((8 - RWKV_PIECES, LANES), F32)], axis=0)
    extra = jnp.concatenate([gam, pltpu.roll(gam, HEAD_DIM, 1), jnp.zeros((16, LANES), F32)], axis=0)
    g_a, g_b = _operand_tile(a_ref[0, 0, rows, lanes], b_ref[0, 0, rows, lanes], extra)
    g_k, _ = _operand_tile(k_ref[0, 0, rows, lanes], None, jnp.zeros((32, LANES), F32))
    rg = r_ref[0, 0, rows, lanes]
    pad = jnp.zeros((HEAD_DIM - RWKV_SUB, LANES), F32)
    r_t = jnp.concatenate([rg, pad, pltpu.roll(rg, HEAD_DIM, 1), pad], axis=0).T[:HEAD_DIM]
    return jnp.concatenate([g_a, g_b, g_k], axis=0), r_t


def _rwkv_steps(streams, z_ref, s_ref):
    head_lane = (lax.broadcasted_iota(jnp.int32, (HEAD_DIM, LANES), 1) // HEAD_DIM) * HEAD_DIM
    lanes = [slice(q * LANES, (q + 1) * LANES) for _, _, _, _, _, q, _, _ in streams]
    vs = [s[2][0, pl.ds(s[6], RWKV_SUB), ln] for s, ln in zip(streams, lanes)]
    sts = [s_ref[d, q] for _, _, _, _, d, q, _, _ in streams]
    ys = [[None] * RWKV_SUB for _ in streams]
    gammas = [jnp.dot(s[0][:HEAD_DIM], z_ref[RWKV_SUB], preferred_element_type=F32) for s in streams]
    for p in range(RWKV_SUB // 2):
        cols = []
        for g_all, _, _, _, _, _, _, reverse in streams:
            ts = (RWKV_SUB - 1 - 2 * p, RWKV_SUB - 2 - 2 * p) if reverse else (2 * p, 2 * p + 1)
            sel = jnp.concatenate([z_ref[ts[0]], z_ref[ts[1]]], axis=1)
            cols.append((ts, jnp.dot(g_all, sel, preferred_element_type=F32)))
        for n in range(2):
            for i, (_, r_t, _, _, _, _, _, _) in enumerate(streams):
                ts, c = cols[i]
                t = ts[n]
                a_c, b_c, k_c = (c[o * HEAD_DIM:(o + 1) * HEAD_DIM, n * LANES:(n + 1) * LANES] for o in range(3))
                r_c = jnp.take_along_axis(r_t, head_lane + t, axis=1)
                sa = jnp.sum(sts[i] * a_c, axis=0, keepdims=True)
                sts[i] = sts[i] + b_c * sa + k_c * vs[i][t:t + 1, :]
                ys[i][t] = jnp.sum(sts[i] * r_c, axis=0, keepdims=True)
    for i, (_, _, _, y_ref, d, q, base, _) in enumerate(streams):
        s_ref[d, q] = sts[i] * gammas[i]
        y_ref[0, 0, pl.ds(base, RWKV_SUB), lanes[i]] = jnp.concatenate(ys[i], axis=0)


def _rwkv_body(af, bf, kf, rf, gf, vf, ab, bb, kb, rb, gb, vb, z_ref, yf_ref, yb_ref, s_ref, g_ref,
               rt_ref, *, tb):
    @pl.when(pl.program_id(2) == 0)
    def _():
        s_ref[...] = jnp.zeros_like(s_ref)

    nsub = tb // RWKV_SUB
    refs_f = (af, bf, kf, rf, gf)
    refs_b = (ab, bb, kb, rb, gb)

    def base_f(c):
        return pl.multiple_of(c * RWKV_SUB, RWKV_SUB)

    def base_b(c):
        return pl.multiple_of((nsub - 1 - c) * RWKV_SUB, RWKV_SUB)

    def prepare(slot, c):
        for q in range(RWKV_PAIRS):
            lanes = slice(q * LANES, (q + 1) * LANES)
            g_ref[slot, 0, q], rt_ref[slot, 0, q] = _rwkv_prepare(refs_f, base_f(c), lanes, False)
            g_ref[slot, 1, q], rt_ref[slot, 1, q] = _rwkv_prepare(refs_b, base_b(c), lanes, True)

    prepare(0, 0)

    def sub(c, carry):
        slot = c % 2
        nxt = jnp.minimum(c + 1, nsub - 1)
        streams = []
        for q in range(RWKV_PAIRS):
            streams.append((g_ref[slot, 0, q], rt_ref[slot, 0, q], vf, yf_ref, 0, q, base_f(c), False))
            streams.append((g_ref[slot, 1, q], rt_ref[slot, 1, q], vb, yb_ref, 1, q, base_b(c), True))
        prepare(1 - slot, nxt)
        _rwkv_steps(streams, z_ref, s_ref)
        return carry

    lax.fori_loop(0, nsub, sub, 0)


def _rwkv7_scan(a, b, k, r, g, v, *, tb=512, interpret=False):
    B, T, C = v.shape
    tb = min(tb, T)
    nt = T // tb
    width = RWKV_PAIRS * LANES
    assert T % tb == 0 and tb % RWKV_SUB == 0 and C % width == 0
    fwd3 = pl.BlockSpec((1, tb, width), lambda b, j, i: (b, i, j))
    bwd3 = pl.BlockSpec((1, tb, width), lambda b, j, i: (b, nt - 1 - i, j))
    fwd4 = pl.BlockSpec((1, 1, tb, width), lambda b, j, i: (0, b, i, j))
    bwd4 = pl.BlockSpec((1, 1, tb, width), lambda b, j, i: (1, b, nt - 1 - i, j))
    yf, yb = pl.pallas_call(
        functools.partial(_rwkv_body, tb=tb),
        out_shape=[jax.ShapeDtypeStruct((1, B, T, C), F32)] * 2,
        grid=(B, C // width, nt),
        in_specs=[fwd4] * 5 + [fwd3] + [bwd4] * 5 + [bwd3]
        + [pl.BlockSpec((RWKV_SUB + 1, LANES, LANES), lambda b, j, i: (0, 0, 0))],
        out_specs=[pl.BlockSpec((1, 1, tb, width), lambda b, j, i: (0, b, i, j)),
                   pl.BlockSpec((1, 1, tb, width), lambda b, j, i: (0, b, nt - 1 - i, j))],
        scratch_shapes=[pltpu.VMEM((2, RWKV_PAIRS, HEAD_DIM, LANES), F32),
                        pltpu.VMEM((2, 2, RWKV_PAIRS, 3 * HEAD_DIM, LANES), BF16),
                        pltpu.VMEM((2, 2, RWKV_PAIRS, HEAD_DIM, LANES), F32)],
        compiler_params=pltpu.CompilerParams(dimension_semantics=("parallel", "parallel", "arbitrary")),
        name="rwkv7_scan",
        interpret=interpret,
    )(a, b, k, r, g, v, a, b, k, r, g, v, _rwkv_selectors())
    return yf, yb


HALO = 8


def _head_sums(x, bd_ref):
    bd = bd_ref[...]
    return sum(jnp.dot(piece.astype(BF16), bd, preferred_element_type=F32) for piece in _split3(x))


def _token_shift(cur, prev_row, next_row, mu):
    tb = cur.shape[0]
    rid = lax.broadcasted_iota(jnp.int32, cur.shape, 0)
    prev = jnp.where(rid == 0, prev_row, pltpu.roll(cur, 1, 0))
    nxt = jnp.where(rid == tb - 1, next_row, pltpu.roll(cur, tb - 1, 0))
    return cur + mu[0:1] * (prev - cur) + mu[1:2] * (nxt - cur)


def _rwkv_prep_body(x_ref, xp_ref, xn_ref, r_ref, rp_ref, rn_ref, k_ref, kp_ref, kn_ref, v_ref, vp_ref, vn_ref,
                    mux_ref, mur_ref, w1_ref, w2_ref, w0_ref, a1_ref, a2_ref, a0_ref, g1_ref, g2_ref,
                    kk_w_ref, ka_w_ref, bd_ref, tri_ref,
                    ro_ref, vo_ref, go_ref, kd_ref, sa_ref, sb_ref, sk_ref, sr_ref, sg_ref):
    i = pl.program_id(1)
    first = i == 0
    last = i == pl.num_programs(1) - 1

    def shifted(cur_ref, p_ref, n_ref, mu):
        prev_row = jnp.where(first, 0.0, p_ref[0, HALO - 1:HALO, :])
        next_row = jnp.where(last, 0.0, n_ref[0, 0:1, :])
        return _token_shift(cur_ref[0], prev_row, next_row, mu)

    r = shifted(r_ref, rp_ref, rn_ref, mur_ref[0])
    k = shifted(k_ref, kp_ref, kn_ref, mur_ref[1])
    v = shifted(v_ref, vp_ref, vn_ref, mur_ref[2])
    xw = shifted(x_ref, xp_ref, xn_ref, mux_ref[0]).astype(BF16)
    xa = shifted(x_ref, xp_ref, xn_ref, mux_ref[1]).astype(BF16)
    xg = shifted(x_ref, xp_ref, xn_ref, mux_ref[2]).astype(BF16)
    hg = jax.nn.sigmoid(jnp.dot(xg, g1_ref[...], preferred_element_type=F32)).astype(BF16)
    go_ref[0] = jnp.dot(hg, g2_ref[...], preferred_element_type=F32)
    kk = k * kk_w_ref[...]
    kk = kk / jnp.maximum(jnp.sqrt(_head_sums(kk * kk, bd_ref)), 1e-12)
    ro_ref[0] = r
    vo_ref[0] = v
    for z in range(2):
        hw = jnp.tanh(jnp.dot(xw, w1_ref[z], preferred_element_type=F32)).astype(BF16)
        ha = jnp.dot(xa, a1_ref[z], preferred_element_type=F32).astype(BF16)
        w_raw = w0_ref[z:z + 1] + jnp.dot(hw, w2_ref[z], preferred_element_type=F32)
        lw = -math.exp(-0.5) * jax.nn.sigmoid(w_raw)
        a = jax.nn.sigmoid(a0_ref[z:z + 1]
                           + jnp.dot(ha, a2_ref[z], preferred_element_type=F32))
        kd = k * (1.0 + (a - 1.0) * ka_w_ref[...])
        kd_ref[z, 0] = kd
        cum = sum(jnp.dot(tri_ref[z], piece.astype(BF16), preferred_element_type=F32) for piece in _split3(lw))
        gamma = jnp.exp(cum)
        inv_gamma = jnp.exp(-cum)
        sa_ref[z, 0] = -kk * jnp.exp(cum - lw)
        sb_ref[z, 0] = kk * a * inv_gamma
        sk_ref[z, 0] = kd * inv_gamma
        sr_ref[z, 0] = r * gamma
        sg_ref[z, 0] = gamma


def _rwkv_post_body(yf_ref, yb_ref, r_ref, v_ref, kd_ref, g_ref, lng_ref, lnb_ref, rk_ref, bd_ref, o_ref):
    y = yf_ref[0, 0] + yb_ref[0, 0]
    inv_n = 1.0 / HEAD_DIM
    yc = y - _head_sums(y, bd_ref) * inv_n
    yn = yc * lax.rsqrt(_head_sums(yc * yc, bd_ref) * inv_n + GN_EPS) * lng_ref[...] + lnb_ref[...]
    r, v = r_ref[0], v_ref[0]
    bonus = (_head_sums(r * kd_ref[0, 0] * rk_ref[...], bd_ref)
             + _head_sums(r * kd_ref[1, 0] * rk_ref[...], bd_ref)) * v
    o_ref[0] = ((yn + bonus) * g_ref[0]).astype(o_ref.dtype)


def _rwkv7_mixer(u, proj, p, *, tb=256):
    B, T, D = u.shape
    C = D_RWKV
    tb = min(tb, T)
    nt = T // tb
    assert T % tb == 0 and tb % HALO == 0
    hb = tb // HALO

    def cur(width, col):
        return pl.BlockSpec((1, tb, width), lambda b, i: (b, i, col))

    def prev(width, col):
        return pl.BlockSpec((1, HALO, width), lambda b, i: (b, jnp.maximum(i * hb - 1, 0), col))

    def nxt(width, col):
        return pl.BlockSpec((1, HALO, width), lambda b, i: (b, jnp.minimum((i + 1) * hb, T // HALO - 1), col))

    def full(a):
        return pl.BlockSpec(a.shape, lambda b, i: (0,) * a.ndim)

    lane = np.arange(C) // HEAD_DIM
    bd = jnp.asarray(lane[:, None] == lane[None, :], BF16)
    t_id = np.arange(tb)
    same = (t_id[:, None] // RWKV_SUB) == (t_id[None, :] // RWKV_SUB)
    tri = jnp.asarray(np.stack([same & (t_id[None, :] <= t_id[:, None]),
                                same & (t_id[None, :] >= t_id[:, None])]), BF16)
    weights = [p['rwkv_mu_x'], p['rwkv_mu_rkv'], p['rwkv_w1'].astype(BF16), p['rwkv_w2'].astype(BF16), p['rwkv_w0'],
               p['rwkv_a1'].astype(BF16), p['rwkv_a2'].astype(BF16), p['rwkv_a0'], p['rwkv_g1'].astype(BF16), p['rwkv_g2'].astype(BF16),
               p['rwkv_k_k'][None], p['rwkv_k_a'][None], bd, tri]
    one = jax.ShapeDtypeStruct((B, T, C), F32)
    two = jax.ShapeDtypeStruct((2, B, T, C), F32)
    out1 = pl.BlockSpec((1, tb, C), lambda b, i: (b, i, 0))
    out2 = pl.BlockSpec((2, 1, tb, C), lambda b, i: (0, b, i, 0))
    r, v, g, kd, sa, sb, sk, sr, sg = pl.pallas_call(
        _rwkv_prep_body,
        out_shape=[one, one, one, two, two, two, two, two, two],
        grid=(B, nt),
        in_specs=[cur(D, 0), prev(D, 0), nxt(D, 0)]
        + [spec(C, col) for col in range(3) for spec in (cur, prev, nxt)]
        + [full(w) for w in weights],
        out_specs=[out1, out1, out1, out2, out2, out2, out2, out2, out2],
        compiler_params=pltpu.CompilerParams(dimension_semantics=("parallel", "parallel")),
        name="rwkv7_prepare",
    )(u, u, u, *([proj] * 9), *weights)
    yf, yb = _rwkv7_scan(sa, sb, sk, sr, sg, v)
    ydir = pl.BlockSpec((1, 1, tb, C), lambda b, i: (0, b, i, 0))
    post_w = [p['rwkv_lnx_g'][None], p['rwkv_lnx_b'][None], p['rwkv_r_k'].reshape(1, C), bd]
    return pl.pallas_call(
        _rwkv_post_body,
        out_shape=jax.ShapeDtypeStruct((B, T, C), BF16),
        grid=(B, nt),
        in_specs=[ydir, ydir, out1, out1, out2, out1] + [full(w) for w in post_w],
        out_specs=out1,
        compiler_params=pltpu.CompilerParams(dimension_semantics=("parallel", "parallel")),
        name="rwkv7_output",
    )(yf, yb, r, v, kd, g, *post_w)


NA_ROWS_PER_STEP = 8


def _na_body(q_ref, k_ref, v_ref, bias_ref, mask_ref, o_ref, *, rows):
    kr = NA_WIN_ROWS
    win = kr * GRID_W
    valid = mask_ref[...] != 0
    lane = lax.broadcasted_iota(jnp.int32, (GRID_W, LANES), 1)
    for ii in range(NA_ROWS_PER_STEP):
        i = pl.program_id(1) * NA_ROWS_PER_STEP + ii
        start = jnp.clip(i - kr // 2, 0, rows - kr)
        d = i - start
        krows = pl.ds(pl.multiple_of(start * GRID_W, GRID_W), win)
        for pair in range(D_NA // LANES):
            lanes = slice(pair * LANES, (pair + 1) * LANES)
            q2 = q_ref[0, ii * GRID_W:(ii + 1) * GRID_W, lanes] * (HEAD_DIM ** -0.5)
            k2 = k_ref[0, krows, lanes]
            v2 = v_ref[0, krows, lanes]
            outs = []
            for hh in range(LANES // HEAD_DIM):
                in_head = (lane >= hh * HEAD_DIM) & (lane < (hh + 1) * HEAD_DIM)
                qh = jnp.where(in_head, q2, 0.0).astype(BF16)
                s = lax.dot_general(qh, k2, (((1,), (1,)), ((), ())), preferred_element_type=F32)
                s = s + bias_ref[d, pair * (LANES // HEAD_DIM) + hh]
                s = jnp.where(valid, s, -1e30)
                m = jnp.max(s, axis=-1, keepdims=True)
                e = jnp.exp(s - m)
                p = e / jnp.sum(e, axis=-1, keepdims=True)
                outs.append(jnp.dot(p.astype(BF16), v2, preferred_element_type=F32))
            o_ref[0, ii * GRID_W:(ii + 1) * GRID_W, lanes] = jnp.where(lane < HEAD_DIM, outs[0], outs[1])


def _neighbourhood_attention_pallas(proj, k, v, rpb, *, interpret=False):
    B, T, _ = proj.shape
    q_col = 3 * D_RWKV // D_NA
    rows = T // GRID_W
    kr, kc = NA_WIN_ROWS, NA_WIN_COLS
    assert rows >= kr and rows % NA_ROWS_PER_STEP == 0
    ci = jnp.arange(GRID_W)
    col_start = jnp.clip(ci - kc // 2, 0, GRID_W - kc)
    col_valid = (ci[None] >= col_start[:, None]) & (ci[None] < col_start[:, None] + kc)
    mask = jnp.tile(col_valid.astype(jnp.int32), (1, kr))
    dc_idx = jnp.clip(ci[None] - ci[:, None] + kc - 1, 0, 2 * kc - 2)
    dr_idx = jnp.arange(kr)[None, :] - jnp.arange(kr)[:, None] + kr - 1
    onehot = (jnp.arange(2 * kc - 1)[:, None] == dc_idx.reshape(1, -1)).astype(F32)
    bias = jnp.dot(rpb[:, dr_idx].reshape(-1, 2 * kc - 1), onehot, precision=lax.Precision.HIGHEST)
    bias = bias.reshape(NA_HEADS, kr, kr, GRID_W, GRID_W)
    bias = jnp.transpose(bias, (1, 0, 3, 2, 4)).reshape(kr, NA_HEADS, GRID_W, kr * GRID_W)
    tq = NA_ROWS_PER_STEP * GRID_W
    return pl.pallas_call(
        functools.partial(_na_body, rows=rows),
        out_shape=jax.ShapeDtypeStruct((B, T, D_NA), F32),
        grid=(B, rows // NA_ROWS_PER_STEP),
        in_specs=[pl.BlockSpec((1, tq, D_NA), lambda b, i: (b, i, q_col)),
                  pl.BlockSpec((1, T, D_NA), lambda b, i: (b, 0, 0)),
                  pl.BlockSpec((1, T, D_NA), lambda b, i: (b, 0, 0)),
                  pl.BlockSpec((kr, NA_HEADS, GRID_W, kr * GRID_W), lambda b, i: (0, 0, 0, 0)),
                  pl.BlockSpec((GRID_W, kr * GRID_W), lambda b, i: (0, 0))],
        out_specs=pl.BlockSpec((1, tq, D_NA), lambda b, i: (b, i, 0)),
        compiler_params=pltpu.CompilerParams(dimension_semantics=("parallel", "arbitrary")),
        name="neighbourhood_attention",
        interpret=interpret,
    )(proj, k, v, bias, mask)


def _hyena_filters(L, p):
    t = jnp.linspace(0.0, 1.0, L, dtype=F32)[:, None]
    n_bands = (HY_EMB - 1) // 2
    omega = 2.0 * math.pi * jnp.arange(L, dtype=F32)[:, None] / L
    bands = jnp.linspace(1e-4, n_bands - 1, n_bands, dtype=F32)[None]
    z = jnp.concatenate([t, jnp.cos(bands * omega), -jnp.sin(bands * omega)], -1)
    freq = p['hy_freq']
    h = jnp.sin(freq * (z @ p['hy_w1'] + p['hy_b1']))
    h = jnp.sin(freq * (h @ p['hy_w2'] + p['hy_b2']))
    h = jnp.sin(freq * (h @ p['hy_w3'] + p['hy_b3']))
    h = (h @ p['hy_w4']).reshape(L, HY_ORDER, 2, D_HY)
    deltas = jnp.abs(jnp.linspace(math.log(HY_TOL) / HY_FAST_PCT, math.log(HY_TOL) / HY_SLOW_PCT, D_HY, dtype=F32))
    h = h * jnp.exp(-t * deltas)[:, None, None, :]
    first_row = lax.broadcasted_iota(jnp.int32, (L, 1, 1, 1), 0) == 0
    halves = jnp.where(first_row & (lax.broadcasted_iota(jnp.int32, (1, 1, 2, 1), 2) == 1), 0.0, h)
    norm = jnp.sum(jnp.abs(halves), axis=(0, 2), keepdims=True)
    return jnp.transpose(halves / norm, (0, 2, 1, 3))


def _dft_tables(R):
    n = np.arange(R)
    ang = (2.0 * np.pi / R) * ((n[:, None] * n[None, :]) % R)
    tw = (2.0 * np.pi / (R * R)) * (n[:, None] * n[None, :])
    c, s, ct, st = (jnp.asarray(f(a), F32) for a in (ang, tw) for f in (np.cos, np.sin))
    first = jnp.concatenate([c, -s], axis=0)
    last = jnp.concatenate([c, -s], axis=1)[:R // 2]
    cp = c[None] * ct[:, None, :] - s[None] * st[:, None, :]
    sp = s[None] * ct[:, None, :] + c[None] * st[:, None, :]
    fwd = jnp.concatenate([jnp.concatenate([cp, sp], axis=2), jnp.concatenate([-sp, cp], axis=2)], axis=1)
    cpt = c[None] * ct[:, :, None] - s[None] * st[:, :, None]
    spt = s[None] * ct[:, :, None] + c[None] * st[:, :, None]
    inv = jnp.concatenate([jnp.concatenate([cpt, -spt], axis=2), jnp.concatenate([spt, cpt], axis=2)], axis=1)
    return first.astype(BF16), fwd.astype(BF16), inv.astype(BF16), last.astype(BF16)


def _conv_first_body(f_ref, z_ref, o_ref):
    o_ref[0] = jnp.dot(f_ref[...], z_ref[0].astype(BF16), preferred_element_type=F32).astype(o_ref.dtype)


def _spectrum_mid_body(a_ref, mf_ref, o_ref, *, R, kb):
    c = o_ref.shape[-1]
    for j in range(kb):
        a = jnp.concatenate([a_ref[0, 0, j], a_ref[0, 1, j]], axis=0)
        x = jnp.dot(mf_ref[j], a, preferred_element_type=F32)
        o_ref[0, j] = x[:R, :c] + x[:R, c:]
        o_ref[1, j] = x[R:, :c] - x[R:, c:]


def _filter_spectrum(halves, tables, *, tn=4096, kb=4):
    L, c2 = halves.shape
    R = math.isqrt(2 * L)
    first, fwd, _, _ = tables
    tn = min(tn, R * c2)
    a = pl.pallas_call(
        _conv_first_body,
        out_shape=jax.ShapeDtypeStruct((1, 2 * R, R * c2), BF16),
        grid=(1, (R * c2) // tn),
        in_specs=[pl.BlockSpec((2 * R, R // 2), lambda b, j: (0, 0)),
                  pl.BlockSpec((1, R // 2, tn), lambda b, j: (b, 0, j))],
        out_specs=pl.BlockSpec((1, 2 * R, tn), lambda b, j: (b, 0, j)),
        name="filter_dft_first",
    )(first[:, :R // 2], halves.reshape(1, R // 2, R * c2))
    return pl.pallas_call(
        functools.partial(_spectrum_mid_body, R=R, kb=kb),
        out_shape=jax.ShapeDtypeStruct((2, R, R, c2 // 2), F32),
        grid=(R // kb,),
        in_specs=[pl.BlockSpec((1, 2, kb, R, c2), lambda k: (0, 0, k, 0, 0)),
                  pl.BlockSpec((kb, 2 * R, 2 * R), lambda k: (k, 0, 0))],
        out_specs=pl.BlockSpec((2, kb, R, c2 // 2), lambda k: (0, k, 0, 0)),
        name="filter_dft_mid",
    )(a.reshape(1, 2, R, R, c2), fwd)


def _conv_mid_body(a_ref, mf_ref, mi_ref, h_ref, o_ref, *, R, kb):
    for j in range(kb):
        a = jnp.concatenate([a_ref[0, 0, j], a_ref[0, 1, j]], axis=0)
        x = jnp.dot(mf_ref[j], a, preferred_element_type=F32)
        xr, xi = x[:R], x[R:]
        hr, hi = h_ref[0, j], h_ref[1, j]
        y = jnp.concatenate([xr * hr - xi * hi, xr * hi + xi * hr], axis=0).astype(BF16)
        b = jnp.dot(mi_ref[j], y, preferred_element_type=F32)
        o_ref[0, 0, j] = b[:R].astype(o_ref.dtype)
        o_ref[0, 1, j] = b[R:].astype(o_ref.dtype)


def _conv_last_body(f_ref, b_ref, z_ref, gate_ref, bias_ref, o_ref):
    y = jnp.dot(f_ref[...], b_ref[0], preferred_element_type=F32)
    o_ref[0] = gate_ref[0] * (y + z_ref[0] * bias_ref[...])


def _gated_long_conv(z, gate, hm, order, bias, tables, *, tn=4096, kb=4, interpret=False):
    B, L, C = z.shape
    R = math.isqrt(2 * L)
    assert R * R == 2 * L and R % (2 * kb) == 0
    first, fwd, inv, last = tables
    tn = min(tn, R * C)
    nt = (R * C) // tn
    z2 = z.reshape(B, R // 2, R * C)
    a = pl.pallas_call(
        _conv_first_body,
        out_shape=jax.ShapeDtypeStruct((B, 2 * R, R * C), BF16),
        grid=(B, nt),
        in_specs=[pl.BlockSpec((2 * R, R // 2), lambda b, j: (0, 0)),
                  pl.BlockSpec((1, R // 2, tn), lambda b, j: (b, 0, j))],
        out_specs=pl.BlockSpec((1, 2 * R, tn), lambda b, j: (b, 0, j)),
        name="long_conv_first",
        interpret=interpret,
    )(first[:, :R // 2], z2)
    bmid = pl.pallas_call(
        functools.partial(_conv_mid_body, R=R, kb=kb),
        out_shape=jax.ShapeDtypeStruct((B, 2, R, R, C), BF16),
        grid=(B, R // kb),
        in_specs=[pl.BlockSpec((1, 2, kb, R, C), lambda b, k: (b, 0, k, 0, 0)),
                  pl.BlockSpec((kb, 2 * R, 2 * R), lambda b, k: (k, 0, 0)),
                  pl.BlockSpec((kb, 2 * R, 2 * R), lambda b, k: (k, 0, 0)),
                  pl.BlockSpec((2, kb, R, C), lambda b, k: (0, k, 0, order))],
        out_specs=pl.BlockSpec((1, 2, kb, R, C), lambda b, k: (b, 0, k, 0, 0)),
        name="long_conv_mid",
        interpret=interpret,
    )(a.reshape(B, 2, R, R, C), fwd, inv, hm)
    out = pl.pallas_call(
        _conv_last_body,
        out_shape=jax.ShapeDtypeStruct((B, R // 2, R * C), F32),
        grid=(B, nt),
        in_specs=[pl.BlockSpec((R // 2, 2 * R), lambda b, j: (0, 0)),
                  pl.BlockSpec((1, 2 * R, tn), lambda b, j: (b, 0, j)),
                  pl.BlockSpec((1, R // 2, tn), lambda b, j: (b, 0, j)),
                  pl.BlockSpec((1, R // 2, tn), lambda b, j: (b, 0, j)),
                  pl.BlockSpec((1, tn), lambda b, j: (0, j))],
        out_specs=pl.BlockSpec((1, R // 2, tn), lambda b, j: (b, 0, j)),
        name="long_conv_last",
        interpret=interpret,
    )(last, bmid.reshape(B, 2 * R, R * C), z2, gate.reshape(B, R // 2, R * C), jnp.tile(bias, R)[None])
    return out.reshape(B, L, C)


def _short_conv_body(u_ref, up_ref, un_ref, nk_ref, nv_ref, sw_ref, sb_ref, x1_ref, x2_ref, v_ref, ko_ref, vo_ref):
    i = pl.program_id(1)
    cur = u_ref[0]
    tb = cur.shape[0]
    prev_row = jnp.where(i == 0, 0.0, up_ref[0, HALO - 1:HALO, :])
    next_row = jnp.where(i == pl.num_programs(1) - 1, 0.0, un_ref[0, 0:1, :])
    rid = lax.broadcasted_iota(jnp.int32, cur.shape, 0)
    prev = jnp.where(rid == 0, prev_row, pltpu.roll(cur, 1, 0))
    nxt = jnp.where(rid == tb - 1, next_row, pltpu.roll(cur, tb - 1, 0))
    u = prev * sw_ref[0:1] + cur * sw_ref[1:2] + nxt * sw_ref[2:3] + sb_ref[...]
    x1_ref[0] = u[:, :D_HY]
    x2_ref[0] = u[:, D_HY:2 * D_HY]
    v_ref[0] = u[:, 2 * D_HY:]
    ko_ref[0] = nk_ref[0].astype(BF16)
    vo_ref[0] = nv_ref[0].astype(BF16)


def _split_projection(proj, sw, sb, *, tb=512):
    B, T, n_in = proj.shape
    width = 3 * D_HY
    col = (3 * D_RWKV + 3 * D_NA) // width
    assert col * width == 3 * D_RWKV + 3 * D_NA
    tb = min(tb, T)
    hb = tb // HALO
    out = jax.ShapeDtypeStruct((B, T, D_HY), F32)
    out_na = jax.ShapeDtypeStruct((B, T, D_NA), BF16)
    ospec = pl.BlockSpec((1, tb, D_HY), lambda b, i: (b, i, 0))
    na_spec = pl.BlockSpec((1, tb, D_NA), lambda b, i: (b, i, 0))
    na_col = 3 * D_RWKV // D_NA
    return pl.pallas_call(
        _short_conv_body,
        out_shape=[out, out, out, out_na, out_na],
        grid=(B, T // tb),
        in_specs=[pl.BlockSpec((1, tb, width), lambda b, i: (b, i, col)),
                  pl.BlockSpec((1, HALO, width), lambda b, i: (b, jnp.maximum(i * hb - 1, 0), col)),
                  pl.BlockSpec((1, HALO, width), lambda b, i: (b, jnp.minimum((i + 1) * hb, T // HALO - 1), col)),
                  pl.BlockSpec((1, tb, D_NA), lambda b, i: (b, i, na_col + 1)),
                  pl.BlockSpec((1, tb, D_NA), lambda b, i: (b, i, na_col + 2)),
                  pl.BlockSpec(sw.shape, lambda b, i: (0, 0)),
                  pl.BlockSpec((1, width), lambda b, i: (0, 0))],
        out_specs=[ospec, ospec, ospec, na_spec, na_spec],
        compiler_params=pltpu.CompilerParams(dimension_semantics=("parallel", "parallel")),
        name="split_projection",
    )(proj, proj, proj, proj, proj, sw, sb[None])


def _hyena_mixer(x1, x2, v, p):
    T = v.shape[1]
    R = math.isqrt(2 * T)
    tables = _dft_tables(R)
    halves = _hyena_filters(T, p).reshape(T, 2 * HY_ORDER * D_HY) * (1.0 / (2 * T))
    hm = _filter_spectrum(halves, tables)
    z = v
    for o, gate in enumerate((x1, x2)):
        z = _gated_long_conv(z, gate, hm, o, p['hy_bias'][o], tables)
    return z


def _route(aff):
    n = aff.shape[0]
    cap = (CAPACITY_FACTOR * n) // N_EXPERTS
    return lax.top_k(aff.T, cap)


def _mix_body(x_ref, ya_ref, yb_ref, yc_ref, g_ref, wa_ref, wb_ref, wc_ref, wo_ref, lg_ref, lb_ref, wr_ref,
              o_ref, ob_ref, aff_ref):
    g = jax.nn.sigmoid(g_ref[...])
    m = (g[:, :D_MODEL] * jnp.dot(ya_ref[...], wa_ref[...], preferred_element_type=F32)
         + g[:, D_MODEL:2 * D_MODEL] * jnp.dot(yb_ref[...].astype(BF16), wb_ref[...], preferred_element_type=F32)
         + g[:, 2 * D_MODEL:] * jnp.dot(yc_ref[...].astype(BF16), wc_ref[...], preferred_element_type=F32))
    h = ALPHA * x_ref[...] + jnp.dot(m.astype(BF16), wo_ref[...], preferred_element_type=F32)
    y = _layer_norm(h, lg_ref[...], lb_ref[...])
    o_ref[...] = y
    ob_ref[...] = y.astype(BF16)
    logits = jnp.dot(y, wr_ref[...], precision=lax.Precision.HIGHEST, preferred_element_type=F32)
    e = jnp.exp(logits - jnp.max(logits, -1, keepdims=True))
    aff_ref[...] = e / jnp.sum(e, -1, keepdims=True)


def _mix_and_norm(x, y_a, y_b, y_c, proj, p, *, tm=512):
    n, d = x.shape
    assert n % tm == 0 and proj.shape[1] == 2 * N_BRANCH * d
    row = lambda w: pl.BlockSpec((tm, w), lambda i: (i, 0))
    full = lambda a: pl.BlockSpec(a.shape, lambda i: (0,) * a.ndim)
    wa, wb, wc, wo = (p[k].astype(BF16) for k in ('w_branch_a', 'w_branch_b', 'w_branch_c', 'w_out'))
    lg, lb = p['ln1_g'][None], p['ln1_b'][None]
    return pl.pallas_call(
        _mix_body,
        out_shape=[jax.ShapeDtypeStruct((n, d), F32), jax.ShapeDtypeStruct((n, d), BF16),
                   jax.ShapeDtypeStruct((n, N_EXPERTS), F32)],
        grid=(n // tm,),
        in_specs=[row(d), row(D_RWKV), row(D_NA), row(D_HY),
                  pl.BlockSpec((tm, N_BRANCH * d), lambda i: (i, 1)),
                  full(wa), full(wb), full(wc), full(wo), full(lg), full(lb), full(p['w_router'])],
        out_specs=[row(d), row(d), row(N_EXPERTS)],
        compiler_params=pltpu.CompilerParams(dimension_semantics=("parallel",)),
        name="mix_norm_route",
    )(x, y_a, y_b, y_c, proj, wa, wb, wc, wo, lg, lb, p['w_router'])


def _mixers(x, p):
    B, T, D = x.shape
    n = B * T
    proj2 = _matmul(x.reshape(n, D), p['w_in'].astype(BF16))
    proj = proj2.reshape(B, T, -1)
    x1c, x2c, vc, kb, vb = _split_projection(proj, p['hy_short_w'], p['hy_short_b'])
    y_a = _rwkv7_mixer(x, proj, p)
    y_b = _neighbourhood_attention_pallas(proj, kb, vb, p['na_rpb'])
    y_c = _hyena_mixer(x1c, x2c, vc, p)
    return _mix_and_norm(x.reshape(n, D), y_a.reshape(n, -1), y_b.reshape(n, -1), y_c.reshape(n, -1), proj2, p)


def _encoder_layer(xs, p, expert_weights, layer):
    mixed = [_mixers(x, p) for x in xs]
    routes = [_route(aff) for _, _, aff in mixed]
    xe = jnp.concatenate([jnp.take(xb, idx, axis=0) for (_, xb, _), (_, idx) in zip(mixed, routes)], axis=1)
    gate = jnp.concatenate([g for g, _ in routes], axis=1)[..., None]
    ye = _expert_ffn(xe, gate, *expert_weights, layer)
    outs, off = [], 0
    for x, (x1, _, _), (_, idx) in zip(xs, mixed, routes):
        cap = idx.shape[1]
        ffn = jnp.zeros_like(x1).at[idx.reshape(-1)].add(ye[:, off:off + cap].reshape(-1, x1.shape[-1]))
        off += cap
        outs.append(_ln(x1, p['ln2_g'], p['ln2_b'], residual=ffn).reshape(x.shape))
    return outs


def kernel(x_prompt, x_sample, ln_in_g, ln_in_b, w_in, rwkv_mu_rkv, rwkv_mu_x, rwkv_w0, rwkv_w1, rwkv_w2,
           rwkv_a0, rwkv_a1, rwkv_a2, rwkv_g1, rwkv_g2, rwkv_k_k, rwkv_k_a, rwkv_r_k, rwkv_lnx_g, rwkv_lnx_b,
           na_rpb, hy_short_w, hy_short_b, hy_w1, hy_b1, hy_w2, hy_b2, hy_w3, hy_b3, hy_w4, hy_freq, hy_bias,
           w_branch_a, w_branch_b, w_branch_c, w_out, ln1_g, ln1_b, w_router, w_exp_gate, w_exp_up,
           w_exp_down, ln2_g, ln2_b):
    stacked = {
        'w_in': w_in, 'rwkv_mu_rkv': rwkv_mu_rkv, 'rwkv_mu_x': rwkv_mu_x, 'rwkv_w0': rwkv_w0,
        'rwkv_w1': rwkv_w1, 'rwkv_w2': rwkv_w2, 'rwkv_a0': rwkv_a0, 'rwkv_a1': rwkv_a1, 'rwkv_a2': rwkv_a2,
        'rwkv_g1': rwkv_g1, 'rwkv_g2': rwkv_g2, 'rwkv_k_k': rwkv_k_k, 'rwkv_k_a': rwkv_k_a,
        'rwkv_r_k': rwkv_r_k, 'rwkv_lnx_g': rwkv_lnx_g, 'rwkv_lnx_b': rwkv_lnx_b, 'na_rpb': na_rpb,
        'hy_short_w': hy_short_w, 'hy_short_b': hy_short_b, 'hy_w1': hy_w1, 'hy_b1': hy_b1,
        'hy_w2': hy_w2, 'hy_b2': hy_b2, 'hy_w3': hy_w3, 'hy_b3': hy_b3, 'hy_w4': hy_w4,
        'hy_freq': hy_freq, 'hy_bias': hy_bias, 'w_branch_a': w_branch_a, 'w_branch_b': w_branch_b,
        'w_branch_c': w_branch_c, 'w_out': w_out, 'ln1_g': ln1_g, 'ln1_b': ln1_b, 'w_router': w_router,
        'w_exp_gate': w_exp_gate, 'w_exp_up': w_exp_up, 'w_exp_down': w_exp_down,
        'ln2_g': ln2_g, 'ln2_b': ln2_b,
    }
    xs = [_ln(x.reshape(-1, D_MODEL), ln_in_g, ln_in_b).reshape(x.shape) for x in (x_prompt, x_sample)]
    expert_weights = [stacked.pop(name) for name in ('w_exp_gate', 'w_exp_up', 'w_exp_down')]
    for l in range(DEPTH):
        xs = _encoder_layer(xs, {name: arr[l] for name, arr in stacked.items()}, expert_weights, l)
    return tuple(xs)
```

```python
import functools
import math

import jax
import jax.numpy as jnp
import numpy as np
from jax import lax
from jax.experimental import pallas as pl
from jax.experimental.pallas import tpu as pltpu

D_MODEL = 1024
DEPTH = 2
GRID_W = 64
HEAD_DIM = 64
D_RWKV = D_MODEL // 2
RWKV_HEADS = D_RWKV // HEAD_DIM
D_NA = D_MODEL // 4
NA_HEADS = D_NA // HEAD_DIM
D_HY = D_MODEL // 4
N_BRANCH = 3
GN_EPS = 64e-5
NA_WIN_ROWS = 8
NA_WIN_COLS = 16
HY_ORDER = 2
HY_EMB = 33
HY_TOL = 1e-2
HY_FAST_PCT = 0.3
HY_SLOW_PCT = 1.5
N_EXPERTS = 16
CAPACITY_FACTOR = 2
ALPHA = (2 * DEPTH) ** 0.25
LN_EPS = 1e-5

F32 = jnp.float32
BF16 = jnp.bfloat16
EXPERT_FFN_VMEM_BYTES = 52 * 1024 * 1024


def _mm_body(x_ref, w_ref, o_ref):
    o_ref[...] = jnp.dot(x_ref[...].astype(BF16), w_ref[...], preferred_element_type=F32).astype(o_ref.dtype)


def _matmul(x, w, *, tm=2048, tn=512, out_dtype=F32):
    m, k = x.shape
    _, n = w.shape
    tm = min(tm, m)
    tn = min(tn, n)
    assert m % tm == 0 and n % tn == 0
    return pl.pallas_call(
        _mm_body,
        out_shape=jax.ShapeDtypeStruct((m, n), out_dtype),
        grid=(m // tm, n // tn),
        in_specs=[pl.BlockSpec((tm, k), lambda i, j: (i, 0)),
                  pl.BlockSpec((k, tn), lambda i, j: (0, j))],
        out_specs=pl.BlockSpec((tm, tn), lambda i, j: (i, j)),
        name="dense_matmul",
    )(x, w)


def _expert_ffn_body(x_ref, wg_ref, wu_ref, wd_ref, gate_ref, o_ref, acc_ref):
    f = pl.program_id(2)

    @pl.when(f == 0)
    def _():
        acc_ref[...] = jnp.zeros_like(acc_ref)

    xb = x_ref[0]
    hg = jnp.dot(xb, wg_ref[0, 0].astype(BF16), preferred_element_type=F32)
    hu = jnp.dot(xb, wu_ref[0, 0].astype(BF16), preferred_element_type=F32)
    h = (hg * jax.nn.sigmoid(hg) * hu).astype(BF16)
    acc_ref[...] += jnp.dot(h, wd_ref[0, 0].astype(BF16), preferred_element_type=F32)

    @pl.when(f == pl.num_programs(2) - 1)
    def _():
        o_ref[0] = acc_ref[...] * gate_ref[0]


def _expert_ffn(xe, gate, w_gate, w_up, w_down, layer, *, tm=2048, tf=256):
    e, m, d = xe.shape
    ff = w_gate.shape[-1]
    tm = min(tm, m)
    assert m % tm == 0 and ff % tf == 0
    return pl.pallas_call(
        _expert_ffn_body,
        out_shape=jax.ShapeDtypeStruct((e, m, d), F32),
        grid=(e, m // tm, ff // tf),
        in_specs=[pl.BlockSpec((1, tm, d), lambda g, i, f: (g, i, 0)),
                  pl.BlockSpec((1, 1, d, tf), lambda g, i, f: (layer, g, 0, f)),
                  pl.BlockSpec((1, 1, d, tf), lambda g, i, f: (layer, g, 0, f)),
                  pl.BlockSpec((1, 1, tf, d), lambda g, i, f: (layer, g, f, 0)),
                  pl.BlockSpec((1, tm, 1), lambda g, i, f: (g, i, 0))],
        out_specs=pl.BlockSpec((1, tm, d), lambda g, i, f: (g, i, 0)),
        scratch_shapes=[pltpu.VMEM((tm, d), F32)],
        compiler_params=pltpu.CompilerParams(
            dimension_semantics=("parallel", "parallel", "arbitrary"), vmem_limit_bytes=EXPERT_FFN_VMEM_BYTES),
        name="expert_ffn",
    )(xe, w_gate, w_up, w_down, gate)


def _layer_norm(h, g, b):
    hc = h - jnp.mean(h, -1, keepdims=True)
    var = jnp.mean(hc * hc, -1, keepdims=True)
    return hc * lax.rsqrt(var + LN_EPS) * g + b


def _ln_body(x_ref, g_ref, b_ref, o_ref):
    o_ref[...] = _layer_norm(x_ref[...], g_ref[...], b_ref[...])


def _ln_residual_body(x_ref, r_ref, g_ref, b_ref, o_ref):
    o_ref[...] = _layer_norm(ALPHA * x_ref[...] + r_ref[...], g_ref[...], b_ref[...])


def _ln(x, g, b, residual=None, *, tm=1024):
    n, d = x.shape
    tm = min(tm, n)
    assert n % tm == 0
    row = pl.BlockSpec((tm, d), lambda i: (i, 0))
    vec = pl.BlockSpec((1, d), lambda i: (0, 0))
    args = (x,) if residual is None else (x, residual)
    return pl.pallas_call(
        _ln_body if residual is None else _ln_residual_body,
        out_shape=jax.ShapeDtypeStruct((n, d), F32),
        grid=(n // tm,),
        in_specs=[row] * len(args) + [vec, vec],
        out_specs=row,
        compiler_params=pltpu.CompilerParams(dimension_semantics=("parallel",)),
        name="layer_norm",
    )(*args, g[None], b[None])


RWKV_SUB = 16
RWKV_PAIRS = 2
LANES = 128
RWKV_PIECES = 3
RWKV_GAMMA_SLOT = 2 * RWKV_PIECES * RWKV_SUB


def _rwkv_selectors():
    z = np.zeros((RWKV_SUB + 1, LANES, LANES), np.float32)
    for h in range(LANES // HEAD_DIM):
        lanes = slice(h * HEAD_DIM, (h + 1) * HEAD_DIM)
        for p in range(RWKV_PIECES):
            for t in range(RWKV_SUB):
                z[t, (h * RWKV_PIECES + p) * RWKV_SUB + t, lanes] = 1.0
            z[RWKV_SUB, RWKV_GAMMA_SLOT + 8 * h + p, lanes] = 1.0
    return jnp.asarray(z, BF16)


def _split3(x):
    hi = x.astype(BF16).astype(F32)
    rem = x - hi
    mid = rem.astype(BF16).astype(F32)
    lo = (rem - mid).astype(BF16).astype(F32)
    return [hi, mid, lo]


def _operand_tile(xa, xb, extra):
    pa = jnp.concatenate(_split3(xa), axis=0)
    pb = pa if xb is None else jnp.concatenate(_split3(xb), axis=0)
    lane_lo = lax.broadcasted_iota(jnp.int32, pa.shape, 1) < HEAD_DIM
    head0 = jnp.where(lane_lo, pa, pltpu.roll(pb, HEAD_DIM, 1))
    head1 = jnp.where(lane_lo, pltpu.roll(pa, HEAD_DIM, 1), pb)
    tt = jnp.concatenate([head0, head1, extra], axis=0).T
    return tt[:HEAD_DIM].astype(BF16), tt[HEAD_DIM:].astype(BF16)


def _rwkv_prepare(refs, base, lanes, reverse):
    a_ref, b_ref, k_ref, r_ref, g_ref = refs
    rows = pl.ds(base, RWKV_SUB)
    last = 0 if reverse else RWKV_SUB - 1
    g_last = g_ref[0, 0, rows, lanes][last:last + 1]
    gam = jnp.concatenate(_split3(g_last) + [jnp.zeros((8 - RWKV_PIECES, LANES), F32)], axis=0)
    extra = jnp.concatenate([gam, pltpu.roll(gam, HEAD_DIM, 1), jnp.zeros((16, LANES), F32)], axis=0)
    g_a, g_b = _operand_tile(a_ref[0, 0, rows, lanes], b_ref[0, 0, rows, lanes], extra)
    g_k, _ = _operand_tile(k_ref[0, 0, rows, lanes], None, jnp.zeros((32, LANES), F32))
    rg = r_ref[0, 0, rows, lanes]
    pad = jnp.zeros((HEAD_DIM - RWKV_SUB, LANES), F32)
    r_t = jnp.concatenate([rg, pad, pltpu.roll(rg, HEAD_DIM, 1), pad], axis=0).T[:HEAD_DIM]
    return jnp.concatenate([g_a, g_b, g_k], axis=0), r_t


def _rwkv_steps(streams, z_ref, s_ref):
    head_lane = (lax.broadcasted_iota(jnp.int32, (HEAD_DIM, LANES), 1) // HEAD_DIM) * HEAD_DIM
    lanes = [slice(q * LANES, (q + 1) * LANES) for _, _, _, _, _, q, _, _ in streams]
    vs = [s[2][0, pl.ds(s[6], RWKV_SUB), ln] for s, ln in zip(streams, lanes)]
    sts = [s_ref[d, q] for _, _, _, _, d, q, _, _ in streams]
    ys = [[None] * RWKV_SUB for _ in streams]
    gammas = [jnp.dot(s[0][:HEAD_DIM], z_ref[RWKV_SUB], preferred_element_type=F32) for s in streams]
    for p in range(RWKV_SUB // 2):
        cols = []
        for g_all, _, _, _, _, _, _, reverse in streams:
            ts = (RWKV_SUB - 1 - 2 * p, RWKV_SUB - 2 - 2 * p) if reverse else (2 * p, 2 * p + 1)
            sel = jnp.concatenate([z_ref[ts[0]], z_ref[ts[1]]], axis=1)
            cols.append((ts, jnp.dot(g_all, sel, preferred_element_type=F32)))
        for n in range(2):
            for i, (_, r_t, _, _, _, _, _, _) in enumerate(streams):
                ts, c = cols[i]
                t = ts[n]
                a_c, b_c, k_c = (c[o * HEAD_DIM:(o + 1) * HEAD_DIM, n * LANES:(n + 1) * LANES] for o in range(3))
                r_c = jnp.take_along_axis(r_t, head_lane + t, axis=1)
                sa = jnp.sum(sts[i] * a_c, axis=0, keepdims=True)
                sts[i] = sts[i] + b_c * sa + k_c * vs[i][t:t + 1, :]
                ys[i][t] = jnp.sum(sts[i] * r_c, axis=0, keepdims=True)
    for i, (_, _, _, y_ref, d, q, base, _) in enumerate(streams):
        s_ref[d, q] = sts[i] * gammas[i]
        y_ref[0, 0, pl.ds(base, RWKV_SUB), lanes[i]] = jnp.concatenate(ys[i], axis=0)


def _rwkv_body(af, bf, kf, rf, gf, vf, ab, bb, kb, rb, gb, vb, z_ref, yf_ref, yb_ref, s_ref, g_ref,
               rt_ref, *, tb):
    @pl.when(pl.program_id(2) == 0)
    def _():
        s_ref[...] = jnp.zeros_like(s_ref)

    nsub = tb // RWKV_SUB
    refs_f = (af, bf, kf, rf, gf)
    refs_b = (ab, bb, kb, rb, gb)

    def base_f(c):
        return pl.multiple_of(c * RWKV_SUB, RWKV_SUB)

    def base_b(c):
        return pl.multiple_of((nsub - 1 - c) * RWKV_SUB, RWKV_SUB)

    def prepare(slot, c):
        for q in range(RWKV_PAIRS):
            lanes = slice(q * LANES, (q + 1) * LANES)
            g_ref[slot, 0, q], rt_ref[slot, 0, q] = _rwkv_prepare(refs_f, base_f(c), lanes, False)
            g_ref[slot, 1, q], rt_ref[slot, 1, q] = _rwkv_prepare(refs_b, base_b(c), lanes, True)

    prepare(0, 0)

    def sub(c, carry):
        slot = c % 2
        nxt = jnp.minimum(c + 1, nsub - 1)
        streams = []
        for q in range(RWKV_PAIRS):
            streams.append((g_ref[slot, 0, q], rt_ref[slot, 0, q], vf, yf_ref, 0, q, base_f(c), False))
            streams.append((g_ref[slot, 1, q], rt_ref[slot, 1, q], vb, yb_ref, 1, q, base_b(c), True))
        prepare(1 - slot, nxt)
        _rwkv_steps(streams, z_ref, s_ref)
        return carry

    lax.fori_loop(0, nsub, sub, 0)


def _rwkv7_scan(a, b, k, r, g, v, *, tb=512, interpret=False):
    B, T, C = v.shape
    tb = min(tb, T)
    nt = T // tb
    width = RWKV_PAIRS * LANES
    assert T % tb == 0 and tb % RWKV_SUB == 0 and C % width == 0
    fwd3 = pl.BlockSpec((1, tb, width), lambda b, j, i: (b, i, j))
    bwd3 = pl.BlockSpec((1, tb, width), lambda b, j, i: (b, nt - 1 - i, j))
    fwd4 = pl.BlockSpec((1, 1, tb, width), lambda b, j, i: (0, b, i, j))
    bwd4 = pl.BlockSpec((1, 1, tb, width), lambda b, j, i: (1, b, nt - 1 - i, j))
    yf, yb = pl.pallas_call(
        functools.partial(_rwkv_body, tb=tb),
        out_shape=[jax.ShapeDtypeStruct((1, B, T, C), F32)] * 2,
        grid=(B, C // width, nt),
        in_specs=[fwd4] * 5 + [fwd3] + [bwd4] * 5 + [bwd3]
        + [pl.BlockSpec((RWKV_SUB + 1, LANES, LANES), lambda b, j, i: (0, 0, 0))],
        out_specs=[pl.BlockSpec((1, 1, tb, width), lambda b, j, i: (0, b, i, j)),
                   pl.BlockSpec((1, 1, tb, width), lambda b, j, i: (0, b, nt - 1 - i, j))],
        scratch_shapes=[pltpu.VMEM((2, RWKV_PAIRS, HEAD_DIM, LANES), F32),
                        pltpu.VMEM((2, 2, RWKV_PAIRS, 3 * HEAD_DIM, LANES), BF16),
                        pltpu.VMEM((2, 2, RWKV_PAIRS, HEAD_DIM, LANES), F32)],
        compiler_params=pltpu.CompilerParams(dimension_semantics=("parallel", "parallel", "arbitrary")),
        name="rwkv7_scan",
        interpret=interpret,
    )(a, b, k, r, g, v, a, b, k, r, g, v, _rwkv_selectors())
    return yf, yb


HALO = 8


def _head_sums(x, bd_ref):
    bd = bd_ref[...]
    return sum(jnp.dot(piece.astype(BF16), bd, preferred_element_type=F32) for piece in _split3(x))


def _token_shift(cur, prev_row, next_row, mu):
    tb = cur.shape[0]
    rid = lax.broadcasted_iota(jnp.int32, cur.shape, 0)
    prev = jnp.where(rid == 0, prev_row, pltpu.roll(cur, 1, 0))
    nxt = jnp.where(rid == tb - 1, next_row, pltpu.roll(cur, tb - 1, 0))
    return cur + mu[0:1] * (prev - cur) + mu[1:2] * (nxt - cur)


def _rwkv_prep_body(x_ref, xp_ref, xn_ref, r_ref, rp_ref, rn_ref, k_ref, kp_ref, kn_ref, v_ref, vp_ref, vn_ref,
                    mux_ref, mur_ref, w1_ref, w2_ref, w0_ref, a1_ref, a2_ref, a0_ref, g1_ref, g2_ref,
                    kk_w_ref, ka_w_ref, bd_ref, tri_ref,
                    ro_ref, vo_ref, go_ref, kd_ref, sa_ref, sb_ref, sk_ref, sr_ref, sg_ref):
    i = pl.program_id(1)
    first = i == 0
    last = i == pl.num_programs(1) - 1

    def shifted(cur_ref, p_ref, n_ref, mu):
        prev_row = jnp.where(first, 0.0, p_ref[0, HALO - 1:HALO, :])
        next_row = jnp.where(last, 0.0, n_ref[0, 0:1, :])
        return _token_shift(cur_ref[0], prev_row, next_row, mu)

    r = shifted(r_ref, rp_ref, rn_ref, mur_ref[0])
    k = shifted(k_ref, kp_ref, kn_ref, mur_ref[1])
    v = shifted(v_ref, vp_ref, vn_ref, mur_ref[2])
    xw = shifted(x_ref, xp_ref, xn_ref, mux_ref[0]).astype(BF16)
    xa = shifted(x_ref, xp_ref, xn_ref, mux_ref[1]).astype(BF16)
    xg = shifted(x_ref, xp_ref, xn_ref, mux_ref[2]).astype(BF16)
    hg = jax.nn.sigmoid(jnp.dot(xg, g1_ref[...], preferred_element_type=F32)).astype(BF16)
    go_ref[0] = jnp.dot(hg, g2_ref[...], preferred_element_type=F32)
    kk = k * kk_w_ref[...]
    kk = kk / jnp.maximum(jnp.sqrt(_head_sums(kk * kk, bd_ref)), 1e-12)
    ro_ref[0] = r
    vo_ref[0] = v
    for z in range(2):
        hw = jnp.tanh(jnp.dot(xw, w1_ref[z], preferred_element_type=F32)).astype(BF16)
        ha = jnp.dot(xa, a1_ref[z], preferred_element_type=F32).astype(BF16)
        w_raw = w0_ref[z:z + 1] + jnp.dot(hw, w2_ref[z], preferred_element_type=F32)
        lw = -math.exp(-0.5) * jax.nn.sigmoid(w_raw)
        a = jax.nn.sigmoid(a0_ref[z:z + 1]
                           + jnp.dot(ha, a2_ref[z], preferred_element_type=F32))
        kd = k * (1.0 + (a - 1.0) * ka_w_ref[...])
        kd_ref[z, 0] = kd
        cum = sum(jnp.dot(tri_ref[z], piece.astype(BF16), preferred_element_type=F32) for piece in _split3(lw))
        gamma = jnp.exp(cum)
        inv_gamma = jnp.exp(-cum)
        sa_ref[z, 0] = -kk * jnp.exp(cum - lw)
        sb_ref[z, 0] = kk * a * inv_gamma
        sk_ref[z, 0] = kd * inv_gamma
        sr_ref[z, 0] = r * gamma
        sg_ref[z, 0] = gamma


def _rwkv_post_body(yf_ref, yb_ref, r_ref, v_ref, kd_ref, g_ref, lng_ref, lnb_ref, rk_ref, bd_ref, o_ref):
    y = yf_ref[0, 0] + yb_ref[0, 0]
    inv_n = 1.0 / HEAD_DIM
    yc = y - _head_sums(y, bd_ref) * inv_n
    yn = yc * lax.rsqrt(_head_sums(yc * yc, bd_ref) * inv_n + GN_EPS) * lng_ref[...] + lnb_ref[...]
    r, v = r_ref[0], v_ref[0]
    bonus = (_head_sums(r * kd_ref[0, 0] * rk_ref[...], bd_ref)
             + _head_sums(r * kd_ref[1, 0] * rk_ref[...], bd_ref)) * v
    o_ref[0] = ((yn + bonus) * g_ref[0]).astype(o_ref.dtype)


def _rwkv7_mixer(u, proj, p, *, tb=256):
    B, T, D = u.shape
    C = D_RWKV
    tb = min(tb, T)
    nt = T // tb
    assert T % tb == 0 and tb % HALO == 0
    hb = tb // HALO

    def cur(width, col):
        return pl.BlockSpec((1, tb, width), lambda b, i: (b, i, col))

    def prev(width, col):
        return pl.BlockSpec((1, HALO, width), lambda b, i: (b, jnp.maximum(i * hb - 1, 0), col))

    def nxt(width, col):
        return pl.BlockSpec((1, HALO, width), lambda b, i: (b, jnp.minimum((i + 1) * hb, T // HALO - 1), col))

    def full(a):
        return pl.BlockSpec(a.shape, lambda b, i: (0,) * a.ndim)

    lane = np.arange(C) // HEAD_DIM
    bd = jnp.asarray(lane[:, None] == lane[None, :], BF16)
    t_id = np.arange(tb)
    same = (t_id[:, None] // RWKV_SUB) == (t_id[None, :] // RWKV_SUB)
    tri = jnp.asarray(np.stack([same & (t_id[None, :] <= t_id[:, None]),
                                same & (t_id[None, :] >= t_id[:, None])]), BF16)
    weights = [p['rwkv_mu_x'], p['rwkv_mu_rkv'], p['rwkv_w1'].astype(BF16), p['rwkv_w2'].astype(BF16), p['rwkv_w0'],
               p['rwkv_a1'].astype(BF16), p['rwkv_a2'].astype(BF16), p['rwkv_a0'], p['rwkv_g1'].astype(BF16), p['rwkv_g2'].astype(BF16),
               p['rwkv_k_k'][None], p['rwkv_k_a'][None], bd, tri]
    one = jax.ShapeDtypeStruct((B, T, C), F32)
    two = jax.ShapeDtypeStruct((2, B, T, C), F32)
    out1 = pl.BlockSpec((1, tb, C), lambda b, i: (b, i, 0))
    out2 = pl.BlockSpec((2, 1, tb, C), lambda b, i: (0, b, i, 0))
    r, v, g, kd, sa, sb, sk, sr, sg = pl.pallas_call(
        _rwkv_prep_body,
        out_shape=[one, one, one, two, two, two, two, two, two],
        grid=(B, nt),
        in_specs=[cur(D, 0), prev(D, 0), nxt(D, 0)]
        + [spec(C, col) for col in range(3) for spec in (cur, prev, nxt)]
        + [full(w) for w in weights],
        out_specs=[out1, out1, out1, out2, out2, out2, out2, out2, out2],
        compiler_params=pltpu.CompilerParams(dimension_semantics=("parallel", "parallel")),
        name="rwkv7_prepare",
    )(u, u, u, *([proj] * 9), *weights)
    yf, yb = _rwkv7_scan(sa, sb, sk, sr, sg, v)
    ydir = pl.BlockSpec((1, 1, tb, C), lambda b, i: (0, b, i, 0))
    post_w = [p['rwkv_lnx_g'][None], p['rwkv_lnx_b'][None], p['rwkv_r_k'].reshape(1, C), bd]
    return pl.pallas_call(
        _rwkv_post_body,
        out_shape=jax.ShapeDtypeStruct((B, T, C), BF16),
        grid=(B, nt),
        in_specs=[ydir, ydir, out1, out1, out2, out1] + [full(w) for w in post_w],
        out_specs=out1,
        compiler_params=pltpu.CompilerParams(dimension_semantics=("parallel", "parallel")),
        name="rwkv7_output",
    )(yf, yb, r, v, kd, g, *post_w)


NA_ROWS_PER_STEP = 8


def _na_body(q_ref, k_ref, v_ref, bias_ref, mask_ref, o_ref, *, rows):
    kr = NA_WIN_ROWS
    win = kr * GRID_W
    valid = mask_ref[...] != 0
    lane = lax.broadcasted_iota(jnp.int32, (GRID_W, LANES), 1)
    for ii in range(NA_ROWS_PER_STEP):
        i = pl.program_id(1) * NA_ROWS_PER_STEP + ii
        start = jnp.clip(i - kr // 2, 0, rows - kr)
        d = i - start
        krows = pl.ds(pl.multiple_of(start * GRID_W, GRID_W), win)
        for pair in range(D_NA // LANES):
            lanes = slice(pair * LANES, (pair + 1) * LANES)
            q2 = q_ref[0, ii * GRID_W:(ii + 1) * GRID_W, lanes] * (HEAD_DIM ** -0.5)
            k2 = k_ref[0, krows, lanes]
            v2 = v_ref[0, krows, lanes]
            outs = []
            for hh in range(LANES // HEAD_DIM):
                in_head = (lane >= hh * HEAD_DIM) & (lane < (hh + 1) * HEAD_DIM)
                qh = jnp.where(in_head, q2, 0.0).astype(BF16)
                s = lax.dot_general(qh, k2, (((1,), (1,)), ((), ())), preferred_element_type=F32)
                s = s + bias_ref[d, pair * (LANES // HEAD_DIM) + hh]
                s = jnp.where(valid, s, -1e30)
                m = jnp.max(s, axis=-1, keepdims=True)
                e = jnp.exp(s - m)
                p = e / jnp.sum(e, axis=-1, keepdims=True)
                outs.append(jnp.dot(p.astype(BF16), v2, preferred_element_type=F32))
            o_ref[0, ii * GRID_W:(ii + 1) * GRID_W, lanes] = jnp.where(lane < HEAD_DIM, outs[0], outs[1])


def _neighbourhood_attention_pallas(proj, k, v, rpb, *, interpret=False):
    B, T, _ = proj.shape
    q_col = 3 * D_RWKV // D_NA
    rows = T // GRID_W
    kr, kc = NA_WIN_ROWS, NA_WIN_COLS
    assert rows >= kr and rows % NA_ROWS_PER_STEP == 0
    ci = jnp.arange(GRID_W)
    col_start = jnp.clip(ci - kc // 2, 0, GRID_W - kc)
    col_valid = (ci[None] >= col_start[:, None]) & (ci[None] < col_start[:, None] + kc)
    mask = jnp.tile(col_valid.astype(jnp.int32), (1, kr))
    dc_idx = jnp.clip(ci[None] - ci[:, None] + kc - 1, 0, 2 * kc - 2)
    dr_idx = jnp.arange(kr)[None, :] - jnp.arange(kr)[:, None] + kr - 1
    onehot = (jnp.arange(2 * kc - 1)[:, None] == dc_idx.reshape(1, -1)).astype(F32)
    bias = jnp.dot(rpb[:, dr_idx].reshape(-1, 2 * kc - 1), onehot, precision=lax.Precision.HIGHEST)
    bias = bias.reshape(NA_HEADS, kr, kr, GRID_W, GRID_W)
    bias = jnp.transpose(bias, (1, 0, 3, 2, 4)).reshape(kr, NA_HEADS, GRID_W, kr * GRID_W)
    tq = NA_ROWS_PER_STEP * GRID_W
    return pl.pallas_call(
        functools.partial(_na_body, rows=rows),
        out_shape=jax.ShapeDtypeStruct((B, T, D_NA), F32),
        grid=(B, rows // NA_ROWS_PER_STEP),
        in_specs=[pl.BlockSpec((1, tq, D_NA), lambda b, i: (b, i, q_col)),
                  pl.BlockSpec((1, T, D_NA), lambda b, i: (b, 0, 0)),
                  pl.BlockSpec((1, T, D_NA), lambda b, i: (b, 0, 0)),
                  pl.BlockSpec((kr, NA_HEADS, GRID_W, kr * GRID_W), lambda b, i: (0, 0, 0, 0)),
                  pl.BlockSpec((GRID_W, kr * GRID_W), lambda b, i: (0, 0))],
        out_specs=pl.BlockSpec((1, tq, D_NA), lambda b, i: (b, i, 0)),
        compiler_params=pltpu.CompilerParams(dimension_semantics=("parallel", "arbitrary")),
        name="neighbourhood_attention",
        interpret=interpret,
    )(proj, k, v, bias, mask)


def _hyena_filters(L, p):
    t = jnp.linspace(0.0, 1.0, L, dtype=F32)[:, None]
    n_bands = (HY_EMB - 1) // 2
    omega = 2.0 * math.pi * jnp.arange(L, dtype=F32)[:, None] / L
    bands = jnp.linspace(1e-4, n_bands - 1, n_bands, dtype=F32)[None]
    z = jnp.concatenate([t, jnp.cos(bands * omega), -jnp.sin(bands * omega)], -1)
    freq = p['hy_freq']
    h = jnp.sin(freq * (z @ p['hy_w1'] + p['hy_b1']))
    h = jnp.sin(freq * (h @ p['hy_w2'] + p['hy_b2']))
    h = jnp.sin(freq * (h @ p['hy_w3'] + p['hy_b3']))
    h = (h @ p['hy_w4']).reshape(L, HY_ORDER, 2, D_HY)
    deltas = jnp.abs(jnp.linspace(math.log(HY_TOL) / HY_FAST_PCT, math.log(HY_TOL) / HY_SLOW_PCT, D_HY, dtype=F32))
    h = h * jnp.exp(-t * deltas)[:, None, None, :]
    first_row = lax.broadcasted_iota(jnp.int32, (L, 1, 1, 1), 0) == 0
    halves = jnp.where(first_row & (lax.broadcasted_iota(jnp.int32, (1, 1, 2, 1), 2) == 1), 0.0, h)
    norm = jnp.sum(jnp.abs(halves), axis=(0, 2), keepdims=True)
    return jnp.transpose(halves / norm, (0, 2, 1, 3))


def _dft_tables(R):
    n = np.arange(R)
    ang = (2.0 * np.pi / R) * ((n[:, None] * n[None, :]) % R)
    tw = (2.0 * np.pi / (R * R)) * (n[:, None] * n[None, :])
    c, s, ct, st = (jnp.asarray(f(a), F32) for a in (ang, tw) for f in (np.cos, np.sin))
    first = jnp.concatenate([c, -s], axis=0)
    last = jnp.concatenate([c, -s], axis=1)[:R // 2]
    cp = c[None] * ct[:, None, :] - s[None] * st[:, None, :]
    sp = s[None] * ct[:, None, :] + c[None] * st[:, None, :]
    fwd = jnp.concatenate([jnp.concatenate([cp, sp], axis=2), jnp.concatenate([-sp, cp], axis=2)], axis=1)
    cpt = c[None] * ct[:, :, None] - s[None] * st[:, :, None]
    spt = s[None] * ct[:, :, None] + c[None] * st[:, :, None]
    inv = jnp.concatenate([jnp.concatenate([cpt, -spt], axis=2), jnp.concatenate([spt, cpt], axis=2)], axis=1)
    return first.astype(BF16), fwd.astype(BF16), inv.astype(BF16), last.astype(BF16)


def _conv_first_body(f_ref, z_ref, o_ref):
    o_ref[0] = jnp.dot(f_ref[...], z_ref[0].astype(BF16), preferred_element_type=F32).astype(o_ref.dtype)


def _spectrum_mid_body(a_ref, mf_ref, o_ref, *, R, kb):
    c = o_ref.shape[-1]
    for j in range(kb):
        a = jnp.concatenate([a_ref[0, 0, j], a_ref[0, 1, j]], axis=0)
        x = jnp.dot(mf_ref[j], a, preferred_element_type=F32)
        o_ref[0, j] = x[:R, :c] + x[:R, c:]
        o_ref[1, j] = x[R:, :c] - x[R:, c:]


def _filter_spectrum(halves, tables, *, tn=4096, kb=4):
    L, c2 = halves.shape
    R = math.isqrt(2 * L)
    first, fwd, _, _ = tables
    tn = min(tn, R * c2)
    a = pl.pallas_call(
        _conv_first_body,
        out_shape=jax.ShapeDtypeStruct((1, 2 * R, R * c2), BF16),
        grid=(1, (R * c2) // tn),
        in_specs=[pl.BlockSpec((2 * R, R // 2), lambda b, j: (0, 0)),
                  pl.BlockSpec((1, R // 2, tn), lambda b, j: (b, 0, j))],
        out_specs=pl.BlockSpec((1, 2 * R, tn), lambda b, j: (b, 0, j)),
        name="filter_dft_first",
    )(first[:, :R // 2], halves.reshape(1, R // 2, R * c2))
    return pl.pallas_call(
        functools.partial(_spectrum_mid_body, R=R, kb=kb),
        out_shape=jax.ShapeDtypeStruct((2, R, R, c2 // 2), F32),
        grid=(R // kb,),
        in_specs=[pl.BlockSpec((1, 2, kb, R, c2), lambda k: (0, 0, k, 0, 0)),
                  pl.BlockSpec((kb, 2 * R, 2 * R), lambda k: (k, 0, 0))],
        out_specs=pl.BlockSpec((2, kb, R, c2 // 2), lambda k: (0, k, 0, 0)),
        name="filter_dft_mid",
    )(a.reshape(1, 2, R, R, c2), fwd)


def _conv_mid_body(a_ref, mf_ref, mi_ref, h_ref, o_ref, *, R, kb):
    for j in range(kb):
        a = jnp.concatenate([a_ref[0, 0, j], a_ref[0, 1, j]], axis=0)
        x = jnp.dot(mf_ref[j], a, preferred_element_type=F32)
        xr, xi = x[:R], x[R:]
        hr, hi = h_ref[0, j], h_ref[1, j]
        y = jnp.concatenate([xr * hr - xi * hi, xr * hi + xi * hr], axis=0).astype(BF16)
        b = jnp.dot(mi_ref[j], y, preferred_element_type=F32)
        o_ref[0, 0, j] = b[:R].astype(o_ref.dtype)
        o_ref[0, 1, j] = b[R:].astype(o_ref.dtype)


def _conv_last_body(f_ref, b_ref, z_ref, gate_ref, bias_ref, o_ref):
    y = jnp.dot(f_ref[...], b_ref[0], preferred_element_type=F32)
    o_ref[0] = gate_ref[0] * (y + z_ref[0] * bias_ref[...])


def _gated_long_conv(z, gate, hm, order, bias, tables, *, tn=4096, kb=4, interpret=False):
    B, L, C = z.shape
    R = math.isqrt(2 * L)
    assert R * R == 2 * L and R % (2 * kb) == 0
    first, fwd, inv, last = tables
    tn = min(tn, R * C)
    nt = (R * C) // tn
    z2 = z.reshape(B, R // 2, R * C)
    a = pl.pallas_call(
        _conv_first_body,
        out_shape=jax.ShapeDtypeStruct((B, 2 * R, R * C), BF16),
        grid=(B, nt),
        in_specs=[pl.BlockSpec((2 * R, R // 2), lambda b, j: (0, 0)),
                  pl.BlockSpec((1, R // 2, tn), lambda b, j: (b, 0, j))],
        out_specs=pl.BlockSpec((1, 2 * R, tn), lambda b, j: (b, 0, j)),
        name="long_conv_first",
        interpret=interpret,
    )(first[:, :R // 2], z2)
    bmid = pl.pallas_call(
        functools.partial(_conv_mid_body, R=R, kb=kb),
        out_shape=jax.ShapeDtypeStruct((B, 2, R, R, C), BF16),
        grid=(B, R // kb),
        in_specs=[pl.BlockSpec((1, 2, kb, R, C), lambda b, k: (b, 0, k, 0, 0)),
                  pl.BlockSpec((kb, 2 * R, 2 * R), lambda b, k: (k, 0, 0)),
                  pl.BlockSpec((kb, 2 * R, 2 * R), lambda b, k: (k, 0, 0)),
                  pl.BlockSpec((2, kb, R, C), lambda b, k: (0, k, 0, order))],
        out_specs=pl.BlockSpec((1, 2, kb, R, C), lambda b, k: (b, 0, k, 0, 0)),
        name="long_conv_mid",
        interpret=interpret,
    )(a.reshape(B, 2, R, R, C), fwd, inv, hm)
    out = pl.pallas_call(
        _conv_last_body,
        out_shape=jax.ShapeDtypeStruct((B, R // 2, R * C), F32),
        grid=(B, nt),
        in_specs=[pl.BlockSpec((R // 2, 2 * R), lambda b, j: (0, 0)),
                  pl.BlockSpec((1, 2 * R, tn), lambda b, j: (b, 0, j)),
                  pl.BlockSpec((1, R // 2, tn), lambda b, j: (b, 0, j)),
                  pl.BlockSpec((1, R // 2, tn), lambda b, j: (b, 0, j)),
                  pl.BlockSpec((1, tn), lambda b, j: (0, j))],
        out_specs=pl.BlockSpec((1, R // 2, tn), lambda b, j: (b, 0, j)),
        name="long_conv_last",
        interpret=interpret,
    )(last, bmid.reshape(B, 2 * R, R * C), z2, gate.reshape(B, R // 2, R * C), jnp.tile(bias, R)[None])
    return out.reshape(B, L, C)


def _short_conv_body(u_ref, up_ref, un_ref, nk_ref, nv_ref, sw_ref, sb_ref, x1_ref, x2_ref, v_ref, ko_ref, vo_ref):
    i = pl.program_id(1)
    cur = u_ref[0]
    tb = cur.shape[0]
    prev_row = jnp.where(i == 0, 0.0, up_ref[0, HALO - 1:HALO, :])
    next_row = jnp.where(i == pl.num_programs(1) - 1, 0.0, un_ref[0, 0:1, :])
    rid = lax.broadcasted_iota(jnp.int32, cur.shape, 0)
    prev = jnp.where(rid == 0, prev_row, pltpu.roll(cur, 1, 0))
    nxt = jnp.where(rid == tb - 1, next_row, pltpu.roll(cur, tb - 1, 0))
    u = prev * sw_ref[0:1] + cur * sw_ref[1:2] + nxt * sw_ref[2:3] + sb_ref[...]
    x1_ref[0] = u[:, :D_HY]
    x2_ref[0] = u[:, D_HY:2 * D_HY]
    v_ref[0] = u[:, 2 * D_HY:]
    ko_ref[0] = nk_ref[0].astype(BF16)
    vo_ref[0] = nv_ref[0].astype(BF16)


def _split_projection(proj, sw, sb, *, tb=512):
    B, T, n_in = proj.shape
    width = 3 * D_HY
    col = (3 * D_RWKV + 3 * D_NA) // width
    assert col * width == 3 * D_RWKV + 3 * D_NA
    tb = min(tb, T)
    hb = tb // HALO
    out = jax.ShapeDtypeStruct((B, T, D_HY), F32)
    out_na = jax.ShapeDtypeStruct((B, T, D_NA), BF16)
    ospec = pl.BlockSpec((1, tb, D_HY), lambda b, i: (b, i, 0))
    na_spec = pl.BlockSpec((1, tb, D_NA), lambda b, i: (b, i, 0))
    na_col = 3 * D_RWKV // D_NA
    return pl.pallas_call(
        _short_conv_body,
        out_shape=[out, out, out, out_na, out_na],
        grid=(B, T // tb),
        in_specs=[pl.BlockSpec((1, tb, width), lambda b, i: (b, i, col)),
                  pl.BlockSpec((1, HALO, width), lambda b, i: (b, jnp.maximum(i * hb - 1, 0), col)),
                  pl.BlockSpec((1, HALO, width), lambda b, i: (b, jnp.minimum((i + 1) * hb, T // HALO - 1), col)),
                  pl.BlockSpec((1, tb, D_NA), lambda b, i: (b, i, na_col + 1)),
                  pl.BlockSpec((1, tb, D_NA), lambda b, i: (b, i, na_col + 2)),
                  pl.BlockSpec(sw.shape, lambda b, i: (0, 0)),
                  pl.BlockSpec((1, width), lambda b, i: (0, 0))],
        out_specs=[ospec, ospec, ospec, na_spec, na_spec],
        compiler_params=pltpu.CompilerParams(dimension_semantics=("parallel", "parallel")),
        name="split_projection",
    )(proj, proj, proj, proj, proj, sw, sb[None])


def _hyena_mixer(x1, x2, v, p):
    T = v.shape[1]
    R = math.isqrt(2 * T)
    tables = _dft_tables(R)
    halves = _hyena_filters(T, p).reshape(T, 2 * HY_ORDER * D_HY) * (1.0 / (2 * T))
    hm = _filter_spectrum(halves, tables)
    z = v
    for o, gate in enumerate((x1, x2)):
        z = _gated_long_conv(z, gate, hm, o, p['hy_bias'][o], tables)
    return z


def _route(aff):
    n = aff.shape[0]
    cap = (CAPACITY_FACTOR * n) // N_EXPERTS
    gate, idx = lax.top_k(aff.T, cap)
    idx, gate = lax.sort((idx, gate), dimension=1, num_keys=1)
    return gate, idx


def _mix_body(x_ref, ya_ref, yb_ref, yc_ref, g_ref, wa_ref, wb_ref, wc_ref, wo_ref, lg_ref, lb_ref, wr_ref,
              o_ref, ob_ref, aff_ref):
    g = jax.nn.sigmoid(g_ref[...])
    m = (g[:, :D_MODEL] * jnp.dot(ya_ref[...], wa_ref[...], preferred_element_type=F32)
         + g[:, D_MODEL:2 * D_MODEL] * jnp.dot(yb_ref[...].astype(BF16), wb_ref[...], preferred_element_type=F32)
         + g[:, 2 * D_MODEL:] * jnp.dot(yc_ref[...].astype(BF16), wc_ref[...], preferred_element_type=F32))
    h = ALPHA * x_ref[...] + jnp.dot(m.astype(BF16), wo_ref[...], preferred_element_type=F32)
    y = _layer_norm(h, lg_ref[...], lb_ref[...])
    o_ref[...] = y
    ob_ref[...] = y.astype(BF16)
    logits = jnp.dot(y, wr_ref[...], precision=lax.Precision.HIGHEST, preferred_element_type=F32)
    e = jnp.exp(logits - jnp.max(logits, -1, keepdims=True))
    aff_ref[...] = e / jnp.sum(e, -1, keepdims=True)


def _mix_and_norm(x, y_a, y_b, y_c, proj, p, *, tm=512):
    n, d = x.shape
    assert n % tm == 0 and proj.shape[1] == 2 * N_BRANCH * d
    row = lambda w: pl.BlockSpec((tm, w), lambda i: (i, 0))
    full = lambda a: pl.BlockSpec(a.shape, lambda i: (0,) * a.ndim)
    wa, wb, wc, wo = (p[k].astype(BF16) for k in ('w_branch_a', 'w_branch_b', 'w_branch_c', 'w_out'))
    lg, lb = p['ln1_g'][None], p['ln1_b'][None]
    return pl.pallas_call(
        _mix_body,
        out_shape=[jax.ShapeDtypeStruct((n, d), F32), jax.ShapeDtypeStruct((n, d), BF16),
                   jax.ShapeDtypeStruct((n, N_EXPERTS), F32)],
        grid=(n // tm,),
        in_specs=[row(d), row(D_RWKV), row(D_NA), row(D_HY),
                  pl.BlockSpec((tm, N_BRANCH * d), lambda i: (i, 1)),
                  full(wa), full(wb), full(wc), full(wo), full(lg), full(lb), full(p['w_router'])],
        out_specs=[row(d), row(d), row(N_EXPERTS)],
        compiler_params=pltpu.CompilerParams(dimension_semantics=("parallel",)),
        name="mix_norm_route",
    )(x, y_a, y_b, y_c, proj, wa, wb, wc, wo, lg, lb, p['w_router'])


def _mixers(x, p):
    B, T, D = x.shape
    n = B * T
    proj2 = _matmul(x.reshape(n, D), p['w_in'].astype(BF16))
    proj = proj2.reshape(B, T, -1)
    x1c, x2c, vc, kb, vb = _split_projection(proj, p['hy_short_w'], p['hy_short_b'])
    y_a = _rwkv7_mixer(x, proj, p)
    y_b = _neighbourhood_attention_pallas(proj, kb, vb, p['na_rpb'])
    y_c = _hyena_mixer(x1c, x2c, vc, p)
    return _mix_and_norm(x.reshape(n, D), y_a.reshape(n, -1), y_b.reshape(n, -1), y_c.reshape(n, -1), proj2, p)


def _encoder_layer(xs, p, expert_weights, layer):
    mixed = [_mixers(x, p) for x in xs]
    routes = [_route(aff) for _, _, aff in mixed]
    xe = jnp.concatenate([jnp.take(xb, idx, axis=0) for (_, xb, _), (_, idx) in zip(mixed, routes)], axis=1)
    gate = jnp.concatenate([g for g, _ in routes], axis=1)[..., None]
    ye = _expert_ffn(xe, gate, *expert_weights, layer)
    outs, off = [], 0
    for x, (x1, _, _), (_, idx) in zip(xs, mixed, routes):
        cap = idx.shape[1]
        ffn = jnp.zeros_like(x1).at[idx.reshape(-1)].add(ye[:, off:off + cap].reshape(-1, x1.shape[-1]))
        off += cap
        outs.append(_ln(x1, p['ln2_g'], p['ln2_b'], residual=ffn).reshape(x.shape))
    return outs


def kernel(x_prompt, x_sample, ln_in_g, ln_in_b, w_in, rwkv_mu_rkv, rwkv_mu_x, rwkv_w0, rwkv_w1, rwkv_w2,
           rwkv_a0, rwkv_a1, rwkv_a2, rwkv_g1, rwkv_g2, rwkv_k_k, rwkv_k_a, rwkv_r_k, rwkv_lnx_g, rwkv_lnx_b,
           na_rpb, hy_short_w, hy_short_b, hy_w1, hy_b1, hy_w2, hy_b2, hy_w3, hy_b3, hy_w4, hy_freq, hy_bias,
           w_branch_a, w_branch_b, w_branch_c, w_out, ln1_g, ln1_b, w_router, w_exp_gate, w_exp_up,
           w_exp_down, ln2_g, ln2_b):
    stacked = {
        'w_in': w_in, 'rwkv_mu_rkv': rwkv_mu_rkv, 'rwkv_mu_x': rwkv_mu_x, 'rwkv_w0': rwkv_w0,
        'rwkv_w1': rwkv_w1, 'rwkv_w2': rwkv_w2, 'rwkv_a0': rwkv_a0, 'rwkv_a1': rwkv_a1, 'rwkv_a2': rwkv_a2,
        'rwkv_g1': rwkv_g1, 'rwkv_g2': rwkv_g2, 'rwkv_k_k': rwkv_k_k, 'rwkv_k_a': rwkv_k_a,
        'rwkv_r_k': rwkv_r_k, 'rwkv_lnx_g': rwkv_lnx_g, 'rwkv_lnx_b': rwkv_lnx_b, 'na_rpb': na_rpb,
        'hy_short_w': hy_short_w, 'hy_short_b': hy_short_b, 'hy_w1': hy_w1, 'hy_b1': hy_b1,
        'hy_w2': hy_w2, 'hy_b2': hy_b2, 'hy_w3': hy_w3, 'hy_b3': hy_b3, 'hy_w4': hy_w4,
        'hy_freq': hy_freq, 'hy_bias': hy_bias, 'w_branch_a': w_branch_a, 'w_branch_b': w_branch_b,
        'w_branch_c': w_branch_c, 'w_out': w_out, 'ln1_g': ln1_g, 'ln1_b': ln1_b, 'w_router': w_router,
        'w_exp_gate': w_exp_gate, 'w_exp_up': w_exp_up, 'w_exp_down': w_exp_down,
        'ln2_g': ln2_g, 'ln2_b': ln2_b,
    }
    xs = [_ln(x.reshape(-1, D_MODEL), ln_in_g, ln_in_b).reshape(x.shape) for x in (x_prompt, x_sample)]
    expert_weights = [stacked.pop(name) for name in ('w_exp_gate', 'w_exp_up', 'w_exp_down')]
    for l in range(DEPTH):
        xs = _encoder_layer(xs, {name: arr[l] for name, arr in stacked.items()}, expert_weights, l)
    return tuple(xs)
```
